```python
import jax
import jax.numpy as jnp
from jax import lax
import numpy as np

D_MODEL = 2048
BATCH = 4
SEQ = 4096
DEPTH = 2

GRID_W = 64
CTX_LEN = 256
BLOCK = 128
HEAD_DIM = 128
BRANCH_WIDTH = 1024
RET_HEADS = 4
RET_DK = 128
RET_DV = 256
GA_HEADS = 8
GA_KV_HEADS = 2
WA_HEADS = 8
WA_KV_HEADS = 2
WINDOW = 128
N_BRANCHES = 3
N_EXPERTS = 32
TOP_K = 4
EXPERT_FF = 2048
SWIGLU_LIMIT = 7.0
SWIGLU_ALPHA = 1.702
ROPE_THETA = 10000.0
NORM_EPS = 1e-6
IN_SPLITS = (RET_HEADS * RET_DK, RET_HEADS * RET_DK, RET_HEADS * RET_DV, RET_HEADS * RET_DV,
             GA_HEADS * HEAD_DIM, GA_KV_HEADS * HEAD_DIM, GA_KV_HEADS * HEAD_DIM,
             WA_HEADS * HEAD_DIM, WA_KV_HEADS * HEAD_DIM, WA_KV_HEADS * HEAD_DIM,
             N_BRANCHES * D_MODEL)
IN_COLS = sum(IN_SPLITS)

kernel_name = 'hybrid_retention_gqa_swa_moe_dit'


def rms_norm(x, gain):
    xf = x.astype(jnp.float32)
    y = xf * lax.rsqrt(jnp.mean(xf * xf, axis=-1, keepdims=True) + NORM_EPS)
    return (y * gain.astype(jnp.float32)).astype(x.dtype)


def modulate(h, shift, scale):
    return h * (1.0 + scale) + shift


def modulation(cond, w_ada, b_ada):
    return jnp.split(jax.nn.silu(cond) @ w_ada + b_ada, 6, axis=-1)


def split_cols(u):
    parts, start = [], 0
    for size in IN_SPLITS:
        parts.append(u[..., start:start + size])
        start += size
    return parts


def axial_rope_tables(n_tokens):
    rows = n_tokens // GRID_W
    row = jnp.repeat(jnp.arange(rows, dtype=jnp.float32), GRID_W)
    col = jnp.tile(jnp.arange(GRID_W, dtype=jnp.float32), rows)
    quarter = HEAD_DIM // 4
    inv_freq = ROPE_THETA ** (-jnp.arange(quarter, dtype=jnp.float32) / quarter)
    ang = jnp.concatenate([row[:, None] * inv_freq, col[:, None] * inv_freq], axis=-1)
    return jnp.cos(ang), jnp.sin(ang)


def apply_rope(t, cos, sin):
    tf = t.astype(jnp.float32).reshape(t.shape[:-1] + (t.shape[-1] // 2, 2))
    a, b = tf[..., 0], tf[..., 1]
    out = jnp.stack([a * cos - b * sin, a * sin + b * cos], axis=-1)
    return out.reshape(t.shape).astype(t.dtype)


def split_heads(t, n_heads):
    b, n, _ = t.shape
    return t.reshape(b, n, n_heads, -1).transpose(0, 2, 1, 3)


def split_gqa(t, n_heads, n_kv):
    b, n, _ = t.shape
    return t.reshape(b, n, n_kv, n_heads // n_kv, -1).transpose(0, 2, 3, 1, 4)


def merge_gqa(t, dtype):
    b, k, g, n, d = t.shape
    return t.transpose(0, 3, 1, 2, 4).reshape(b, n, k * g * d).astype(dtype)


def head_rms(t, gain):
    tf = t.astype(jnp.float32)
    y = tf * lax.rsqrt(jnp.mean(tf * tf, axis=-1, keepdims=True) + NORM_EPS)
    return (y * gain.astype(jnp.float32)).astype(t.dtype)


def retention_chunked(q, k, v, log_g, state0, inclusive):
    b, h, n, _ = q.shape
    dv = v.shape[-1]
    nc = n // BLOCK
    pos = jnp.arange(BLOCK, dtype=jnp.float32)
    diff = pos[:, None] - pos[None, :]
    allowed = (diff >= 0) if inclusive else (diff > 0)
    lg = log_g.astype(jnp.float32)[:, None]
    intra = jnp.where(allowed, jnp.exp(jnp.where(allowed, diff, 0.0) * lg[:, :, None]), 0.0)
    q_dec = jnp.exp((pos + 1.0) * lg)[:, :, None]
    k_dec = jnp.exp((BLOCK - 1.0 - pos) * lg)[:, :, None]
    chunk_dec = jnp.exp(BLOCK * lg)[:, :, None]

    def to_chunks(t):
        return t.reshape(b, h, nc, BLOCK, t.shape[-1]).transpose(2, 0, 1, 3, 4)

    def step(state, inp):
        qc, kc, vc = inp
        scores = jnp.einsum('bhid,bhjd->bhij', qc, kc) * intra
        out = (jnp.einsum('bhij,bhjv->bhiv', scores, vc)
               + jnp.einsum('bhid,bhdv->bhiv', qc * q_dec, state))
        state = state * chunk_dec + jnp.einsum('bhjd,bhjv->bhdv', kc * k_dec, vc)
        return state, out

    state, outs = lax.scan(step, state0, (to_chunks(q), to_chunks(k), to_chunks(v)))
    return state, outs.transpose(1, 2, 0, 3, 4).reshape(b, h, n, dv)


def retention_state(k, v, log_g):
    n = k.shape[2]
    w = jnp.exp((n - 1.0 - jnp.arange(n, dtype=jnp.float32)) * log_g[:, None])
    return jnp.einsum('bhnd,bhnv->bhdv', k.astype(jnp.float32) * w[None, :, :, None],
                      v.astype(jnp.float32))


def flip_seq(t):
    return jnp.flip(t, axis=2)


def bidirectional_retention(q, k, v, log_f, log_b, state_f, state_b):
    _, o_f = retention_chunked(q, k, v, log_f, state_f, True)
    _, o_b = retention_chunked(flip_seq(q), flip_seq(k), flip_seq(v), log_b, state_b, False)
    return o_f + flip_seq(o_b)


def retention_readout(o, g):
    b, h, n, dv = o.shape
    o = o * lax.rsqrt(jnp.mean(o * o, axis=-1, keepdims=True) + NORM_EPS)
    o = o.transpose(0, 2, 1, 3).reshape(b, n, h * dv)
    return (o * jax.nn.silu(g.astype(jnp.float32))).astype(g.dtype)


def retention_mixer(qc, kc, vc, gc, qx, kx, vx, gx, decay_logit, cos, sin, need_ctx):
    log_g = jax.nn.log_sigmoid(decay_logit.astype(jnp.float32))
    log_f, log_b = log_g[0], log_g[1]
    k_scale = RET_DK ** -0.5
    q_x = apply_rope(split_heads(qx, RET_HEADS), cos, sin)
    k_x = apply_rope(split_heads(kx, RET_HEADS), cos, sin) * k_scale
    v_x = split_heads(vx, RET_HEADS)
    q_c = split_heads(qc, RET_HEADS)
    k_c = split_heads(kc, RET_HEADS) * k_scale
    v_c = split_heads(vc, RET_HEADS)
    state_f = retention_state(k_c, v_c, log_f)
    state_b = retention_state(flip_seq(k_c), flip_seq(v_c), log_b)
    out_x = retention_readout(bidirectional_retention(q_x, k_x, v_x, log_f, log_b, state_f, state_b), gx)
    out_c = None
    if need_ctx:
        zero = jnp.zeros_like(state_f)
        out_c = retention_readout(bidirectional_retention(q_c, k_c, v_c, log_f, log_b, zero, zero), gc)
    return out_c, out_x


def global_attention(qc, kc, vc, qx, kx, vx, q_gain, k_gain, cos, sin, need_ctx):
    scale = HEAD_DIM ** -0.5
    q_x = apply_rope(head_rms(split_gqa(qx, GA_HEADS, GA_KV_HEADS), q_gain), cos, sin)
    k_x = apply_rope(head_rms(split_heads(kx, GA_KV_HEADS), k_gain), cos, sin)
    v_x = split_heads(vx, GA_KV_HEADS)
    k_c = head_rms(split_heads(kc, GA_KV_HEADS), k_gain)
    v_c = split_heads(vc, GA_KV_HEADS)
    k_all = jnp.concatenate([k_c, k_x], axis=2)
    v_all = jnp.concatenate([v_c, v_x], axis=2)
    b, hk, g, n, d = q_x.shape
    nb = n // BLOCK
    q_blocks = q_x.reshape(b, hk, g, nb, BLOCK, d).transpose(3, 0, 1, 2, 4, 5)

    def attend(qb):
        s = jnp.einsum('bkgqd,bknd->bkgqn', qb, k_all).astype(jnp.float32) * scale
        p = jax.nn.softmax(s, axis=-1)
        return jnp.einsum('bkgqn,bknd->bkgqd', p, v_all)

    out = lax.map(attend, q_blocks).transpose(1, 2, 3, 0, 4, 5).reshape(b, hk, g, n, d)
    out_x = merge_gqa(out, vx.dtype)
    out_c = None
    if need_ctx:
        q_c = head_rms(split_gqa(qc, GA_HEADS, GA_KV_HEADS), q_gain)
        s = jnp.einsum('bkgqd,bkmd->bkgqm', q_c, k_c).astype(jnp.float32) * scale
        p = jax.nn.softmax(s, axis=-1)
        out_c = merge_gqa(jnp.einsum('bkgqm,bkmd->bkgqd', p, v_c), vc.dtype)
    return out_c, out_x


def window_attention(qc, kc, vc, qx, kx, vx, sink, cos, sin, need_ctx):
    scale = HEAD_DIM ** -0.5
    q_x = apply_rope(split_gqa(qx, WA_HEADS, WA_KV_HEADS), cos, sin)
    k_x = apply_rope(split_heads(kx, WA_KV_HEADS), cos, sin)
    v_x = split_heads(vx, WA_KV_HEADS)
    k_c = split_heads(kc, WA_KV_HEADS)
    v_c = split_heads(vc, WA_KV_HEADS)
    b, hk, g, n, d = q_x.shape
    m = k_c.shape[2]
    nb = n // BLOCK
    sink_l = sink.astype(jnp.float32).reshape(hk, g)

    def band(t):
        pad = jnp.pad(t, ((0, 0), (0, 0), (BLOCK, BLOCK), (0, 0))).reshape(b, hk, nb + 2, BLOCK, d)
        return jnp.concatenate([pad[:, :, :-2], pad[:, :, 1:-1], pad[:, :, 2:]], axis=3)

    k_band, v_band = band(k_x), band(v_x)
    qb = q_x.reshape(b, hk, g, nb, BLOCK, d)
    key_idx = jnp.arange(3 * BLOCK)
    rel = (key_idx - BLOCK)[None, :] - jnp.arange(BLOCK)[:, None]
    key_pos = (jnp.arange(nb)[:, None] - 1) * BLOCK + key_idx[None, :]
    allowed = (jnp.abs(rel) <= WINDOW)[None] & ((key_pos >= 0) & (key_pos < n))[:, None, :]
    s_win = jnp.einsum('bkgnqd,bkncd->bkgnqc', qb, k_band).astype(jnp.float32) * scale
    s_win = jnp.where(allowed, s_win, -jnp.inf)
    s_ctx = jnp.einsum('bkgnqd,bkmd->bkgnqm', qb, k_c).astype(jnp.float32) * scale
    s_sink = jnp.broadcast_to(sink_l[None, :, :, None, None, None], s_win.shape[:-1] + (1,))
    p = jax.nn.softmax(jnp.concatenate([s_win, s_ctx, s_sink], axis=-1), axis=-1)
    out = (jnp.einsum('bkgnqc,bkncd->bkgnqd', p[..., :3 * BLOCK], v_band)
           + jnp.einsum('bkgnqm,bkmd->bkgnqd', p[..., 3 * BLOCK:3 * BLOCK + m], v_c))
    out_x = merge_gqa(out.reshape(b, hk, g, n, d), vx.dtype)
    out_c = None
    if need_ctx:
        q_c = split_gqa(qc, WA_HEADS, WA_KV_HEADS)
        s = jnp.einsum('bkgqd,bkmd->bkgqm', q_c, k_c).astype(jnp.float32) * scale
        s_sink_c = jnp.broadcast_to(sink_l[None, :, :, None, None], s.shape[:-1] + (1,))
        p = jax.nn.softmax(jnp.concatenate([s, s_sink_c], axis=-1), axis=-1)
        out_c = merge_gqa(jnp.einsum('bkgqm,bkmd->bkgqd', p[..., :m], v_c), vc.dtype)
    return out_c, out_x


def hybrid_mixer(hc, hx, w_in, decay_logit, q_gain, k_gain, sink, w_branch, w_out, cos, sin, need_ctx):
    uc = split_cols(hc @ w_in)
    ux = split_cols(hx @ w_in)
    ret = retention_mixer(*uc[0:4], *ux[0:4], decay_logit, cos, sin, need_ctx)
    ga = global_attention(*uc[4:7], *ux[4:7], q_gain, k_gain, cos, sin, need_ctx)
    wa = window_attention(*uc[7:10], *ux[7:10], sink, cos, sin, need_ctx)

    def merge(branches, gate_cols):
        terms = [jax.nn.sigmoid(gate_cols[..., i * D_MODEL:(i + 1) * D_MODEL]) * (o @ w_branch[i])
                 for i, o in enumerate(branches)]
        return (terms[0] + terms[1] + terms[2]) @ w_out

    out_x = merge((ret[1], ga[1], wa[1]), ux[10])
    out_c = merge((ret[0], ga[0], wa[0]), uc[10]) if need_ctx else None
    return out_c, out_x


def moe_ffn(h, w_router, b_router, w_gu, b_gu, w_down, b_down):
    b, n, d = h.shape
    t = h.reshape(b * n, d)
    logits = (t @ w_router + b_router).astype(jnp.float32)
    vals, idx = lax.top_k(logits, TOP_K)
    wts = jax.nn.softmax(vals, axis=-1)
    gates = jnp.sum(jax.nn.one_hot(idx, N_EXPERTS, dtype=jnp.float32) * wts[..., None], axis=1)
    y = jnp.zeros((b * n, d), jnp.float32)
    for e in range(N_EXPERTS):
        gu = t @ w_gu[e] + b_gu[e]
        x_glu = jnp.minimum(gu[:, :EXPERT_FF], SWIGLU_LIMIT)
        x_lin = jnp.clip(gu[:, EXPERT_FF:], -SWIGLU_LIMIT, SWIGLU_LIMIT)
        act = x_glu * jax.nn.sigmoid(SWIGLU_ALPHA * x_glu) * (x_lin + 1.0)
        y = y + gates[:, e:e + 1] * (act @ w_down[e] + b_down[e])
    return y.reshape(b, n, d).astype(h.dtype)


def setup_inputs(seed: int = 0) -> dict:
    key = jax.random.key(seed)
    ks = jax.random.split(key, 24)
    f32 = jnp.float32
    D, E, F = D_MODEL, N_EXPERTS, EXPERT_FF

    def nrm(k, shape, scale):
        return jax.random.normal(k, shape, f32) * scale

    p = 2.0 ** (-5.0 - jnp.arange(RET_HEADS, dtype=f32))
    decay_base = jnp.log1p(-p) - jnp.log(p)
    return {
        'x': nrm(ks[0], (BATCH, SEQ, D), 1.0),
        'c': nrm(ks[1], (BATCH, D), 1.0),
        'ctx': nrm(ks[2], (BATCH, CTX_LEN, D), 1.0),
        'c_ctx': nrm(ks[3], (D,), 1.0),
        'ln1_g': 1.0 + nrm(ks[4], (DEPTH, D), 0.01),
        'ln2_g': 1.0 + nrm(ks[5], (DEPTH, D), 0.01),
        'w_ada': nrm(ks[6], (DEPTH, D, 6 * D), 0.5 * D ** -0.5),
        'b_ada': nrm(ks[7], (DEPTH, 6 * D), 0.01),
        'w_in': nrm(ks[8], (DEPTH, D, IN_COLS), D ** -0.5),
        'ret_decay_logit': decay_base + nrm(ks[9], (DEPTH, 2, RET_HEADS), 0.05),
        'ga_q_gain': 1.0 + nrm(ks[10], (DEPTH, HEAD_DIM), 0.01),
        'ga_k_gain': 1.0 + nrm(ks[11], (DEPTH, HEAD_DIM), 0.01),
        'wa_sink': nrm(ks[12], (DEPTH, WA_HEADS), 0.5),
        'w_branch': nrm(ks[13], (DEPTH, N_BRANCHES, BRANCH_WIDTH, D), BRANCH_WIDTH ** -0.5),
        'w_out': nrm(ks[14], (DEPTH, D, D), D ** -0.5),
        'w_router': nrm(ks[15], (DEPTH, D, E), D ** -0.5),
        'b_router': nrm(ks[16], (DEPTH, E), 0.01),
        'w_gu': nrm(ks[17], (DEPTH, E, D, 2 * F), D ** -0.5),
        'b_gu': nrm(ks[18], (DEPTH, E, 2 * F), 0.01),
        'w_down': nrm(ks[19], (DEPTH, E, F, D), F ** -0.5),
        'b_down': nrm(ks[20], (DEPTH, E, D), 0.01),
        'final_g': 1.0 + nrm(ks[21], (D,), 0.01),
    }


def reference(x, c, ctx, c_ctx, ln1_g, ln2_g, w_ada, b_ada, w_in, ret_decay_logit, ga_q_gain,
              ga_k_gain, wa_sink, w_branch, w_out, w_router, b_router, w_gu, b_gu, w_down, b_down,
              final_g):
    cos, sin = axial_rope_tables(x.shape[1])
    xc, xx = ctx, x
    n_ctx = ctx.shape[1]
    for l in range(DEPTH):
        need_ctx = l < DEPTH - 1
        sh1x, sc1x, g1x, sh2x, sc2x, g2x = [mm[:, None, :] for mm in modulation(c, w_ada[l], b_ada[l])]
        sh1c, sc1c, g1c, sh2c, sc2c, g2c = modulation(c_ctx, w_ada[l], b_ada[l])
        hx = modulate(rms_norm(xx, ln1_g[l]), sh1x, sc1x)
        hc = modulate(rms_norm(xc, ln1_g[l]), sh1c, sc1c)
        mix_c, mix_x = hybrid_mixer(hc, hx, w_in[l], ret_decay_logit[l], ga_q_gain[l], ga_k_gain[l],
                                    wa_sink[l], w_branch[l], w_out[l], cos, sin, need_ctx)
        xx = xx + g1x * mix_x
        hx = modulate(rms_norm(xx, ln2_g[l]), sh2x, sc2x)
        moe_args = (w_router[l], b_router[l], w_gu[l], b_gu[l], w_down[l], b_down[l])
        if need_ctx:
            xc = xc + g1c * mix_c
            hc = modulate(rms_norm(xc, ln2_g[l]), sh2c, sc2c)
            ffn = moe_ffn(jnp.concatenate([hc, hx], axis=1), *moe_args)
            xc = xc + g2c * ffn[:, :n_ctx]
            xx = xx + g2x * ffn[:, n_ctx:]
        else:
            xx = xx + g2x * moe_ffn(hx, *moe_args)
    return rms_norm(xx, final_g)
```

```python
import functools

import jax
import jax.numpy as jnp
from jax import lax
from jax.experimental import pallas as pl
from jax.experimental.pallas import tpu as pltpu

GRID_W = 64
HEAD_DIM = 128
RET_HEADS = 4
RET_DK = 128
RET_DV = 256
GA_HEADS = 8
GA_KV_HEADS = 2
WA_HEADS = 8
WA_KV_HEADS = 2
WINDOW = 128
N_BRANCHES = 3
TOP_K = 4
SWIGLU_LIMIT = 7.0
SWIGLU_ALPHA = 1.702
ROPE_THETA = 10000.0
NORM_EPS = 1e-6

LANES = 128
CHUNK = 256
GROUP = GA_HEADS // GA_KV_HEADS
VMEM_LIMIT = 56 * 1024 * 1024

RQ, RK, GQ, WQ, GK, WK = 0, 512, 1024, 2048, 3072, 3328
ROPE_COLS = 3584
GV, WV, RV, RG, GATE = 3584, 3840, 4096, 5120, 6144
_ORQ, _ORK, _ORV, _ORG, _OGQ, _OGK, _OGV, _OWQ, _OWK, _OWV, _OGATE = (
    0, 512, 1024, 2048, 3072, 4096, 4352, 4608, 5632, 5888, 6144)

F32 = jnp.float32
BF16 = jnp.bfloat16


def _cparams(sem, vmem=VMEM_LIMIT):
    return pltpu.CompilerParams(dimension_semantics=sem, vmem_limit_bytes=vmem)


def _head_perm():
    j = jnp.arange(HEAD_DIM)
    return jnp.where(j < HEAD_DIM // 2, 2 * j, 2 * (j - HEAD_DIM // 2) + 1)


def _column_perm(d_model):
    hp = _head_perm()

    def rope_heads(start, n_heads):
        return (start + jnp.arange(n_heads)[:, None] * HEAD_DIM + hp[None, :]).reshape(-1)

    def plain(start, size):
        return start + jnp.arange(size)

    return jnp.concatenate([
        rope_heads(_ORQ, RET_HEADS), rope_heads(_ORK, RET_HEADS),
        rope_heads(_OGQ, GA_HEADS), rope_heads(_OWQ, WA_HEADS),
        rope_heads(_OGK, GA_KV_HEADS), rope_heads(_OWK, WA_KV_HEADS),
        plain(_OGV, GA_KV_HEADS * HEAD_DIM), plain(_OWV, WA_KV_HEADS * HEAD_DIM),
        plain(_ORV, RET_HEADS * RET_DV), plain(_ORG, RET_HEADS * RET_DV),
        plain(_OGATE, N_BRANCHES * d_model)])


def _mod_kernel(c_ref, w_ref, b_ref, o_ref):
    c = c_ref[...]
    s = (c * jax.nn.sigmoid(c)).astype(BF16)
    o_ref[...] = jnp.dot(s, w_ref[...].astype(BF16), preferred_element_type=F32) + b_ref[...]


def _modulation(cc8, w_ada, b_ada):
    n_layers, d, n = w_ada.shape
    tn = min(1024, n)
    return pl.pallas_call(
        _mod_kernel,
        grid=(n_layers, n // tn),
        in_specs=[pl.BlockSpec((8, d), lambda l, j: (0, 0)),
                  pl.BlockSpec((None, d, tn), lambda l, j: (l, 0, j)),
                  pl.BlockSpec((None, 1, tn), lambda l, j: (l, 0, j))],
        out_specs=pl.BlockSpec((None, 8, tn), lambda l, j: (l, 0, j)),
        out_shape=jax.ShapeDtypeStruct((n_layers, 8, n), F32),
        compiler_params=_cparams(("arbitrary", "arbitrary")),
        name="modulation",
    )(cc8, w_ada, b_ada.reshape(n_layers, 1, n))


def _in_proj_kernel(x_ref, sh_ref, sc_ref, g_ref, w_ref, cos_ref, sin_ref, cs_ref, rm_ref,
                    o_ref, h_ref, *, n_rope_tiles, tn):
    j = pl.program_id(1)

    @pl.when(j == 0)
    def _():
        x = x_ref[...]
        y = x * lax.rsqrt(jnp.mean(x * x, axis=-1, keepdims=True) + NORM_EPS) * g_ref[...]
        h_ref[...] = (y * (1.0 + sc_ref[...]) + sh_ref[...]).astype(BF16)

    acc = jnp.dot(h_ref[...], w_ref[...], preferred_element_type=F32)

    @pl.when(j < n_rope_tiles)
    def _():
        cos = cos_ref[...]
        sin = sin_ref[...]
        for hh in range(tn // HEAD_DIM):
            sl = slice(hh * HEAD_DIM, (hh + 1) * HEAD_DIM)
            t = acc[:, sl]
            r = lax.rsqrt(jnp.mean(t * t, axis=-1, keepdims=True) + NORM_EPS)
            rm = rm_ref[:, sl]
            t = t * (rm * r + (1.0 - rm)) * cs_ref[:, sl]
            o_ref[:, sl] = (t * cos + pltpu.roll(t, HEAD_DIM // 2, 1) * sin).astype(BF16)

    @pl.when(j >= n_rope_tiles)
    def _():
        o_ref[...] = acc.astype(BF16)


def _in_proj(x_all, mods3, ln_g, w_in_p, cos_f, sin_f, colscale, rmsmask, *, tm, mod_idx):
    m, d = x_all.shape
    ncol = w_in_p.shape[1]
    tn = 512
    kern = functools.partial(_in_proj_kernel, n_rope_tiles=ROPE_COLS // tn, tn=tn)
    return pl.pallas_call(
        kern,
        grid=(m // tm, ncol // tn),
        in_specs=[pl.BlockSpec((tm, d), lambda i, j: (i, 0)),
                  pl.BlockSpec((None, 1, d), lambda i, j: (mod_idx(i), 0, 0)),
                  pl.BlockSpec((None, 1, d), lambda i, j: (mod_idx(i), 0, 1)),
                  pl.BlockSpec((1, d), lambda i, j: (0, 0)),
                  pl.BlockSpec((d, tn), lambda i, j: (0, j)),
                  pl.BlockSpec((tm, HEAD_DIM), lambda i, j: (i, 0)),
                  pl.BlockSpec((tm, HEAD_DIM), lambda i, j: (i, 0)),
                  pl.BlockSpec((1, tn), lambda i, j: (0, j)),
                  pl.BlockSpec((1, tn), lambda i, j: (0, j))],
        out_specs=pl.BlockSpec((tm, tn), lambda i, j: (i, j)),
        out_shape=jax.ShapeDtypeStruct((m, ncol), BF16),
        scratch_shapes=[pltpu.VMEM((tm, d), BF16)],
        compiler_params=_cparams(("arbitrary", "arbitrary")),
        name="in_proj",
    )(x_all, mods3, mods3, ln_g.reshape(1, d), w_in_p, cos_f, sin_f,
      colscale.reshape(1, ncol), rmsmask.reshape(1, ncol))


def _retention_kernel(q_ref, k_ref, v_ref, g_ref, dm_ref, qd_ref, kd_ref, cd_ref,
                      o_ref, s_ref, ob_ref, *, n_lat_chunks):
    d = pl.program_id(1)
    t = pl.program_id(2)

    @pl.when(t == 0)
    def _():
        s_ref[...] = jnp.zeros_like(s_ref)

    cid = jnp.where(d == 1, t, jnp.where(t == 0, 0, 1 + n_lat_chunks - t))
    for h in range(RET_HEADS):
        q = q_ref[:, h * RET_DK:(h + 1) * RET_DK]
        k = k_ref[:, h * RET_DK:(h + 1) * RET_DK]
        v = v_ref[:, h * RET_DV:(h + 1) * RET_DV]
        state = s_ref[h]
        scores = lax.dot_general(q, k, (((1,), (1,)), ((), ())), preferred_element_type=F32)
        p = (scores * dm_ref[h]).astype(BF16)
        qd = (q.astype(F32) * qd_ref[h]).astype(BF16)
        o = (jnp.dot(p, v, preferred_element_type=F32)
             + jnp.dot(qd, state.astype(BF16), preferred_element_type=F32))
        kd = (k.astype(F32) * kd_ref[h]).astype(BF16)
        s_ref[h] = state * cd_ref[h] + lax.dot_general(
            kd, v, (((0,), (0,)), ((), ())), preferred_element_type=F32)
        sl = slice(h * RET_DV, (h + 1) * RET_DV)

        @pl.when(d == 0)
        def _():
            ob_ref[cid, :, sl] = o

        @pl.when(d == 1)
        def _():
            tot = o + ob_ref[cid, :, sl]
            tot = tot * lax.rsqrt(jnp.mean(tot * tot, axis=-1, keepdims=True) + NORM_EPS)
            g = g_ref[:, sl].astype(F32)
            o_ref[:, sl] = (tot * (g * jax.nn.sigmoid(g))).astype(BF16)


def _retention(u, tabs, *, batch, seq, n_ctx):
    m = u.shape[0]
    assert n_ctx == CHUNK and seq % CHUNK == 0
    nc = seq // CHUNK
    lat_blocks = batch * nc
    dmat, qdec, kdec, cdec = tabs

    def rowblk(b, d, t):
        lat = b * nc + jnp.where(d == 1, t - 1, nc - t)
        return jnp.where(t == 0, lat_blocks + b, lat)

    def in_map(col):
        return lambda b, d, t: (rowblk(b, d, t), col)

    def out_map(b, d, t):
        return (jnp.where(d == 1, rowblk(b, 1, t), lat_blocks + b), 0)

    def tab_map(b, d, t):
        return (d, 0, 0, 0)

    qk_w = RET_HEADS * RET_DK
    v_w = RET_HEADS * RET_DV
    kern = functools.partial(_retention_kernel, n_lat_chunks=nc)
    return pl.pallas_call(
        kern,
        grid=(batch, 2, nc + 1),
        in_specs=[pl.BlockSpec((CHUNK, qk_w), in_map(RQ // qk_w)),
                  pl.BlockSpec((CHUNK, qk_w), in_map(RK // qk_w)),
                  pl.BlockSpec((CHUNK, v_w), in_map(RV // v_w)),
                  pl.BlockSpec((CHUNK, v_w), in_map(RG // v_w)),
                  pl.BlockSpec((None, RET_HEADS, CHUNK, CHUNK), tab_map),
                  pl.BlockSpec((None, RET_HEADS, CHUNK, RET_DK), tab_map),
                  pl.BlockSpec((None, RET_HEADS, CHUNK, RET_DK), tab_map),
                  pl.BlockSpec((None, RET_HEADS, 1, RET_DV), tab_map)],
        out_specs=pl.BlockSpec((CHUNK, v_w), out_map),
        out_shape=jax.ShapeDtypeStruct((m, v_w), BF16),
        scratch_shapes=[pltpu.VMEM((RET_HEADS, RET_DK, RET_DV), F32),
                        pltpu.VMEM((nc + 1, CHUNK, v_w), F32)],
        compiler_params=_cparams(("arbitrary", "arbitrary", "arbitrary")),
        name="retention",
    )(u, u, u, u, dmat, qdec, kdec, cdec)


def _retention_tables(decay_logit):
    lg = jax.nn.log_sigmoid(decay_logit.astype(F32))
    lf, lb = lg[0][:, None, None], lg[1][:, None, None]
    pos = jnp.arange(CHUNK, dtype=F32)
    diff = pos[:, None] - pos[None, :]
    d_f = jnp.where(diff >= 0, jnp.exp(jnp.where(diff >= 0, diff, 0.0) * lf), 0.0)
    d_b = jnp.where(diff < 0, jnp.exp(jnp.where(diff < 0, -diff, 0.0) * lb), 0.0)
    ones_k = jnp.ones((1, 1, RET_DK), F32)
    q_f = jnp.exp((pos + 1.0)[None, :, None] * lf) * ones_k
    q_b = jnp.exp((CHUNK - pos)[None, :, None] * lb) * ones_k
    k_f = jnp.exp((CHUNK - 1.0 - pos)[None, :, None] * lf) * ones_k
    k_b = jnp.exp(pos[None, :, None] * lb) * ones_k
    ones_v = jnp.ones((1, 1, RET_DV), F32)
    c_f = jnp.exp(CHUNK * lf) * ones_v
    c_b = jnp.exp(CHUNK * lb) * ones_v
    return (jnp.stack([d_b, d_f]), jnp.stack([q_b, q_f]), jnp.stack([k_b, k_f]),
            jnp.stack([c_b, c_f]))


def _global_attn_kernel(q_ref, kx_ref, vx_ref, kc_ref, vc_ref, o_ref, m_ref, l_ref, acc_ref,
                        *, n_lat_tiles, n_key_chunks):
    i = pl.program_id(2)
    qs = jnp.concatenate([q_ref[:, h * HEAD_DIM:(h + 1) * HEAD_DIM] for h in range(GROUP)], axis=0)
    m_ref[...] = jnp.full_like(m_ref, -jnp.inf)
    l_ref[...] = jnp.zeros_like(l_ref)
    acc_ref[...] = jnp.zeros_like(acc_ref)

    def step(k, v):
        s = lax.dot_general(qs, k, (((1,), (1,)), ((), ())), preferred_element_type=F32)
        m_old = m_ref[...]
        m_new = jnp.maximum(m_old, jnp.max(s, axis=-1, keepdims=True))
        alpha = jnp.exp(m_old - m_new)
        p = jnp.exp(s - m_new)
        l_ref[...] = alpha * l_ref[...] + jnp.sum(p, axis=-1, keepdims=True)
        acc_ref[...] = alpha * acc_ref[...] + jnp.dot(p.astype(BF16), v, preferred_element_type=F32)
        m_ref[...] = m_new

    step(kc_ref[...], vc_ref[...])

    def body(c, carry):
        off = pl.multiple_of(c * CHUNK, CHUNK)
        step(kx_ref[pl.ds(off, CHUNK), :], vx_ref[pl.ds(off, CHUNK), :])
        return carry

    lax.fori_loop(0, jnp.where(i < n_lat_tiles, n_key_chunks, 0), body, 0)
    out = acc_ref[...] / l_ref[...]
    for h in range(GROUP):
        o_ref[:, h * HEAD_DIM:(h + 1) * HEAD_DIM] = out[h * CHUNK:(h + 1) * CHUNK].astype(BF16)


def _global_attention(u, *, batch, seq, n_ctx, with_ctx):
    m = u.shape[0]
    nq = seq // CHUNK
    lat_blocks = batch * nq
    gw = GROUP * HEAD_DIM

    def qrow(b, i):
        return jnp.where(i < nq, b * nq + i, lat_blocks + b)

    kern = functools.partial(_global_attn_kernel, n_lat_tiles=nq, n_key_chunks=seq // CHUNK)
    rows = GROUP * CHUNK
    return pl.pallas_call(
        kern,
        grid=(batch, GA_KV_HEADS, nq + (1 if with_ctx else 0)),
        in_specs=[pl.BlockSpec((CHUNK, gw), lambda b, kh, i: (qrow(b, i), GQ // gw + kh)),
                  pl.BlockSpec((seq, HEAD_DIM), lambda b, kh, i: (b, GK // HEAD_DIM + kh)),
                  pl.BlockSpec((seq, HEAD_DIM), lambda b, kh, i: (b, GV // HEAD_DIM + kh)),
                  pl.BlockSpec((n_ctx, HEAD_DIM),
                               lambda b, kh, i: (batch * seq // n_ctx + b, GK // HEAD_DIM + kh)),
                  pl.BlockSpec((n_ctx, HEAD_DIM),
                               lambda b, kh, i: (batch * seq // n_ctx + b, GV // HEAD_DIM + kh))],
        out_specs=pl.BlockSpec((CHUNK, gw), lambda b, kh, i: (qrow(b, i), kh)),
        out_shape=jax.ShapeDtypeStruct((m, GA_HEADS * HEAD_DIM), BF16),
        scratch_shapes=[pltpu.VMEM((rows, 1), F32), pltpu.VMEM((rows, 1), F32),
                        pltpu.VMEM((rows, HEAD_DIM), F32)],
        compiler_params=_cparams(("arbitrary", "arbitrary", "arbitrary")),
        name="global_attention",
    )(u, u, u, u, u)


def _window_attn_kernel(sink_ref, q_ref, kx_ref, vx_ref, kc_ref, vc_ref, o_ref,
                        *, n_lat_tiles, seq):
    kh = pl.program_id(1)
    i = pl.program_id(2)
    span = CHUNK + 2 * WINDOW
    qs = jnp.concatenate([q_ref[:, h * HEAD_DIM:(h + 1) * HEAD_DIM] for h in range(GROUP)], axis=0)
    is_lat = i < n_lat_tiles
    start = jnp.clip(i * CHUNK - WINDOW, 0, seq - span)
    start = pl.multiple_of(jnp.where(is_lat, start, 0), WINDOW)
    kw = kx_ref[pl.ds(start, span), :]
    vw = vx_ref[pl.ds(start, span), :]
    s_win = lax.dot_general(qs, kw, (((1,), (1,)), ((), ())), preferred_element_type=F32)
    s_ctx = lax.dot_general(qs, kc_ref[...], (((1,), (1,)), ((), ())), preferred_element_type=F32)
    rows = GROUP * CHUNK
    q_pos = i * CHUNK + lax.broadcasted_iota(jnp.int32, (rows, span), 0) % CHUNK
    k_pos = start + lax.broadcasted_iota(jnp.int32, (rows, span), 1)
    allowed = (jnp.abs(k_pos - q_pos) <= WINDOW) & is_lat
    s_win = jnp.where(allowed, s_win, -jnp.inf)
    sink = jnp.concatenate(
        [jnp.full((CHUNK, 1), sink_ref[kh * GROUP + h], F32) for h in range(GROUP)], axis=0)
    mx = jnp.maximum(jnp.maximum(jnp.max(s_win, axis=-1, keepdims=True),
                                 jnp.max(s_ctx, axis=-1, keepdims=True)), sink)
    p_win = jnp.exp(s_win - mx)
    p_ctx = jnp.exp(s_ctx - mx)
    den = (jnp.sum(p_win, axis=-1, keepdims=True) + jnp.sum(p_ctx, axis=-1, keepdims=True)
           + jnp.exp(sink - mx))
    out = (jnp.dot(p_win.astype(BF16), vw, preferred_element_type=F32)
           + jnp.dot(p_ctx.astype(BF16), vc_ref[...], preferred_element_type=F32)) / den
    for h in range(GROUP):
        o_ref[:, h * HEAD_DIM:(h + 1) * HEAD_DIM] = out[h * CHUNK:(h + 1) * CHUNK].astype(BF16)


def _window_attention(u, sink, *, batch, seq, n_ctx, with_ctx):
    m = u.shape[0]
    nq = seq // CHUNK
    lat_blocks = batch * nq
    gw = GROUP * HEAD_DIM
    assert seq >= CHUNK + 2 * WINDOW

    def qrow(b, i):
        return jnp.where(i < nq, b * nq + i, lat_blocks + b)

    kern = functools.partial(_window_attn_kernel, n_lat_tiles=nq, seq=seq)
    return pl.pallas_call(
        kern,
        grid=(batch, WA_KV_HEADS, nq + (1 if with_ctx else 0)),
        in_specs=[pl.BlockSpec(memory_space=pltpu.SMEM),
                  pl.BlockSpec((CHUNK, gw), lambda b, kh, i: (qrow(b, i), WQ // gw + kh)),
                  pl.BlockSpec((seq, HEAD_DIM), lambda b, kh, i: (b, WK // HEAD_DIM + kh)),
                  pl.BlockSpec((seq, HEAD_DIM), lambda b, kh, i: (b, WV // HEAD_DIM + kh)),
                  pl.BlockSpec((n_ctx, HEAD_DIM),
                               lambda b, kh, i: (batch * seq // n_ctx + b, WK // HEAD_DIM + kh)),
                  pl.BlockSpec((n_ctx, HEAD_DIM),
                               lambda b, kh, i: (batch * seq // n_ctx + b, WV // HEAD_DIM + kh))],
        out_specs=pl.BlockSpec((CHUNK, gw), lambda b, kh, i: (qrow(b, i), kh)),
        out_shape=jax.ShapeDtypeStruct((m, WA_HEADS * HEAD_DIM), BF16),
        compiler_params=_cparams(("arbitrary", "arbitrary", "arbitrary")),
        name="window_attention",
    )(sink.astype(F32), u, u, u, u, u)


def _merge_kernel(o0_ref, o1_ref, o2_ref, g0_ref, g1_ref, g2_ref, w0_ref, w1_ref, w2_ref, out_ref):
    tot = None
    for o_ref, g_ref, w_ref in ((o0_ref, g0_ref, w0_ref), (o1_ref, g1_ref, w1_ref),
                                (o2_ref, g2_ref, w2_ref)):
        term = jax.nn.sigmoid(g_ref[...].astype(F32)) * jnp.dot(
            o_ref[...], w_ref[...], preferred_element_type=F32)
        tot = term if tot is None else tot + term
    out_ref[...] = tot.astype(BF16)


def _merge(o_ret, o_ga, o_wa, u, w_branch, *, rows, tm, d):
    tn = 512
    bw = w_branch.shape[1]
    gate_blk = GATE // tn
    nd = d // tn

    def o_spec():
        return pl.BlockSpec((tm, bw), lambda i, j: (i, 0))

    def g_spec(br):
        return pl.BlockSpec((tm, tn), lambda i, j: (i, gate_blk + br * nd + j))

    def w_spec(br):
        return pl.BlockSpec((None, bw, tn), lambda i, j: (br, 0, j))

    return pl.pallas_call(
        _merge_kernel,
        grid=(rows // tm, nd),
        in_specs=[o_spec(), o_spec(), o_spec(), g_spec(0), g_spec(1), g_spec(2),
                  w_spec(0), w_spec(1), w_spec(2)],
        out_specs=pl.BlockSpec((tm, tn), lambda i, j: (i, j)),
        out_shape=jax.ShapeDtypeStruct((rows, d), BF16),
        compiler_params=_cparams(("arbitrary", "arbitrary")),
        name="branch_merge",
    )(o_ret, o_ga, o_wa, u, u, u, w_branch, w_branch, w_branch)


def _out_proj_kernel(m_ref, w_ref, x_ref, g_ref, o_ref):
    o_ref[...] = x_ref[...] + g_ref[...] * jnp.dot(m_ref[...], w_ref[...], preferred_element_type=F32)


def _out_proj(mix, w_out, x_all, mods3, *, rows, tm, d, mod_idx):
    tn = 512
    nd = d // tn
    return pl.pallas_call(
        _out_proj_kernel,
        grid=(rows // tm, nd),
        in_specs=[pl.BlockSpec((tm, d), lambda i, j: (i, 0)),
                  pl.BlockSpec((d, tn), lambda i, j: (0, j)),
                  pl.BlockSpec((tm, tn), lambda i, j: (i, j)),
                  pl.BlockSpec((None, 1, tn), lambda i, j: (mod_idx(i), 0, 2 * nd + j))],
        out_specs=pl.BlockSpec((tm, tn), lambda i, j: (i, j)),
        out_shape=jax.ShapeDtypeStruct((rows, d), F32),
        compiler_params=_cparams(("arbitrary", "arbitrary")),
        name="out_proj",
    )(mix, w_out, x_all, mods3)


def _router_kernel(x_ref, sh_ref, sc_ref, g_ref, wr_ref, br_ref, h_ref, idx_ref, wt_ref, *, n_experts):
    x = x_ref[...]
    y = x * lax.rsqrt(jnp.mean(x * x, axis=-1, keepdims=True) + NORM_EPS) * g_ref[...]
    h = y * (1.0 + sc_ref[...]) + sh_ref[...]
    h_ref[...] = h.astype(BF16)
    logits = jnp.dot(h, wr_ref[...], preferred_element_type=F32,
                     precision=lax.Precision.HIGHEST) + br_ref[...]
    lane = lax.broadcasted_iota(jnp.int32, logits.shape, 1).astype(F32)
    vals, ids = [], []
    cur = logits
    for _ in range(TOP_K):
        mx = jnp.max(cur, axis=-1, keepdims=True)
        sel = jnp.min(jnp.where(cur == mx, lane, float(n_experts)), axis=-1, keepdims=True)
        vals.append(mx)
        ids.append(sel)
        cur = jnp.where(lane == sel, -jnp.inf, cur)
    e = [jnp.exp(v - vals[0]) for v in vals]
    den = e[0] + e[1] + e[2] + e[3]
    for k in range(TOP_K):
        idx_ref[:, k:k + 1] = ids[k].astype(jnp.int32)
        wt_ref[:, k:k + 1] = e[k] / den


def _router(x_new, mods3, ln_g, w_router, b_router, *, rows, tm, d, mod_idx):
    n_experts = w_router.shape[1]
    kern = functools.partial(_router_kernel, n_experts=n_experts)
    return pl.pallas_call(
        kern,
        grid=(rows // tm,),
        in_specs=[pl.BlockSpec((tm, d), lambda i: (i, 0)),
                  pl.BlockSpec((None, 1, d), lambda i: (mod_idx(i), 0, 3)),
                  pl.BlockSpec((None, 1, d), lambda i: (mod_idx(i), 0, 4)),
                  pl.BlockSpec((1, d), lambda i: (0, 0)),
                  pl.BlockSpec((d, n_experts), lambda i: (0, 0)),
                  pl.BlockSpec((1, n_experts), lambda i: (0, 0))],
        out_specs=[pl.BlockSpec((tm, d), lambda i: (i, 0)),
                   pl.BlockSpec((tm, TOP_K), lambda i: (i, 0)),
                   pl.BlockSpec((tm, TOP_K), lambda i: (i, 0))],
        out_shape=[jax.ShapeDtypeStruct((rows, d), BF16),
                   jax.ShapeDtypeStruct((rows, TOP_K), jnp.int32),
                   jax.ShapeDtypeStruct((rows, TOP_K), F32)],
        compiler_params=_cparams(("arbitrary",)),
        name="norm2_router",
    )(x_new, mods3, mods3, ln_g.reshape(1, d), w_router, b_router.reshape(1, n_experts))


def _moe_up_kernel(ie_ref, ic_ref, it_ref, iv_ref, if_ref, x_ref, wg_ref, wl_ref, bg_ref, bl_ref,
                   o_ref, wg_s, wl_s):
    i = pl.program_id(0)

    @pl.when(if_ref[i] == 1)
    def _():
        wg_s[...] = wg_ref[...].astype(BF16)
        wl_s[...] = wl_ref[...].astype(BF16)

    @pl.when(iv_ref[i] == 1)
    def _():
        x = x_ref[...]
        glu = jnp.dot(x, wg_s[...], preferred_element_type=F32) + bg_ref[...]
        lin = jnp.dot(x, wl_s[...], preferred_element_type=F32) + bl_ref[...]
        glu = jnp.minimum(glu, SWIGLU_LIMIT)
        lin = jnp.clip(lin, -SWIGLU_LIMIT, SWIGLU_LIMIT)
        o_ref[...] = (glu * jax.nn.sigmoid(SWIGLU_ALPHA * glu) * (lin + 1.0)).astype(BF16)


def _moe_up(sched, xs, w_gu, b_gu, *, tm, tf):
    r_pad, d = xs.shape
    n_experts, _, f2 = w_gu.shape
    f = f2 // 2
    nc = f // tf
    n_items = sched[0].shape[0]
    grid_spec = pltpu.PrefetchScalarGridSpec(
        num_scalar_prefetch=5,
        grid=(n_items,),
        in_specs=[pl.BlockSpec((tm, d), lambda i, ie, ic, it, iv, fl: (it[i], 0)),
                  pl.BlockSpec((None, d, tf), lambda i, ie, ic, it, iv, fl: (ie[i], 0, ic[i])),
                  pl.BlockSpec((None, d, tf), lambda i, ie, ic, it, iv, fl: (ie[i], 0, nc + ic[i])),
                  pl.BlockSpec((None, 1, tf), lambda i, ie, ic, it, iv, fl: (ie[i], 0, ic[i])),
                  pl.BlockSpec((None, 1, tf), lambda i, ie, ic, it, iv, fl: (ie[i], 0, nc + ic[i]))],
        out_specs=pl.BlockSpec((tm, tf), lambda i, ie, ic, it, iv, fl: (it[i], ic[i])),
        scratch_shapes=[pltpu.VMEM((d, tf), BF16), pltpu.VMEM((d, tf), BF16)])
    b3 = b_gu.reshape(n_experts, 1, f2)
    return pl.pallas_call(
        _moe_up_kernel,
        grid_spec=grid_spec,
        out_shape=jax.ShapeDtypeStruct((r_pad, f), BF16),
        compiler_params=_cparams(("arbitrary",)),
        name="moe_up",
    )(*sched, xs, w_gu, w_gu, b3, b3)


def _moe_down_kernel(ie_ref, ic_ref, it_ref, iv_ref, if_ref, a_ref, w_ref, b_ref, gt_ref, o_ref, w_s):
    i = pl.program_id(0)

    @pl.when(if_ref[i] == 1)
    def _():
        w_s[...] = w_ref[...].astype(BF16)

    @pl.when(iv_ref[i] == 1)
    def _():
        y = jnp.dot(a_ref[...], w_s[...], preferred_element_type=F32) + b_ref[...]
        o_ref[...] = (gt_ref[...] * y).astype(BF16)


def _moe_down(sched, act, w_down, b_down, gate_sorted, *, tm, tn):
    r_pad, f = act.shape
    n_experts, _, d = w_down.shape
    n_items = sched[0].shape[0]
    grid_spec = pltpu.PrefetchScalarGridSpec(
        num_scalar_prefetch=5,
        grid=(n_items,),
        in_specs=[pl.BlockSpec((tm, f), lambda i, ie, ic, it, iv, fl: (it[i], 0)),
                  pl.BlockSpec((None, f, tn), lambda i, ie, ic, it, iv, fl: (ie[i], 0, ic[i])),
                  pl.BlockSpec((None, 1, tn), lambda i, ie, ic, it, iv, fl: (ie[i], 0, ic[i])),
                  pl.BlockSpec((tm, 1), lambda i, ie, ic, it, iv, fl: (it[i], 0))],
        out_specs=pl.BlockSpec((tm, tn), lambda i, ie, ic, it, iv, fl: (it[i], ic[i])),
        scratch_shapes=[pltpu.VMEM((f, tn), BF16)])
    return pl.pallas_call(
        _moe_down_kernel,
        grid_spec=grid_spec,
        out_shape=jax.ShapeDtypeStruct((r_pad, d), BF16),
        compiler_params=_cparams(("arbitrary",)),
        name="moe_down",
    )(*sched, act, w_down, b_down.reshape(n_experts, 1, d), gate_sorted.reshape(r_pad, 1))


def _moe_schedule(idx, wts, *, n_experts, tm, n_chunks):
    n_tok = idx.shape[0]
    n_pairs = n_tok * TOP_K
    r_pad = n_pairs + n_experts * tm
    n_tiles = r_pad // tm
    flat_e = idx.reshape(-1)
    onehot = (flat_e[:, None] == jnp.arange(n_experts, dtype=jnp.int32)[None, :]).astype(jnp.int32)
    csum = jnp.cumsum(onehot, axis=0)
    rank = jnp.take_along_axis(csum, flat_e[:, None], axis=1)[:, 0] - 1
    counts = csum[-1]
    ntiles = (counts + tm - 1) // tm
    tile_end = jnp.cumsum(ntiles)
    tile_start = tile_end - ntiles
    dest = tile_start[flat_e] * tm + rank
    src_tok = jnp.zeros((r_pad,), jnp.int32).at[dest].set(jnp.arange(n_pairs, dtype=jnp.int32) // TOP_K)
    gate_sorted = jnp.zeros((r_pad,), F32).at[dest].set(wts.reshape(-1))
    n_used = tile_end[-1]
    n_items = n_chunks * n_tiles
    item = jnp.arange(n_items, dtype=jnp.int32)
    valid = item < n_chunks * n_used
    ic_ = jnp.minimum(item, jnp.maximum(n_chunks * n_used - 1, 0))
    e = jnp.minimum(jnp.searchsorted(n_chunks * tile_end, ic_, side="right"), n_experts - 1).astype(jnp.int32)
    nt_e = jnp.maximum(ntiles[e], 1)
    r = ic_ - n_chunks * tile_start[e]
    c = r // nt_e
    t = tile_start[e] + r % nt_e
    first = valid & (r % nt_e == 0)
    sched = (e, c.astype(jnp.int32), t.astype(jnp.int32), valid.astype(jnp.int32), first.astype(jnp.int32))
    return sched, src_tok, gate_sorted, dest.reshape(n_tok, TOP_K)


def _combine_kernel(x_ref, y_ref, g_ref, fg_ref, o_ref, *, d, final):
    y = y_ref[:, 0:d].astype(F32)
    for k in range(1, TOP_K):
        y = y + y_ref[:, k * d:(k + 1) * d].astype(F32)
    x = x_ref[...] + g_ref[...] * y
    if final:
        x = x * lax.rsqrt(jnp.mean(x * x, axis=-1, keepdims=True) + NORM_EPS) * fg_ref[...]
    o_ref[...] = x


def _combine(x_new, ysg, mods3, final_g, *, rows, tm, d, mod_idx, final):
    kern = functools.partial(_combine_kernel, d=d, final=final)
    return pl.pallas_call(
        kern,
        grid=(rows // tm,),
        in_specs=[pl.BlockSpec((tm, d), lambda i: (i, 0)),
                  pl.BlockSpec((tm, TOP_K * d), lambda i: (i, 0)),
                  pl.BlockSpec((None, 1, d), lambda i: (mod_idx(i), 0, 5)),
                  pl.BlockSpec((1, d), lambda i: (0, 0))],
        out_specs=pl.BlockSpec((tm, d), lambda i: (i, 0)),
        out_shape=jax.ShapeDtypeStruct((rows, d), F32),
        compiler_params=_cparams(("arbitrary",)),
        name="moe_combine",
    )(x_new, ysg, mods3, final_g.reshape(1, d))


def _rope_tables(batch, seq, n_ctx):
    rows = seq // GRID_W
    row = jnp.repeat(jnp.arange(rows, dtype=F32), GRID_W)
    col = jnp.tile(jnp.arange(GRID_W, dtype=F32), rows)
    quarter = HEAD_DIM // 4
    inv_freq = ROPE_THETA ** (-jnp.arange(quarter, dtype=F32) / quarter)
    ang = jnp.concatenate([row[:, None] * inv_freq, col[:, None] * inv_freq], axis=-1)
    cos, sin = jnp.cos(ang), jnp.sin(ang)
    cos_x = jnp.tile(jnp.concatenate([cos, cos], axis=-1), (batch, 1))
    sin_x = jnp.tile(jnp.concatenate([-sin, sin], axis=-1), (batch, 1))
    cos_f = jnp.concatenate([cos_x, jnp.ones((batch * n_ctx, HEAD_DIM), F32)], axis=0)
    sin_f = jnp.concatenate([sin_x, jnp.zeros((batch * n_ctx, HEAD_DIM), F32)], axis=0)
    return cos_f, sin_f


def _column_params(q_gain, k_gain, ncol):
    hp = _head_perm()
    scale = HEAD_DIM ** -0.5
    k_scale = RET_DK ** -0.5
    cs = jnp.ones((ncol,), F32)
    cs = cs.at[RK:RK + RET_HEADS * RET_DK].set(k_scale)
    cs = cs.at[GQ:GQ + GA_HEADS * HEAD_DIM].set(jnp.tile(q_gain.astype(F32)[hp] * scale, GA_HEADS))
    cs = cs.at[GK:GK + GA_KV_HEADS * HEAD_DIM].set(jnp.tile(k_gain.astype(F32)[hp], GA_KV_HEADS))
    cs = cs.at[WQ:WQ + WA_HEADS * HEAD_DIM].set(scale)
    rm = jnp.zeros((ncol,), F32).at[GQ:GQ + GA_HEADS * HEAD_DIM].set(1.0)
    rm = rm.at[GK:GK + GA_KV_HEADS * HEAD_DIM].set(1.0)
    return cs, rm


def kernel(x, c, ctx, c_ctx, ln1_g, ln2_g, w_ada, b_ada, w_in, ret_decay_logit, ga_q_gain,
           ga_k_gain, wa_sink, w_branch, w_out, w_router, b_router, w_gu, b_gu, w_down, b_down,
           final_g):
    batch, seq, d = x.shape
    n_ctx = ctx.shape[1]
    depth = w_ada.shape[0]
    n_experts = w_router.shape[2]
    ncol = w_in.shape[2]
    mx, mc = batch * seq, batch * n_ctx
    m = mx + mc
    tm = min(1024, mc)
    assert seq % tm == 0 and mc % tm == 0 and batch + 1 <= 8
    tm_stream = min(256, tm)
    assert w_gu.shape[3] // 2 == w_down.shape[3] == d

    def mod_idx_for(tile):
        def mod_idx(i):
            return jnp.where(i < mx // tile, i // (seq // tile), batch)
        return mod_idx

    mod_idx = mod_idx_for(tm)
    mod_idx_s = mod_idx_for(tm_stream)

    cc8 = jnp.zeros((8, d), F32).at[:batch].set(c).at[batch].set(c_ctx)
    mods = _modulation(cc8, w_ada, b_ada)
    cos_f, sin_f = _rope_tables(batch, seq, n_ctx)
    perm = _column_perm(d)

    x_all = jnp.concatenate([x.reshape(mx, d), ctx.reshape(mc, d)], axis=0)
    e_tm = 256
    tf = min(1024, w_down.shape[2], w_gu.shape[3] // 2)

    for l in range(depth):
        need_ctx = l < depth - 1
        rows = m if need_ctx else mx
        mods3 = mods[l].reshape(8, 1, 6 * d)
        w_in_p = jnp.take(w_in[l], perm, axis=1).astype(BF16)
        colscale, rmsmask = _column_params(ga_q_gain[l], ga_k_gain[l], ncol)
        u = _in_proj(x_all, mods3, ln1_g[l], w_in_p, cos_f, sin_f, colscale, rmsmask,
                     tm=tm, mod_idx=mod_idx)
        o_ret = _retention(u, _retention_tables(ret_decay_logit[l]), batch=batch, seq=seq, n_ctx=n_ctx)
        o_ga = _global_attention(u, batch=batch, seq=seq, n_ctx=n_ctx, with_ctx=need_ctx)
        o_wa = _window_attention(u, wa_sink[l], batch=batch, seq=seq, n_ctx=n_ctx, with_ctx=need_ctx)
        mix = _merge(o_ret, o_ga, o_wa, u, w_branch[l].astype(BF16), rows=rows, tm=tm, d=d)
        x_new = _out_proj(mix, w_out[l].astype(BF16), x_all, mods3, rows=rows, tm=tm, d=d, mod_idx=mod_idx)
        h2, idx, wts = _router(x_new, mods3, ln2_g[l], w_router[l], b_router[l],
                               rows=rows, tm=tm_stream, d=d, mod_idx=mod_idx_s)
        sched, src_tok, gate_sorted, pos = _moe_schedule(
            idx, wts, n_experts=n_experts, tm=e_tm, n_chunks=w_down.shape[2] // tf)
        xs = jnp.take(h2, src_tok, axis=0)
        act = _moe_up(sched, xs, w_gu[l], b_gu[l], tm=e_tm, tf=tf)
        ys = _moe_down(sched, act, w_down[l], b_down[l], gate_sorted, tm=e_tm, tn=tf)
        ysg = jnp.take(ys, pos.reshape(-1), axis=0).reshape(rows, TOP_K * d)
        x_all = _combine(x_new, ysg, mods3, final_g, rows=rows, tm=tm_stream, d=d,
                         mod_idx=mod_idx_s, final=not need_ctx)
    return x_all[:mx].reshape(batch, seq, d)
```

```python
import functools

import jax
import jax.numpy as jnp
from jax import lax
from jax.experimental import pallas as pl
from jax.experimental.pallas import tpu as pltpu

GRID_W = 64
HEAD_DIM = 128
RET_HEADS = 4
RET_DK = 128
RET_DV = 256
GA_HEADS = 8
GA_KV_HEADS = 2
WA_HEADS = 8
WA_KV_HEADS = 2
WINDOW = 128
N_BRANCHES = 3
TOP_K = 4
SWIGLU_LIMIT = 7.0
SWIGLU_ALPHA = 1.702
ROPE_THETA = 10000.0
NORM_EPS = 1e-6

LANES = 128
CHUNK = 256
GROUP = GA_HEADS // GA_KV_HEADS
GA_TQ = 256
VMEM_LIMIT = 56 * 1024 * 1024

RQ, RK, GQ, WQ, GK, WK = 0, 512, 1024, 2048, 3072, 3328
ROPE_COLS = 3584
GV, WV, RV, RG, GATE = 3584, 3840, 4096, 5120, 6144
_ORQ, _ORK, _ORV, _ORG, _OGQ, _OGK, _OGV, _OWQ, _OWK, _OWV, _OGATE = (
    0, 512, 1024, 2048, 3072, 4096, 4352, 4608, 5632, 5888, 6144)

F32 = jnp.float32
BF16 = jnp.bfloat16


def _cparams(sem, vmem=VMEM_LIMIT):
    return pltpu.CompilerParams(dimension_semantics=sem, vmem_limit_bytes=vmem)


def _head_perm():
    j = jnp.arange(HEAD_DIM)
    return jnp.where(j < HEAD_DIM // 2, 2 * j, 2 * (j - HEAD_DIM // 2) + 1)


def _column_perm(d_model):
    hp = _head_perm()

    def rope_heads(start, n_heads):
        return (start + jnp.arange(n_heads)[:, None] * HEAD_DIM + hp[None, :]).reshape(-1)

    def plain(start, size):
        return start + jnp.arange(size)

    return jnp.concatenate([
        rope_heads(_ORQ, RET_HEADS), rope_heads(_ORK, RET_HEADS),
        rope_heads(_OGQ, GA_HEADS), rope_heads(_OWQ, WA_HEADS),
        rope_heads(_OGK, GA_KV_HEADS), rope_heads(_OWK, WA_KV_HEADS),
        plain(_OGV, GA_KV_HEADS * HEAD_DIM), plain(_OWV, WA_KV_HEADS * HEAD_DIM),
        plain(_ORV, RET_HEADS * RET_DV), plain(_ORG, RET_HEADS * RET_DV),
        plain(_OGATE, N_BRANCHES * d_model)])


def _mod_kernel(c_ref, w_ref, b_ref, o_ref):
    c = c_ref[...]
    s = (c * jax.nn.sigmoid(c)).astype(BF16)
    o_ref[...] = jnp.dot(s, w_ref[...].astype(BF16), preferred_element_type=F32) + b_ref[...]


def _modulation(cc8, w_ada, b_ada):
    n_layers, d, n = w_ada.shape
    tn = min(1024, n)
    return pl.pallas_call(
        _mod_kernel,
        grid=(n_layers, n // tn),
        in_specs=[pl.BlockSpec((8, d), lambda l, j: (0, 0)),
                  pl.BlockSpec((None, d, tn), lambda l, j: (l, 0, j)),
                  pl.BlockSpec((None, 1, tn), lambda l, j: (l, 0, j))],
        out_specs=pl.BlockSpec((None, 8, tn), lambda l, j: (l, 0, j)),
        out_shape=jax.ShapeDtypeStruct((n_layers, 8, n), F32),
        compiler_params=_cparams(("arbitrary", "arbitrary")),
        name="modulation",
    )(cc8, w_ada, b_ada.reshape(n_layers, 1, n))


def _in_proj_kernel(x_ref, sh_ref, sc_ref, g_ref, w_ref, cos_ref, sin_ref, cs_ref, rm_ref,
                    o_ref, h_ref, *, n_rope_tiles, tn):
    j = pl.program_id(1)

    @pl.when(j == 0)
    def _():
        x = x_ref[...]
        y = x * lax.rsqrt(jnp.mean(x * x, axis=-1, keepdims=True) + NORM_EPS) * g_ref[...]
        h_ref[...] = (y * (1.0 + sc_ref[...]) + sh_ref[...]).astype(BF16)

    acc = jnp.dot(h_ref[...], w_ref[...], preferred_element_type=F32)

    @pl.when(j < n_rope_tiles)
    def _():
        cos = cos_ref[...]
        sin = sin_ref[...]
        for hh in range(tn // HEAD_DIM):
            sl = slice(hh * HEAD_DIM, (hh + 1) * HEAD_DIM)
            t = acc[:, sl]
            r = lax.rsqrt(jnp.mean(t * t, axis=-1, keepdims=True) + NORM_EPS)
            rm = rm_ref[:, sl]
            t = t * (rm * r + (1.0 - rm)) * cs_ref[:, sl]
            o_ref[:, sl] = (t * cos + pltpu.roll(t, HEAD_DIM // 2, 1) * sin).astype(BF16)

    @pl.when(j >= n_rope_tiles)
    def _():
        o_ref[...] = acc.astype(BF16)


def _in_proj(x_all, mods3, ln_g, w_in_p, cos_f, sin_f, colscale, rmsmask, *, tm, mod_idx):
    m, d = x_all.shape
    ncol = w_in_p.shape[1]
    tn = 512
    kern = functools.partial(_in_proj_kernel, n_rope_tiles=ROPE_COLS // tn, tn=tn)
    return pl.pallas_call(
        kern,
        grid=(m // tm, ncol // tn),
        in_specs=[pl.BlockSpec((tm, d), lambda i, j: (i, 0)),
                  pl.BlockSpec((None, 1, d), lambda i, j: (mod_idx(i), 0, 0)),
                  pl.BlockSpec((None, 1, d), lambda i, j: (mod_idx(i), 0, 1)),
                  pl.BlockSpec((1, d), lambda i, j: (0, 0)),
                  pl.BlockSpec((d, tn), lambda i, j: (0, j)),
                  pl.BlockSpec((tm, HEAD_DIM), lambda i, j: (i, 0)),
                  pl.BlockSpec((tm, HEAD_DIM), lambda i, j: (i, 0)),
                  pl.BlockSpec((1, tn), lambda i, j: (0, j)),
                  pl.BlockSpec((1, tn), lambda i, j: (0, j))],
        out_specs=pl.BlockSpec((tm, tn), lambda i, j: (i, j)),
        out_shape=jax.ShapeDtypeStruct((m, ncol), BF16),
        scratch_shapes=[pltpu.VMEM((tm, d), BF16)],
        compiler_params=_cparams(("arbitrary", "arbitrary")),
        name="in_proj",
    )(x_all, mods3, mods3, ln_g.reshape(1, d), w_in_p, cos_f, sin_f,
      colscale.reshape(1, ncol), rmsmask.reshape(1, ncol))


def _retention_kernel(q_ref, k_ref, v_ref, g_ref, dm_ref, qd_ref, kd_ref, cd_ref,
                      o_ref, s_ref, ob_ref, *, n_lat_chunks):
    d = pl.program_id(1)
    t = pl.program_id(2)

    @pl.when(t == 0)
    def _():
        s_ref[...] = jnp.zeros_like(s_ref)

    cid = jnp.where(d == 1, t, jnp.where(t == 0, 0, 1 + n_lat_chunks - t))
    for h in range(RET_HEADS):
        q = q_ref[:, h * RET_DK:(h + 1) * RET_DK]
        k = k_ref[:, h * RET_DK:(h + 1) * RET_DK]
        v = v_ref[:, h * RET_DV:(h + 1) * RET_DV]
        state = s_ref[h]
        scores = lax.dot_general(q, k, (((1,), (1,)), ((), ())), preferred_element_type=F32)
        p = (scores * dm_ref[h]).astype(BF16)
        qd = (q.astype(F32) * qd_ref[h]).astype(BF16)
        o = (jnp.dot(p, v, preferred_element_type=F32)
             + jnp.dot(qd, state.astype(BF16), preferred_element_type=F32))
        kd = (k.astype(F32) * kd_ref[h]).astype(BF16)
        s_ref[h] = state * cd_ref[h] + lax.dot_general(
            kd, v, (((0,), (0,)), ((), ())), preferred_element_type=F32)
        sl = slice(h * RET_DV, (h + 1) * RET_DV)

        @pl.when(d == 0)
        def _():
            ob_ref[cid, :, sl] = o

        @pl.when(d == 1)
        def _():
            tot = o + ob_ref[cid, :, sl]
            tot = tot * lax.rsqrt(jnp.mean(tot * tot, axis=-1, keepdims=True) + NORM_EPS)
            g = g_ref[:, sl].astype(F32)
            o_ref[:, sl] = (tot * (g * jax.nn.sigmoid(g))).astype(BF16)


def _retention(u, tabs, *, batch, seq, n_ctx):
    m = u.shape[0]
    assert n_ctx == CHUNK and seq % CHUNK == 0
    nc = seq // CHUNK
    lat_blocks = batch * nc
    dmat, qdec, kdec, cdec = tabs

    def rowblk(b, d, t):
        lat = b * nc + jnp.where(d == 1, t - 1, nc - t)
        return jnp.where(t == 0, lat_blocks + b, lat)

    def in_map(col):
        return lambda b, d, t: (rowblk(b, d, t), col)

    def out_map(b, d, t):
        return (jnp.where(d == 1, rowblk(b, 1, t), lat_blocks + b), 0)

    def tab_map(b, d, t):
        return (d, 0, 0, 0)

    qk_w = RET_HEADS * RET_DK
    v_w = RET_HEADS * RET_DV
    kern = functools.partial(_retention_kernel, n_lat_chunks=nc)
    return pl.pallas_call(
        kern,
        grid=(batch, 2, nc + 1),
        in_specs=[pl.BlockSpec((CHUNK, qk_w), in_map(RQ // qk_w)),
                  pl.BlockSpec((CHUNK, qk_w), in_map(RK // qk_w)),
                  pl.BlockSpec((CHUNK, v_w), in_map(RV // v_w)),
                  pl.BlockSpec((CHUNK, v_w), in_map(RG // v_w)),
                  pl.BlockSpec((None, RET_HEADS, CHUNK, CHUNK), tab_map),
                  pl.BlockSpec((None, RET_HEADS, CHUNK, RET_DK), tab_map),
                  pl.BlockSpec((None, RET_HEADS, CHUNK, RET_DK), tab_map),
                  pl.BlockSpec((None, RET_HEADS, 1, RET_DV), tab_map)],
        out_specs=pl.BlockSpec((CHUNK, v_w), out_map),
        out_shape=jax.ShapeDtypeStruct((m, v_w), BF16),
        scratch_shapes=[pltpu.VMEM((RET_HEADS, RET_DK, RET_DV), F32),
                        pltpu.VMEM((nc + 1, CHUNK, v_w), F32)],
        compiler_params=_cparams(("arbitrary", "arbitrary", "arbitrary")),
        name="retention",
    )(u, u, u, u, dmat, qdec, kdec, cdec)


def _retention_tables(decay_logit):
    lg = jax.nn.log_sigmoid(decay_logit.astype(F32))
    lf, lb = lg[0][:, None, None], lg[1][:, None, None]
    pos = jnp.arange(CHUNK, dtype=F32)
    diff = pos[:, None] - pos[None, :]
    d_f = jnp.where(diff >= 0, jnp.exp(jnp.where(diff >= 0, diff, 0.0) * lf), 0.0)
    d_b = jnp.where(diff < 0, jnp.exp(jnp.where(diff < 0, -diff, 0.0) * lb), 0.0)
    ones_k = jnp.ones((1, 1, RET_DK), F32)
    q_f = jnp.exp((pos + 1.0)[None, :, None] * lf) * ones_k
    q_b = jnp.exp((CHUNK - pos)[None, :, None] * lb) * ones_k
    k_f = jnp.exp((CHUNK - 1.0 - pos)[None, :, None] * lf) * ones_k
    k_b = jnp.exp(pos[None, :, None] * lb) * ones_k
    ones_v = jnp.ones((1, 1, RET_DV), F32)
    c_f = jnp.exp(CHUNK * lf) * ones_v
    c_b = jnp.exp(CHUNK * lb) * ones_v
    return (jnp.stack([d_b, d_f]), jnp.stack([q_b, q_f]), jnp.stack([k_b, k_f]),
            jnp.stack([c_b, c_f]))


def _global_attn_kernel(q_ref, kx_ref, vx_ref, kc_ref, vc_ref, o_ref, *, n_lat_tiles):
    i = pl.program_id(2)
    nt = (((1,), (1,)), ((), ()))

    @pl.when(i < n_lat_tiles)
    def _():
        for h in range(GROUP):
            sl = slice(h * HEAD_DIM, (h + 1) * HEAD_DIM)
            q = q_ref[:, sl]
            s_c = lax.dot_general(q, kc_ref[...], nt, preferred_element_type=F32)
            s_x = lax.dot_general(q, kx_ref[...], nt, preferred_element_type=F32)
            mx = jnp.maximum(jnp.max(s_c, axis=-1, keepdims=True), jnp.max(s_x, axis=-1, keepdims=True))
            p_c = jnp.exp(s_c - mx)
            p_x = jnp.exp(s_x - mx)
            den = jnp.sum(p_c, axis=-1, keepdims=True) + jnp.sum(p_x, axis=-1, keepdims=True)
            num = (jnp.dot(p_c.astype(BF16), vc_ref[...], preferred_element_type=F32)
                   + jnp.dot(p_x.astype(BF16), vx_ref[...], preferred_element_type=F32))
            o_ref[:, sl] = (num / den).astype(BF16)

    @pl.when(i >= n_lat_tiles)
    def _():
        for h in range(GROUP):
            sl = slice(h * HEAD_DIM, (h + 1) * HEAD_DIM)
            s_c = lax.dot_general(q_ref[:, sl], kc_ref[...], nt, preferred_element_type=F32)
            p_c = jnp.exp(s_c - jnp.max(s_c, axis=-1, keepdims=True))
            den = jnp.sum(p_c, axis=-1, keepdims=True)
            o_ref[:, sl] = (jnp.dot(p_c.astype(BF16), vc_ref[...], preferred_element_type=F32)
                            / den).astype(BF16)


def _global_attention(u, *, batch, seq, n_ctx, with_ctx):
    m = u.shape[0]
    tq = GA_TQ
    nq = seq // tq
    n_ctx_tiles = n_ctx // tq
    lat_blocks = batch * nq
    gw = GROUP * HEAD_DIM

    def qrow(b, i):
        return jnp.where(i < nq, b * nq + i, lat_blocks + b * n_ctx_tiles + (i - nq))

    kern = functools.partial(_global_attn_kernel, n_lat_tiles=nq)
    return pl.pallas_call(
        kern,
        grid=(batch, GA_KV_HEADS, nq + (n_ctx_tiles if with_ctx else 0)),
        in_specs=[pl.BlockSpec((tq, gw), lambda b, kh, i: (qrow(b, i), GQ // gw + kh)),
                  pl.BlockSpec((seq, HEAD_DIM), lambda b, kh, i: (b, GK // HEAD_DIM + kh)),
                  pl.BlockSpec((seq, HEAD_DIM), lambda b, kh, i: (b, GV // HEAD_DIM + kh)),
                  pl.BlockSpec((n_ctx, HEAD_DIM),
                               lambda b, kh, i: (batch * seq // n_ctx + b, GK // HEAD_DIM + kh)),
                  pl.BlockSpec((n_ctx, HEAD_DIM),
                               lambda b, kh, i: (batch * seq // n_ctx + b, GV // HEAD_DIM + kh))],
        out_specs=pl.BlockSpec((tq, gw), lambda b, kh, i: (qrow(b, i), kh)),
        out_shape=jax.ShapeDtypeStruct((m, GA_HEADS * HEAD_DIM), BF16),
        compiler_params=_cparams(("arbitrary", "arbitrary", "arbitrary")),
        name="global_attention",
    )(u, u, u, u, u)


def _window_attn_kernel(sink_ref, q_ref, kx_ref, vx_ref, kc_ref, vc_ref, o_ref,
                        *, n_lat_tiles, seq):
    kh = pl.program_id(1)
    i = pl.program_id(2)
    span = CHUNK + 2 * WINDOW
    qs = jnp.concatenate([q_ref[:, h * HEAD_DIM:(h + 1) * HEAD_DIM] for h in range(GROUP)], axis=0)
    is_lat = i < n_lat_tiles
    start = jnp.clip(i * CHUNK - WINDOW, 0, seq - span)
    start = pl.multiple_of(jnp.where(is_lat, start, 0), WINDOW)
    kw = kx_ref[pl.ds(start, span), :]
    vw = vx_ref[pl.ds(start, span), :]
    s_win = lax.dot_general(qs, kw, (((1,), (1,)), ((), ())), preferred_element_type=F32)
    s_ctx = lax.dot_general(qs, kc_ref[...], (((1,), (1,)), ((), ())), preferred_element_type=F32)
    rows = GROUP * CHUNK
    q_pos = i * CHUNK + lax.broadcasted_iota(jnp.int32, (rows, span), 0) % CHUNK
    k_pos = start + lax.broadcasted_iota(jnp.int32, (rows, span), 1)
    allowed = (jnp.abs(k_pos - q_pos) <= WINDOW) & is_lat
    s_win = jnp.where(allowed, s_win, -jnp.inf)
    sink = jnp.concatenate(
        [jnp.full((CHUNK, 1), sink_ref[kh * GROUP + h], F32) for h in range(GROUP)], axis=0)
    mx = jnp.maximum(jnp.maximum(jnp.max(s_win, axis=-1, keepdims=True),
                                 jnp.max(s_ctx, axis=-1, keepdims=True)), sink)
    p_win = jnp.exp(s_win - mx)
    p_ctx = jnp.exp(s_ctx - mx)
    den = (jnp.sum(p_win, axis=-1, keepdims=True) + jnp.sum(p_ctx, axis=-1, keepdims=True)
           + jnp.exp(sink - mx))
    out = (jnp.dot(p_win.astype(BF16), vw, preferred_element_type=F32)
           + jnp.dot(p_ctx.astype(BF16), vc_ref[...], preferred_element_type=F32)) / den
    for h in range(GROUP):
        o_ref[:, h * HEAD_DIM:(h + 1) * HEAD_DIM] = out[h * CHUNK:(h + 1) * CHUNK].astype(BF16)


def _window_attention(u, sink, *, batch, seq, n_ctx, with_ctx):
    m = u.shape[0]
    nq = seq // CHUNK
    lat_blocks = batch * nq
    gw = GROUP * HEAD_DIM
    assert seq >= CHUNK + 2 * WINDOW

    def qrow(b, i):
        return jnp.where(i < nq, b * nq + i, lat_blocks + b)

    kern = functools.partial(_window_attn_kernel, n_lat_tiles=nq, seq=seq)
    return pl.pallas_call(
        kern,
        grid=(batch, WA_KV_HEADS, nq + (1 if with_ctx else 0)),
        in_specs=[pl.BlockSpec(memory_space=pltpu.SMEM),
                  pl.BlockSpec((CHUNK, gw), lambda b, kh, i: (qrow(b, i), WQ // gw + kh)),
                  pl.BlockSpec((seq, HEAD_DIM), lambda b, kh, i: (b, WK // HEAD_DIM + kh)),
                  pl.BlockSpec((seq, HEAD_DIM), lambda b, kh, i: (b, WV // HEAD_DIM + kh)),
                  pl.BlockSpec((n_ctx, HEAD_DIM),
                               lambda b, kh, i: (batch * seq // n_ctx + b, WK // HEAD_DIM + kh)),
                  pl.BlockSpec((n_ctx, HEAD_DIM),
                               lambda b, kh, i: (batch * seq // n_ctx + b, WV // HEAD_DIM + kh))],
        out_specs=pl.BlockSpec((CHUNK, gw), lambda b, kh, i: (qrow(b, i), kh)),
        out_shape=jax.ShapeDtypeStruct((m, WA_HEADS * HEAD_DIM), BF16),
        compiler_params=_cparams(("arbitrary", "arbitrary", "arbitrary")),
        name="window_attention",
    )(sink.astype(F32), u, u, u, u, u)


def _merge_kernel(o0_ref, o1_ref, o2_ref, g0_ref, g1_ref, g2_ref, w0_ref, w1_ref, w2_ref, out_ref):
    tot = None
    for o_ref, g_ref, w_ref in ((o0_ref, g0_ref, w0_ref), (o1_ref, g1_ref, w1_ref),
                                (o2_ref, g2_ref, w2_ref)):
        term = jax.nn.sigmoid(g_ref[...].astype(F32)) * jnp.dot(
            o_ref[...], w_ref[...], preferred_element_type=F32)
        tot = term if tot is None else tot + term
    out_ref[...] = tot.astype(BF16)


def _merge(o_ret, o_ga, o_wa, u, w_branch, *, rows, tm, d):
    tn = 512
    bw = w_branch.shape[1]
    gate_blk = GATE // tn
    nd = d // tn

    def o_spec():
        return pl.BlockSpec((tm, bw), lambda i, j: (i, 0))

    def g_spec(br):
        return pl.BlockSpec((tm, tn), lambda i, j: (i, gate_blk + br * nd + j))

    def w_spec(br):
        return pl.BlockSpec((None, bw, tn), lambda i, j: (br, 0, j))

    return pl.pallas_call(
        _merge_kernel,
        grid=(rows // tm, nd),
        in_specs=[o_spec(), o_spec(), o_spec(), g_spec(0), g_spec(1), g_spec(2),
                  w_spec(0), w_spec(1), w_spec(2)],
        out_specs=pl.BlockSpec((tm, tn), lambda i, j: (i, j)),
        out_shape=jax.ShapeDtypeStruct((rows, d), BF16),
        compiler_params=_cparams(("arbitrary", "arbitrary")),
        name="branch_merge",
    )(o_ret, o_ga, o_wa, u, u, u, w_branch, w_branch, w_branch)


def _out_proj_kernel(m_ref, w_ref, x_ref, g_ref, o_ref):
    o_ref[...] = x_ref[...] + g_ref[...] * jnp.dot(m_ref[...], w_ref[...], preferred_element_type=F32)


def _out_proj(mix, w_out, x_all, mods3, *, rows, tm, d, mod_idx):
    tn = 512
    nd = d // tn
    return pl.pallas_call(
        _out_proj_kernel,
        grid=(rows // tm, nd),
        in_specs=[pl.BlockSpec((tm, d), lambda i, j: (i, 0)),
                  pl.BlockSpec((d, tn), lambda i, j: (0, j)),
                  pl.BlockSpec((tm, tn), lambda i, j: (i, j)),
                  pl.BlockSpec((None, 1, tn), lambda i, j: (mod_idx(i), 0, 2 * nd + j))],
        out_specs=pl.BlockSpec((tm, tn), lambda i, j: (i, j)),
        out_shape=jax.ShapeDtypeStruct((rows, d), F32),
        compiler_params=_cparams(("arbitrary", "arbitrary")),
        name="out_proj",
    )(mix, w_out, x_all, mods3)


def _router_kernel(x_ref, sh_ref, sc_ref, g_ref, wr_ref, br_ref, h_ref, idx_ref, wt_ref, *, n_experts):
    x = x_ref[...]
    y = x * lax.rsqrt(jnp.mean(x * x, axis=-1, keepdims=True) + NORM_EPS) * g_ref[...]
    h = y * (1.0 + sc_ref[...]) + sh_ref[...]
    h_ref[...] = h.astype(BF16)
    logits = jnp.dot(h, wr_ref[...], preferred_element_type=F32,
                     precision=lax.Precision.HIGHEST) + br_ref[...]
    lane = lax.broadcasted_iota(jnp.int32, logits.shape, 1).astype(F32)
    vals, ids = [], []
    cur = logits
    for _ in range(TOP_K):
        mx = jnp.max(cur, axis=-1, keepdims=True)
        sel = jnp.min(jnp.where(cur == mx, lane, float(n_experts)), axis=-1, keepdims=True)
        vals.append(mx)
        ids.append(sel)
        cur = jnp.where(lane == sel, -jnp.inf, cur)
    e = [jnp.exp(v - vals[0]) for v in vals]
    den = e[0] + e[1] + e[2] + e[3]
    for k in range(TOP_K):
        idx_ref[:, k:k + 1] = ids[k].astype(jnp.int32)
        wt_ref[:, k:k + 1] = e[k] / den


def _router(x_new, mods3, ln_g, w_router, b_router, *, rows, tm, d, mod_idx):
    n_experts = w_router.shape[1]
    kern = functools.partial(_router_kernel, n_experts=n_experts)
    return pl.pallas_call(
        kern,
        grid=(rows // tm,),
        in_specs=[pl.BlockSpec((tm, d), lambda i: (i, 0)),
                  pl.BlockSpec((None, 1, d), lambda i: (mod_idx(i), 0, 3)),
                  pl.BlockSpec((None, 1, d), lambda i: (mod_idx(i), 0, 4)),
                  pl.BlockSpec((1, d), lambda i: (0, 0)),
                  pl.BlockSpec((d, n_experts), lambda i: (0, 0)),
                  pl.BlockSpec((1, n_experts), lambda i: (0, 0))],
        out_specs=[pl.BlockSpec((tm, d), lambda i: (i, 0)),
                   pl.BlockSpec((tm, TOP_K), lambda i: (i, 0)),
                   pl.BlockSpec((tm, TOP_K), lambda i: (i, 0))],
        out_shape=[jax.ShapeDtypeStruct((rows, d), BF16),
                   jax.ShapeDtypeStruct((rows, TOP_K), jnp.int32),
                   jax.ShapeDtypeStruct((rows, TOP_K), F32)],
        compiler_params=_cparams(("arbitrary",)),
        name="norm2_router",
    )(x_new, mods3, mods3, ln_g.reshape(1, d), w_router, b_router.reshape(1, n_experts))


def _moe_up_kernel(ie_ref, ic_ref, it_ref, in_ref, if_ref, x_ref, wg_ref, wl_ref, bg_ref, bl_ref,
                   o_ref, wg_s, wl_s):
    i = pl.program_id(0)

    @pl.when(if_ref[i] == 1)
    def _():
        wg_s[...] = wg_ref[...].astype(BF16)
        wl_s[...] = wl_ref[...].astype(BF16)

    def compute(n):
        x = x_ref[0:n, :]
        glu = jnp.dot(x, wg_s[...], preferred_element_type=F32) + bg_ref[...]
        lin = jnp.dot(x, wl_s[...], preferred_element_type=F32) + bl_ref[...]
        glu = jnp.minimum(glu, SWIGLU_LIMIT)
        lin = jnp.clip(lin, -SWIGLU_LIMIT, SWIGLU_LIMIT)
        o_ref[0:n, :] = (glu * jax.nn.sigmoid(SWIGLU_ALPHA * glu) * (lin + 1.0)).astype(BF16)

    tm = x_ref.shape[0]
    pl.when(in_ref[i] == tm)(lambda: compute(tm))
    pl.when(in_ref[i] == tm // 2)(lambda: compute(tm // 2))


def _moe_up(sched, xs, w_gu, b_gu, *, layer, tm, tf):
    r_pad, d = xs.shape
    n_layers, n_experts, _, f2 = w_gu.shape
    f = f2 // 2
    nc = f // tf
    n_items = sched[0].shape[0]
    grid_spec = pltpu.PrefetchScalarGridSpec(
        num_scalar_prefetch=5,
        grid=(n_items,),
        in_specs=[pl.BlockSpec((tm, d), lambda i, ie, ic, it, nr, fl: (it[i], 0)),
                  pl.BlockSpec((None, None, d, tf), lambda i, ie, ic, it, nr, fl: (layer, ie[i], 0, ic[i])),
                  pl.BlockSpec((None, None, d, tf),
                               lambda i, ie, ic, it, nr, fl: (layer, ie[i], 0, nc + ic[i])),
                  pl.BlockSpec((None, None, 1, tf), lambda i, ie, ic, it, nr, fl: (layer, ie[i], 0, ic[i])),
                  pl.BlockSpec((None, None, 1, tf),
                               lambda i, ie, ic, it, nr, fl: (layer, ie[i], 0, nc + ic[i]))],
        out_specs=pl.BlockSpec((tm, tf), lambda i, ie, ic, it, nr, fl: (it[i], ic[i])),
        scratch_shapes=[pltpu.VMEM((d, tf), BF16), pltpu.VMEM((d, tf), BF16)])
    b4 = b_gu.reshape(n_layers, n_experts, 1, f2)
    return pl.pallas_call(
        _moe_up_kernel,
        grid_spec=grid_spec,
        out_shape=jax.ShapeDtypeStruct((r_pad, f), BF16),
        compiler_params=_cparams(("arbitrary",)),
        name="moe_up",
    )(*sched, xs, w_gu, w_gu, b4, b4)


def _moe_down_kernel(ie_ref, ic_ref, it_ref, in_ref, if_ref, a_ref, w_ref, b_ref, o_ref, w_s):
    i = pl.program_id(0)

    @pl.when(if_ref[i] == 1)
    def _():
        w_s[...] = w_ref[...].astype(BF16)

    def compute(n):
        y = jnp.dot(a_ref[0:n, :], w_s[...], preferred_element_type=F32) + b_ref[...]
        o_ref[0:n, :] = y.astype(BF16)

    tm = a_ref.shape[0]
    pl.when(in_ref[i] == tm)(lambda: compute(tm))
    pl.when(in_ref[i] == tm // 2)(lambda: compute(tm // 2))


def _moe_down(sched, act, w_down, b_down, *, layer, tm, tn):
    r_pad, f = act.shape
    n_layers, n_experts, _, d = w_down.shape
    n_items = sched[0].shape[0]
    grid_spec = pltpu.PrefetchScalarGridSpec(
        num_scalar_prefetch=5,
        grid=(n_items,),
        in_specs=[pl.BlockSpec((tm, f), lambda i, ie, ic, it, nr, fl: (it[i], 0)),
                  pl.BlockSpec((None, None, f, tn), lambda i, ie, ic, it, nr, fl: (layer, ie[i], 0, ic[i])),
                  pl.BlockSpec((None, None, 1, tn), lambda i, ie, ic, it, nr, fl: (layer, ie[i], 0, ic[i]))],
        out_specs=pl.BlockSpec((tm, tn), lambda i, ie, ic, it, nr, fl: (it[i], ic[i])),
        scratch_shapes=[pltpu.VMEM((f, tn), BF16)])
    return pl.pallas_call(
        _moe_down_kernel,
        grid_spec=grid_spec,
        out_shape=jax.ShapeDtypeStruct((r_pad, d), BF16),
        compiler_params=_cparams(("arbitrary",)),
        name="moe_down",
    )(*sched, act, w_down, b_down.reshape(n_layers, n_experts, 1, d))


def _moe_schedule(idx, *, n_experts, tm, n_chunks):
    n_tok = idx.shape[0]
    n_pairs = n_tok * TOP_K
    r_pad = n_pairs + n_experts * tm
    n_tiles = r_pad // tm
    half = tm // 2
    flat_e = idx.reshape(-1)
    experts = jnp.arange(n_experts, dtype=jnp.int32)
    onehot = (flat_e[:, None] == experts[None, :]).astype(jnp.int32)
    csum = jnp.cumsum(onehot, axis=0)
    rank = jnp.sum(csum * onehot, axis=1) - 1
    counts = csum[-1]
    ntiles = (counts + tm - 1) // tm
    tile_end = jnp.cumsum(ntiles)
    tile_start = tile_end - ntiles
    dest = jnp.sum(onehot * tile_start[None, :], axis=1) * tm + rank
    src_tok = jnp.zeros((r_pad,), jnp.int32).at[dest].set(
        jnp.arange(n_pairs, dtype=jnp.int32) // TOP_K, unique_indices=True, mode="promise_in_bounds")
    n_used = tile_end[-1]
    n_items = n_chunks * n_tiles
    item = jnp.arange(n_items, dtype=jnp.int32)
    valid = item < n_chunks * n_used
    ic_ = jnp.minimum(item, jnp.maximum(n_chunks * n_used - 1, 0))
    e = jnp.minimum(jnp.sum((ic_[:, None] >= n_chunks * tile_end[None, :]).astype(jnp.int32), axis=1),
                    n_experts - 1)
    sel = (e[:, None] == experts[None, :]).astype(jnp.int32)
    nt_e = jnp.maximum(jnp.sum(sel * ntiles[None, :], axis=1), 1)
    ts_e = jnp.sum(sel * tile_start[None, :], axis=1)
    cnt_e = jnp.sum(sel * counts[None, :], axis=1)
    r = ic_ - n_chunks * ts_e
    c = r // nt_e
    lt = r % nt_e
    t = ts_e + lt
    left = cnt_e - lt * tm
    nrows = jnp.where(valid, jnp.where(left <= half, half, tm), 0)
    first = valid & (lt == 0)
    sched = (e.astype(jnp.int32), c.astype(jnp.int32), t.astype(jnp.int32), nrows.astype(jnp.int32),
             first.astype(jnp.int32))
    return sched, src_tok, dest.reshape(n_tok, TOP_K)


def _combine_kernel(x_ref, y_ref, w_ref, g_ref, fg_ref, o_ref, *, d, final):
    w = w_ref[...]
    y = w[:, 0:1] * y_ref[:, 0:d].astype(F32)
    for k in range(1, TOP_K):
        y = y + w[:, k:k + 1] * y_ref[:, k * d:(k + 1) * d].astype(F32)
    x = x_ref[...] + g_ref[...] * y
    if final:
        x = x * lax.rsqrt(jnp.mean(x * x, axis=-1, keepdims=True) + NORM_EPS) * fg_ref[...]
    o_ref[...] = x


def _combine(x_new, ysg, wts, mods3, final_g, *, rows, tm, d, mod_idx, final):
    kern = functools.partial(_combine_kernel, d=d, final=final)
    return pl.pallas_call(
        kern,
        grid=(rows // tm,),
        in_specs=[pl.BlockSpec((tm, d), lambda i: (i, 0)),
                  pl.BlockSpec((tm, TOP_K * d), lambda i: (i, 0)),
                  pl.BlockSpec((tm, TOP_K), lambda i: (i, 0)),
                  pl.BlockSpec((None, 1, d), lambda i: (mod_idx(i), 0, 5)),
                  pl.BlockSpec((1, d), lambda i: (0, 0))],
        out_specs=pl.BlockSpec((tm, d), lambda i: (i, 0)),
        out_shape=jax.ShapeDtypeStruct((rows, d), F32),
        compiler_params=_cparams(("arbitrary",)),
        name="moe_combine",
    )(x_new, ysg, wts, mods3, final_g.reshape(1, d))


def _rope_tables(batch, seq, n_ctx):
    rows = seq // GRID_W
    row = jnp.repeat(jnp.arange(rows, dtype=F32), GRID_W)
    col = jnp.tile(jnp.arange(GRID_W, dtype=F32), rows)
    quarter = HEAD_DIM // 4
    inv_freq = ROPE_THETA ** (-jnp.arange(quarter, dtype=F32) / quarter)
    ang = jnp.concatenate([row[:, None] * inv_freq, col[:, None] * inv_freq], axis=-1)
    cos, sin = jnp.cos(ang), jnp.sin(ang)
    cos_x = jnp.tile(jnp.concatenate([cos, cos], axis=-1), (batch, 1))
    sin_x = jnp.tile(jnp.concatenate([-sin, sin], axis=-1), (batch, 1))
    cos_f = jnp.concatenate([cos_x, jnp.ones((batch * n_ctx, HEAD_DIM), F32)], axis=0)
    sin_f = jnp.concatenate([sin_x, jnp.zeros((batch * n_ctx, HEAD_DIM), F32)], axis=0)
    return cos_f, sin_f


def _column_params(q_gain, k_gain, ncol):
    hp = _head_perm()
    scale = HEAD_DIM ** -0.5
    k_scale = RET_DK ** -0.5
    cs = jnp.ones((ncol,), F32)
    cs = cs.at[RK:RK + RET_HEADS * RET_DK].set(k_scale)
    cs = cs.at[GQ:GQ + GA_HEADS * HEAD_DIM].set(jnp.tile(q_gain.astype(F32)[hp] * scale, GA_HEADS))
    cs = cs.at[GK:GK + GA_KV_HEADS * HEAD_DIM].set(jnp.tile(k_gain.astype(F32)[hp], GA_KV_HEADS))
    cs = cs.at[WQ:WQ + WA_HEADS * HEAD_DIM].set(scale)
    rm = jnp.zeros((ncol,), F32).at[GQ:GQ + GA_HEADS * HEAD_DIM].set(1.0)
    rm = rm.at[GK:GK + GA_KV_HEADS * HEAD_DIM].set(1.0)
    return cs, rm


def kernel(x, c, ctx, c_ctx, ln1_g, ln2_g, w_ada, b_ada, w_in, ret_decay_logit, ga_q_gain,
           ga_k_gain, wa_sink, w_branch, w_out, w_router, b_router, w_gu, b_gu, w_down, b_down,
           final_g):
    batch, seq, d = x.shape
    n_ctx = ctx.shape[1]
    depth = w_ada.shape[0]
    n_experts = w_router.shape[2]
    ncol = w_in.shape[2]
    mx, mc = batch * seq, batch * n_ctx
    m = mx + mc
    tm = min(1024, mc)
    assert seq % tm == 0 and mc % tm == 0 and batch + 1 <= 8
    tm_stream = min(256, tm)
    assert w_gu.shape[3] // 2 == w_down.shape[3] == d

    def mod_idx_for(tile):
        def mod_idx(i):
            return jnp.where(i < mx // tile, i // (seq // tile), batch)
        return mod_idx

    mod_idx = mod_idx_for(tm)
    mod_idx_s = mod_idx_for(tm_stream)

    cc8 = jnp.zeros((8, d), F32).at[:batch].set(c).at[batch].set(c_ctx)
    mods = _modulation(cc8, w_ada, b_ada)
    cos_f, sin_f = _rope_tables(batch, seq, n_ctx)
    perm = _column_perm(d)

    x_all = jnp.concatenate([x.reshape(mx, d), ctx.reshape(mc, d)], axis=0)
    e_tm = 512
    tf = min(1024, d)

    for l in range(depth):
        need_ctx = l < depth - 1
        rows = m if need_ctx else mx
        mods3 = mods[l].reshape(8, 1, 6 * d)
        w_in_p = jnp.take(w_in[l], perm, axis=1).astype(BF16)
        colscale, rmsmask = _column_params(ga_q_gain[l], ga_k_gain[l], ncol)
        u = _in_proj(x_all, mods3, ln1_g[l], w_in_p, cos_f, sin_f, colscale, rmsmask,
                     tm=tm, mod_idx=mod_idx)
        o_ret = _retention(u, _retention_tables(ret_decay_logit[l]), batch=batch, seq=seq, n_ctx=n_ctx)
        o_ga = _global_attention(u, batch=batch, seq=seq, n_ctx=n_ctx, with_ctx=need_ctx)
        o_wa = _window_attention(u, wa_sink[l], batch=batch, seq=seq, n_ctx=n_ctx, with_ctx=need_ctx)
        mix = _merge(o_ret, o_ga, o_wa, u, w_branch[l].astype(BF16), rows=rows, tm=tm, d=d)
        x_new = _out_proj(mix, w_out[l].astype(BF16), x_all, mods3, rows=rows, tm=tm, d=d, mod_idx=mod_idx)
        h2, idx, wts = _router(x_new, mods3, ln2_g[l], w_router[l], b_router[l],
                               rows=rows, tm=tm_stream, d=d, mod_idx=mod_idx_s)
        sched, src_tok, pos = _moe_schedule(
            idx, n_experts=n_experts, tm=e_tm, n_chunks=w_down.shape[2] // tf)
        xs = h2.at[src_tok].get(mode="promise_in_bounds")
        act = _moe_up(sched, xs, w_gu, b_gu, layer=l, tm=e_tm, tf=tf)
        ys = _moe_down(sched, act, w_down, b_down, layer=l, tm=e_tm, tn=tf)
        ysg = ys.at[pos.reshape(-1)].get(mode="promise_in_bounds").reshape(rows, TOP_K * d)
        x_all = _combine(x_new, ysg, wts, mods3, final_g, rows=rows, tm=tm_stream, d=d,
                         mod_idx=mod_idx_s, final=not need_ctx)
    return x_all[:mx].reshape(batch, seq, d)
```

```python
import functools

import jax
import jax.numpy as jnp
from jax import lax
from jax.experimental import pallas as pl
from jax.experimental.pallas import tpu as pltpu

GRID_W = 64
HEAD_DIM = 128
RET_HEADS = 4
RET_DK = 128
RET_DV = 256
GA_HEADS = 8
GA_KV_HEADS = 2
WA_HEADS = 8
WA_KV_HEADS = 2
WINDOW = 128
N_BRANCHES = 3
TOP_K = 4
SWIGLU_LIMIT = 7.0
SWIGLU_ALPHA = 1.702
ROPE_THETA = 10000.0
NORM_EPS = 1e-6
LOG2_E = 1.4426950408889634

LANES = 128
CHUNK = 256
GROUP = GA_HEADS // GA_KV_HEADS
GA_TQ = 256
VMEM_LIMIT = 56 * 1024 * 1024

RQ, RK, GQ, WQ, GK, WK = 0, 512, 1024, 2048, 3072, 3328
ROPE_COLS = 3584
GV, WV, RV, RG, GATE = 3584, 3840, 4096, 5120, 6144
_ORQ, _ORK, _ORV, _ORG, _OGQ, _OGK, _OGV, _OWQ, _OWK, _OWV, _OGATE = (
    0, 512, 1024, 2048, 3072, 4096, 4352, 4608, 5632, 5888, 6144)

F32 = jnp.float32
BF16 = jnp.bfloat16


def _cparams(sem, vmem=VMEM_LIMIT):
    return pltpu.CompilerParams(dimension_semantics=sem, vmem_limit_bytes=vmem)


def _head_perm():
    j = jnp.arange(HEAD_DIM)
    return jnp.where(j < HEAD_DIM // 2, 2 * j, 2 * (j - HEAD_DIM // 2) + 1)


def _regroup_columns(w):
    d_model = w.shape[0]
    half = HEAD_DIM // 2

    def rope_heads(start, n_heads):
        blk = w[:, start:start + n_heads * HEAD_DIM].reshape(d_model, n_heads, half, 2)
        return blk.transpose(0, 1, 3, 2).reshape(d_model, n_heads * HEAD_DIM)

    def plain(start, size):
        return w[:, start:start + size]

    return jnp.concatenate([
        rope_heads(_ORQ, RET_HEADS), rope_heads(_ORK, RET_HEADS),
        rope_heads(_OGQ, GA_HEADS), rope_heads(_OWQ, WA_HEADS),
        rope_heads(_OGK, GA_KV_HEADS), rope_heads(_OWK, WA_KV_HEADS),
        plain(_OGV, GA_KV_HEADS * HEAD_DIM), plain(_OWV, WA_KV_HEADS * HEAD_DIM),
        plain(_ORV, RET_HEADS * RET_DV), plain(_ORG, RET_HEADS * RET_DV),
        plain(_OGATE, w.shape[1] - _OGATE)], axis=1).astype(BF16)


def _mod_kernel(c_ref, w_ref, b_ref, o_ref):
    c = c_ref[...]
    s = (c * jax.nn.sigmoid(c)).astype(BF16)
    o_ref[...] = jnp.dot(s, w_ref[...].astype(BF16), preferred_element_type=F32) + b_ref[...]


def _modulation(cc8, w_ada, b_ada):
    n_layers, d, n = w_ada.shape
    tn = min(1024, n)
    return pl.pallas_call(
        _mod_kernel,
        grid=(n_layers, n // tn),
        in_specs=[pl.BlockSpec((8, d), lambda l, j: (0, 0)),
                  pl.BlockSpec((None, d, tn), lambda l, j: (l, 0, j)),
                  pl.BlockSpec((None, 1, tn), lambda l, j: (l, 0, j))],
        out_specs=pl.BlockSpec((None, 8, tn), lambda l, j: (l, 0, j)),
        out_shape=jax.ShapeDtypeStruct((n_layers, 8, n), F32),
        compiler_params=_cparams(("arbitrary", "arbitrary")),
        name="modulation",
    )(cc8, w_ada, b_ada.reshape(n_layers, 1, n))


def _in_proj_kernel(x_ref, sh_ref, sc_ref, g_ref, w_ref, cos_ref, sin_ref, cs_ref, rm_ref,
                    o_ref, h_ref, *, n_rope_tiles, tn):
    j = pl.program_id(1)

    @pl.when(j == 0)
    def _():
        x = x_ref[...]
        y = x * lax.rsqrt(jnp.mean(x * x, axis=-1, keepdims=True) + NORM_EPS) * g_ref[...]
        h_ref[...] = (y * (1.0 + sc_ref[...]) + sh_ref[...]).astype(BF16)

    acc = jnp.dot(h_ref[...], w_ref[...], preferred_element_type=F32)

    @pl.when(j < n_rope_tiles)
    def _():
        cos = cos_ref[...]
        sin = sin_ref[...]
        for hh in range(tn // HEAD_DIM):
            sl = slice(hh * HEAD_DIM, (hh + 1) * HEAD_DIM)
            t = acc[:, sl]
            r = lax.rsqrt(jnp.mean(t * t, axis=-1, keepdims=True) + NORM_EPS)
            rm = rm_ref[:, sl]
            t = t * (rm * r + (1.0 - rm)) * cs_ref[:, sl]
            o_ref[:, sl] = (t * cos + pltpu.roll(t, HEAD_DIM // 2, 1) * sin).astype(BF16)

    @pl.when(j >= n_rope_tiles)
    def _():
        o_ref[...] = acc.astype(BF16)


def _in_proj(x_all, mods3, ln_g, w_in_p, cos_f, sin_f, colscale, rmsmask, *, tm, mod_idx):
    m, d = x_all.shape
    ncol = w_in_p.shape[1]
    tn = 512
    kern = functools.partial(_in_proj_kernel, n_rope_tiles=ROPE_COLS // tn, tn=tn)
    return pl.pallas_call(
        kern,
        grid=(m // tm, ncol // tn),
        in_specs=[pl.BlockSpec((tm, d), lambda i, j: (i, 0)),
                  pl.BlockSpec((None, 1, d), lambda i, j: (mod_idx(i), 0, 0)),
                  pl.BlockSpec((None, 1, d), lambda i, j: (mod_idx(i), 0, 1)),
                  pl.BlockSpec((1, d), lambda i, j: (0, 0)),
                  pl.BlockSpec((d, tn), lambda i, j: (0, j)),
                  pl.BlockSpec((tm, HEAD_DIM), lambda i, j: (i, 0)),
                  pl.BlockSpec((tm, HEAD_DIM), lambda i, j: (i, 0)),
                  pl.BlockSpec((1, tn), lambda i, j: (0, j)),
                  pl.BlockSpec((1, tn), lambda i, j: (0, j))],
        out_specs=pl.BlockSpec((tm, tn), lambda i, j: (i, j)),
        out_shape=jax.ShapeDtypeStruct((m, ncol), BF16),
        scratch_shapes=[pltpu.VMEM((tm, d), BF16)],
        compiler_params=_cparams(("arbitrary", "arbitrary")),
        name="in_proj",
    )(x_all, mods3, mods3, ln_g.reshape(1, d), w_in_p, cos_f, sin_f,
      colscale.reshape(1, ncol), rmsmask.reshape(1, ncol))


def _retention_kernel(q_ref, k_ref, v_ref, g_ref, dm_ref, qd_ref, kd_ref, cd_ref,
                      o_ref, s_ref, ob_ref, *, n_lat_chunks):
    d = pl.program_id(1)
    t = pl.program_id(2)

    @pl.when(t == 0)
    def _():
        s_ref[...] = jnp.zeros_like(s_ref)

    cid = jnp.where(d == 1, t, jnp.where(t == 0, 0, 1 + n_lat_chunks - t))
    for h in range(RET_HEADS):
        q = q_ref[:, h * RET_DK:(h + 1) * RET_DK]
        k = k_ref[:, h * RET_DK:(h + 1) * RET_DK]
        v = v_ref[:, h * RET_DV:(h + 1) * RET_DV]
        state = s_ref[h]
        scores = lax.dot_general(q, k, (((1,), (1,)), ((), ())), preferred_element_type=F32)
        p = (scores * dm_ref[h]).astype(BF16)
        qd = (q.astype(F32) * qd_ref[h]).astype(BF16)
        o = (jnp.dot(p, v, preferred_element_type=F32)
             + jnp.dot(qd, state.astype(BF16), preferred_element_type=F32))
        kd = (k.astype(F32) * kd_ref[h]).astype(BF16)
        s_ref[h] = state * cd_ref[h] + lax.dot_general(
            kd, v, (((0,), (0,)), ((), ())), preferred_element_type=F32)
        sl = slice(h * RET_DV, (h + 1) * RET_DV)

        @pl.when(d == 0)
        def _():
            ob_ref[cid, :, sl] = o

        @pl.when(d == 1)
        def _():
            tot = o + ob_ref[cid, :, sl]
            tot = tot * lax.rsqrt(jnp.mean(tot * tot, axis=-1, keepdims=True) + NORM_EPS)
            g = g_ref[:, sl].astype(F32)
            o_ref[:, sl] = (tot * (g * jax.nn.sigmoid(g))).astype(BF16)


def _retention(u, tabs, *, batch, seq, n_ctx):
    m = u.shape[0]
    assert n_ctx == CHUNK and seq % CHUNK == 0
    nc = seq // CHUNK
    lat_blocks = batch * nc
    dmat, qdec, kdec, cdec = tabs

    def rowblk(b, d, t):
        lat = b * nc + jnp.where(d == 1, t - 1, nc - t)
        return jnp.where(t == 0, lat_blocks + b, lat)

    def in_map(col):
        return lambda b, d, t: (rowblk(b, d, t), col)

    def out_map(b, d, t):
        return (jnp.where(d == 1, rowblk(b, 1, t), lat_blocks + b), 0)

    def tab_map(b, d, t):
        return (d, 0, 0, 0)

    qk_w = RET_HEADS * RET_DK
    v_w = RET_HEADS * RET_DV
    kern = functools.partial(_retention_kernel, n_lat_chunks=nc)
    return pl.pallas_call(
        kern,
        grid=(batch, 2, nc + 1),
        in_specs=[pl.BlockSpec((CHUNK, qk_w), in_map(RQ // qk_w)),
                  pl.BlockSpec((CHUNK, qk_w), in_map(RK // qk_w)),
                  pl.BlockSpec((CHUNK, v_w), in_map(RV // v_w)),
                  pl.BlockSpec((CHUNK, v_w), in_map(RG // v_w)),
                  pl.BlockSpec((None, RET_HEADS, CHUNK, CHUNK), tab_map),
                  pl.BlockSpec((None, RET_HEADS, CHUNK, RET_DK), tab_map),
                  pl.BlockSpec((None, RET_HEADS, CHUNK, RET_DK), tab_map),
                  pl.BlockSpec((None, RET_HEADS, 1, RET_DV), tab_map)],
        out_specs=pl.BlockSpec((CHUNK, v_w), out_map),
        out_shape=jax.ShapeDtypeStruct((m, v_w), BF16),
        scratch_shapes=[pltpu.VMEM((RET_HEADS, RET_DK, RET_DV), F32),
                        pltpu.VMEM((nc + 1, CHUNK, v_w), F32)],
        compiler_params=_cparams(("arbitrary", "arbitrary", "arbitrary")),
        name="retention",
    )(u, u, u, u, dmat, qdec, kdec, cdec)


def _retention_tables(decay_logit):
    lg = jax.nn.log_sigmoid(decay_logit.astype(F32))
    lf, lb = lg[0][:, None, None], lg[1][:, None, None]
    pos = jnp.arange(CHUNK, dtype=F32)
    diff = pos[:, None] - pos[None, :]
    d_f = jnp.where(diff >= 0, jnp.exp(jnp.where(diff >= 0, diff, 0.0) * lf), 0.0)
    d_b = jnp.where(diff < 0, jnp.exp(jnp.where(diff < 0, -diff, 0.0) * lb), 0.0)
    ones_k = jnp.ones((1, 1, RET_DK), F32)
    q_f = jnp.exp((pos + 1.0)[None, :, None] * lf) * ones_k
    q_b = jnp.exp((CHUNK - pos)[None, :, None] * lb) * ones_k
    k_f = jnp.exp((CHUNK - 1.0 - pos)[None, :, None] * lf) * ones_k
    k_b = jnp.exp(pos[None, :, None] * lb) * ones_k
    ones_v = jnp.ones((1, 1, RET_DV), F32)
    c_f = jnp.exp(CHUNK * lf) * ones_v
    c_b = jnp.exp(CHUNK * lb) * ones_v
    return (jnp.stack([d_b, d_f]), jnp.stack([q_b, q_f]), jnp.stack([k_b, k_f]),
            jnp.stack([c_b, c_f]))


def _global_attn_kernel(q_ref, kx_ref, vx_ref, kc_ref, vc_ref, o_ref, *, n_lat_tiles):
    i = pl.program_id(2)
    nt = (((1,), (1,)), ((), ()))

    def scores(h):
        q = q_ref[:, h * HEAD_DIM:(h + 1) * HEAD_DIM]
        return (lax.dot_general(q, kc_ref[...], nt, preferred_element_type=F32),
                lax.dot_general(q, kx_ref[...], nt, preferred_element_type=F32))

    @pl.when(i < n_lat_tiles)
    def _():
        nxt = scores(0)
        for h in range(GROUP):
            s_c, s_x = nxt
            if h + 1 < GROUP:
                nxt = scores(h + 1)
            mx = jnp.maximum(jnp.max(s_c, axis=-1, keepdims=True), jnp.max(s_x, axis=-1, keepdims=True))
            p_c = jnp.exp2(s_c - mx)
            p_x = jnp.exp2(s_x - mx)
            den = jnp.sum(p_c, axis=-1, keepdims=True) + jnp.sum(p_x, axis=-1, keepdims=True)
            num = (jnp.dot(p_c.astype(BF16), vc_ref[...], preferred_element_type=F32)
                   + jnp.dot(p_x.astype(BF16), vx_ref[...], preferred_element_type=F32))
            o_ref[:, h * HEAD_DIM:(h + 1) * HEAD_DIM] = (num / den).astype(BF16)

    @pl.when(i >= n_lat_tiles)
    def _():
        for h in range(GROUP):
            sl = slice(h * HEAD_DIM, (h + 1) * HEAD_DIM)
            s_c = lax.dot_general(q_ref[:, sl], kc_ref[...], nt, preferred_element_type=F32)
            p_c = jnp.exp2(s_c - jnp.max(s_c, axis=-1, keepdims=True))
            den = jnp.sum(p_c, axis=-1, keepdims=True)
            o_ref[:, sl] = (jnp.dot(p_c.astype(BF16), vc_ref[...], preferred_element_type=F32)
                            / den).astype(BF16)


def _global_attention(u, *, batch, seq, n_ctx, with_ctx):
    m = u.shape[0]
    tq = GA_TQ
    nq = seq // tq
    n_ctx_tiles = n_ctx // tq
    lat_blocks = batch * nq
    gw = GROUP * HEAD_DIM

    def qrow(b, i):
        return jnp.where(i < nq, b * nq + i, lat_blocks + b * n_ctx_tiles + (i - nq))

    kern = functools.partial(_global_attn_kernel, n_lat_tiles=nq)
    return pl.pallas_call(
        kern,
        grid=(batch, GA_KV_HEADS, nq + (n_ctx_tiles if with_ctx else 0)),
        in_specs=[pl.BlockSpec((tq, gw), lambda b, kh, i: (qrow(b, i), GQ // gw + kh)),
                  pl.BlockSpec((seq, HEAD_DIM), lambda b, kh, i: (b, GK // HEAD_DIM + kh)),
                  pl.BlockSpec((seq, HEAD_DIM), lambda b, kh, i: (b, GV // HEAD_DIM + kh)),
                  pl.BlockSpec((n_ctx, HEAD_DIM),
                               lambda b, kh, i: (batch * seq // n_ctx + b, GK // HEAD_DIM + kh)),
                  pl.BlockSpec((n_ctx, HEAD_DIM),
                               lambda b, kh, i: (batch * seq // n_ctx + b, GV // HEAD_DIM + kh))],
        out_specs=pl.BlockSpec((tq, gw), lambda b, kh, i: (qrow(b, i), kh)),
        out_shape=jax.ShapeDtypeStruct((m, GA_HEADS * HEAD_DIM), BF16),
        compiler_params=_cparams(("arbitrary", "arbitrary", "arbitrary")),
        name="global_attention",
    )(u, u, u, u, u)


def _window_attn_kernel(sink_ref, q_ref, kx_ref, vx_ref, kc_ref, vc_ref, o_ref,
                        *, n_lat_tiles, seq):
    kh = pl.program_id(1)
    i = pl.program_id(2)
    span = CHUNK + 2 * WINDOW
    qs = jnp.concatenate([q_ref[:, h * HEAD_DIM:(h + 1) * HEAD_DIM] for h in range(GROUP)], axis=0)
    is_lat = i < n_lat_tiles
    start = jnp.clip(i * CHUNK - WINDOW, 0, seq - span)
    start = pl.multiple_of(jnp.where(is_lat, start, 0), WINDOW)
    kw = kx_ref[pl.ds(start, span), :]
    vw = vx_ref[pl.ds(start, span), :]
    s_win = lax.dot_general(qs, kw, (((1,), (1,)), ((), ())), preferred_element_type=F32)
    s_ctx = lax.dot_general(qs, kc_ref[...], (((1,), (1,)), ((), ())), preferred_element_type=F32)
    rows = GROUP * CHUNK
    q_pos = i * CHUNK + lax.broadcasted_iota(jnp.int32, (rows, span), 0) % CHUNK
    k_pos = start + lax.broadcasted_iota(jnp.int32, (rows, span), 1)
    allowed = (jnp.abs(k_pos - q_pos) <= WINDOW) & is_lat
    s_win = jnp.where(allowed, s_win, -jnp.inf)
    sink = jnp.concatenate(
        [jnp.full((CHUNK, 1), sink_ref[kh * GROUP + h], F32) for h in range(GROUP)], axis=0)
    mx = jnp.maximum(jnp.maximum(jnp.max(s_win, axis=-1, keepdims=True),
                                 jnp.max(s_ctx, axis=-1, keepdims=True)), sink)
    p_win = jnp.exp(s_win - mx)
    p_ctx = jnp.exp(s_ctx - mx)
    den = (jnp.sum(p_win, axis=-1, keepdims=True) + jnp.sum(p_ctx, axis=-1, keepdims=True)
           + jnp.exp(sink - mx))
    out = (jnp.dot(p_win.astype(BF16), vw, preferred_element_type=F32)
           + jnp.dot(p_ctx.astype(BF16), vc_ref[...], preferred_element_type=F32)) / den
    for h in range(GROUP):
        o_ref[:, h * HEAD_DIM:(h + 1) * HEAD_DIM] = out[h * CHUNK:(h + 1) * CHUNK].astype(BF16)


def _window_attention(u, sink, *, batch, seq, n_ctx, with_ctx):
    m = u.shape[0]
    nq = seq // CHUNK
    lat_blocks = batch * nq
    gw = GROUP * HEAD_DIM
    assert seq >= CHUNK + 2 * WINDOW

    def qrow(b, i):
        return jnp.where(i < nq, b * nq + i, lat_blocks + b)

    kern = functools.partial(_window_attn_kernel, n_lat_tiles=nq, seq=seq)
    return pl.pallas_call(
        kern,
        grid=(batch, WA_KV_HEADS, nq + (1 if with_ctx else 0)),
        in_specs=[pl.BlockSpec(memory_space=pltpu.SMEM),
                  pl.BlockSpec((CHUNK, gw), lambda b, kh, i: (qrow(b, i), WQ // gw + kh)),
                  pl.BlockSpec((seq, HEAD_DIM), lambda b, kh, i: (b, WK // HEAD_DIM + kh)),
                  pl.BlockSpec((seq, HEAD_DIM), lambda b, kh, i: (b, WV // HEAD_DIM + kh)),
                  pl.BlockSpec((n_ctx, HEAD_DIM),
                               lambda b, kh, i: (batch * seq // n_ctx + b, WK // HEAD_DIM + kh)),
                  pl.BlockSpec((n_ctx, HEAD_DIM),
                               lambda b, kh, i: (batch * seq // n_ctx + b, WV // HEAD_DIM + kh))],
        out_specs=pl.BlockSpec((CHUNK, gw), lambda b, kh, i: (qrow(b, i), kh)),
        out_shape=jax.ShapeDtypeStruct((m, WA_HEADS * HEAD_DIM), BF16),
        compiler_params=_cparams(("arbitrary", "arbitrary", "arbitrary")),
        name="window_attention",
    )(sink.astype(F32), u, u, u, u, u)


def _merge_kernel(o0_ref, o1_ref, o2_ref, g0_ref, g1_ref, g2_ref, w0_ref, w1_ref, w2_ref, out_ref):
    tot = None
    for o_ref, g_ref, w_ref in ((o0_ref, g0_ref, w0_ref), (o1_ref, g1_ref, w1_ref),
                                (o2_ref, g2_ref, w2_ref)):
        term = jax.nn.sigmoid(g_ref[...].astype(F32)) * jnp.dot(
            o_ref[...], w_ref[...], preferred_element_type=F32)
        tot = term if tot is None else tot + term
    out_ref[...] = tot.astype(BF16)


def _merge(o_ret, o_ga, o_wa, u, w_branch, *, rows, tm, d):
    tn = 512
    bw = w_branch.shape[1]
    gate_blk = GATE // tn
    nd = d // tn

    def o_spec():
        return pl.BlockSpec((tm, bw), lambda i, j: (i, 0))

    def g_spec(br):
        return pl.BlockSpec((tm, tn), lambda i, j: (i, gate_blk + br * nd + j))

    def w_spec(br):
        return pl.BlockSpec((None, bw, tn), lambda i, j: (br, 0, j))

    return pl.pallas_call(
        _merge_kernel,
        grid=(rows // tm, nd),
        in_specs=[o_spec(), o_spec(), o_spec(), g_spec(0), g_spec(1), g_spec(2),
                  w_spec(0), w_spec(1), w_spec(2)],
        out_specs=pl.BlockSpec((tm, tn), lambda i, j: (i, j)),
        out_shape=jax.ShapeDtypeStruct((rows, d), BF16),
        compiler_params=_cparams(("arbitrary", "arbitrary")),
        name="branch_merge",
    )(o_ret, o_ga, o_wa, u, u, u, w_branch, w_branch, w_branch)


def _out_proj_kernel(m_ref, w_ref, x_ref, g_ref, o_ref):
    o_ref[...] = x_ref[...] + g_ref[...] * jnp.dot(m_ref[...], w_ref[...], preferred_element_type=F32)


def _out_proj(mix, w_out, x_all, mods3, *, rows, tm, d, mod_idx):
    tn = 512
    nd = d // tn
    return pl.pallas_call(
        _out_proj_kernel,
        grid=(rows // tm, nd),
        in_specs=[pl.BlockSpec((tm, d), lambda i, j: (i, 0)),
                  pl.BlockSpec((d, tn), lambda i, j: (0, j)),
                  pl.BlockSpec((tm, tn), lambda i, j: (i, j)),
                  pl.BlockSpec((None, 1, tn), lambda i, j: (mod_idx(i), 0, 2 * nd + j))],
        out_specs=pl.BlockSpec((tm, tn), lambda i, j: (i, j)),
        out_shape=jax.ShapeDtypeStruct((rows, d), F32),
        compiler_params=_cparams(("arbitrary", "arbitrary")),
        name="out_proj",
    )(mix, w_out, x_all, mods3)


def _router_kernel(x_ref, sh_ref, sc_ref, g_ref, wr_ref, br_ref, h_ref, idx_ref, wt_ref, *, n_experts):
    x = x_ref[...]
    y = x * lax.rsqrt(jnp.mean(x * x, axis=-1, keepdims=True) + NORM_EPS) * g_ref[...]
    h = y * (1.0 + sc_ref[...]) + sh_ref[...]
    h_ref[...] = h.astype(BF16)
    logits = jnp.dot(h, wr_ref[...], preferred_element_type=F32,
                     precision=lax.Precision.HIGHEST) + br_ref[...]
    lane = lax.broadcasted_iota(jnp.int32, logits.shape, 1).astype(F32)
    vals, ids = [], []
    cur = logits
    for _ in range(TOP_K):
        mx = jnp.max(cur, axis=-1, keepdims=True)
        sel = jnp.min(jnp.where(cur == mx, lane, float(n_experts)), axis=-1, keepdims=True)
        vals.append(mx)
        ids.append(sel)
        cur = jnp.where(lane == sel, -jnp.inf, cur)
    e = [jnp.exp(v - vals[0]) for v in vals]
    den = e[0] + e[1] + e[2] + e[3]
    for k in range(TOP_K):
        idx_ref[:, k:k + 1] = ids[k].astype(jnp.int32)
        wt_ref[:, k:k + 1] = e[k] / den


def _router(x_new, mods3, ln_g, w_router, b_router, *, rows, tm, d, mod_idx):
    n_experts = w_router.shape[1]
    kern = functools.partial(_router_kernel, n_experts=n_experts)
    return pl.pallas_call(
        kern,
        grid=(rows // tm,),
        in_specs=[pl.BlockSpec((tm, d), lambda i: (i, 0)),
                  pl.BlockSpec((None, 1, d), lambda i: (mod_idx(i), 0, 3)),
                  pl.BlockSpec((None, 1, d), lambda i: (mod_idx(i), 0, 4)),
                  pl.BlockSpec((1, d), lambda i: (0, 0)),
                  pl.BlockSpec((d, n_experts), lambda i: (0, 0)),
                  pl.BlockSpec((1, n_experts), lambda i: (0, 0))],
        out_specs=[pl.BlockSpec((tm, d), lambda i: (i, 0)),
                   pl.BlockSpec((tm, TOP_K), lambda i: (i, 0)),
                   pl.BlockSpec((tm, TOP_K), lambda i: (i, 0))],
        out_shape=[jax.ShapeDtypeStruct((rows, d), BF16),
                   jax.ShapeDtypeStruct((rows, TOP_K), jnp.int32),
                   jax.ShapeDtypeStruct((rows, TOP_K), F32)],
        compiler_params=_cparams(("arbitrary",)),
        name="norm2_router",
    )(x_new, mods3, mods3, ln_g.reshape(1, d), w_router, b_router.reshape(1, n_experts))


def _moe_up_kernel(ie_ref, ic_ref, it_ref, in_ref, if_ref, x_ref, wg_ref, wl_ref, bg_ref, bl_ref,
                   o_ref, wg_s, wl_s):
    i = pl.program_id(0)

    @pl.when(if_ref[i] == 1)
    def _():
        wg_s[...] = wg_ref[...].astype(BF16)
        wl_s[...] = wl_ref[...].astype(BF16)

    def compute(n):
        x = x_ref[0:n, :]
        glu = jnp.dot(x, wg_s[...], preferred_element_type=F32) + bg_ref[...]
        lin = jnp.dot(x, wl_s[...], preferred_element_type=F32) + bl_ref[...]
        glu = jnp.minimum(glu, SWIGLU_LIMIT)
        lin = jnp.clip(lin, -SWIGLU_LIMIT, SWIGLU_LIMIT)
        o_ref[0:n, :] = (glu * jax.nn.sigmoid(SWIGLU_ALPHA * glu) * (lin + 1.0)).astype(BF16)

    tm = x_ref.shape[0]
    pl.when(in_ref[i] == tm)(lambda: compute(tm))
    pl.when(in_ref[i] == tm // 2)(lambda: compute(tm // 2))


def _moe_up(sched, xs, w_gu, b_gu, *, layer, tm, tf):
    r_pad, d = xs.shape
    n_layers, n_experts, _, f2 = w_gu.shape
    f = f2 // 2
    nc = f // tf
    n_items = sched[0].shape[0]
    grid_spec = pltpu.PrefetchScalarGridSpec(
        num_scalar_prefetch=5,
        grid=(n_items,),
        in_specs=[pl.BlockSpec((tm, d), lambda i, ie, ic, it, nr, fl: (it[i], 0)),
                  pl.BlockSpec((None, None, d, tf), lambda i, ie, ic, it, nr, fl: (layer, ie[i], 0, ic[i])),
                  pl.BlockSpec((None, None, d, tf),
                               lambda i, ie, ic, it, nr, fl: (layer, ie[i], 0, nc + ic[i])),
                  pl.BlockSpec((None, None, 1, tf), lambda i, ie, ic, it, nr, fl: (layer, ie[i], 0, ic[i])),
                  pl.BlockSpec((None, None, 1, tf),
                               lambda i, ie, ic, it, nr, fl: (layer, ie[i], 0, nc + ic[i]))],
        out_specs=pl.BlockSpec((tm, tf), lambda i, ie, ic, it, nr, fl: (it[i], ic[i])),
        scratch_shapes=[pltpu.VMEM((d, tf), BF16), pltpu.VMEM((d, tf), BF16)])
    b4 = b_gu.reshape(n_layers, n_experts, 1, f2)
    return pl.pallas_call(
        _moe_up_kernel,
        grid_spec=grid_spec,
        out_shape=jax.ShapeDtypeStruct((r_pad, f), BF16),
        compiler_params=_cparams(("arbitrary",)),
        name="moe_up",
    )(*sched, xs, w_gu, w_gu, b4, b4)


def _moe_down_kernel(ie_ref, ic_ref, it_ref, in_ref, if_ref, a_ref, w_ref, b_ref, o_ref, w_s):
    i = pl.program_id(0)

    @pl.when(if_ref[i] == 1)
    def _():
        w_s[...] = w_ref[...].astype(BF16)

    def compute(n):
        y = jnp.dot(a_ref[0:n, :], w_s[...], preferred_element_type=F32) + b_ref[...]
        o_ref[0:n, :] = y.astype(BF16)

    tm = a_ref.shape[0]
    pl.when(in_ref[i] == tm)(lambda: compute(tm))
    pl.when(in_ref[i] == tm // 2)(lambda: compute(tm // 2))


def _moe_down(sched, act, w_down, b_down, *, layer, tm, tn):
    r_pad, f = act.shape
    n_layers, n_experts, _, d = w_down.shape
    n_items = sched[0].shape[0]
    grid_spec = pltpu.PrefetchScalarGridSpec(
        num_scalar_prefetch=5,
        grid=(n_items,),
        in_specs=[pl.BlockSpec((tm, f), lambda i, ie, ic, it, nr, fl: (it[i], 0)),
                  pl.BlockSpec((None, None, f, tn), lambda i, ie, ic, it, nr, fl: (layer, ie[i], 0, ic[i])),
                  pl.BlockSpec((None, None, 1, tn), lambda i, ie, ic, it, nr, fl: (layer, ie[i], 0, ic[i]))],
        out_specs=pl.BlockSpec((tm, tn), lambda i, ie, ic, it, nr, fl: (it[i], ic[i])),
        scratch_shapes=[pltpu.VMEM((f, tn), BF16)])
    return pl.pallas_call(
        _moe_down_kernel,
        grid_spec=grid_spec,
        out_shape=jax.ShapeDtypeStruct((r_pad, d), BF16),
        compiler_params=_cparams(("arbitrary",)),
        name="moe_down",
    )(*sched, act, w_down, b_down.reshape(n_layers, n_experts, 1, d))


def _moe_schedule(idx, *, n_experts, tm, n_chunks):
    n_tok = idx.shape[0]
    n_pairs = n_tok * TOP_K
    r_pad = n_pairs + n_experts * tm
    n_tiles = r_pad // tm
    half = tm // 2
    flat_e = idx.reshape(-1)
    experts = jnp.arange(n_experts, dtype=jnp.int32)
    onehot = (flat_e[:, None] == experts[None, :]).astype(jnp.int32)
    csum = jnp.cumsum(onehot, axis=0)
    rank = jnp.sum(csum * onehot, axis=1) - 1
    counts = csum[-1]
    ntiles = (counts + tm - 1) // tm
    tile_end = jnp.cumsum(ntiles)
    tile_start = tile_end - ntiles
    dest = jnp.sum(onehot * tile_start[None, :], axis=1) * tm + rank
    src_tok = (jnp.arange(r_pad, dtype=jnp.int32) % n_tok).at[dest].set(
        jnp.arange(n_pairs, dtype=jnp.int32) // TOP_K, unique_indices=True, mode="promise_in_bounds")
    n_used = tile_end[-1]
    n_items = n_chunks * n_tiles
    item = jnp.arange(n_items, dtype=jnp.int32)
    valid = item < n_chunks * n_used
    ic_ = jnp.minimum(item, jnp.maximum(n_chunks * n_used - 1, 0))
    e = jnp.minimum(jnp.sum((ic_[:, None] >= n_chunks * tile_end[None, :]).astype(jnp.int32), axis=1),
                    n_experts - 1)
    sel = (e[:, None] == experts[None, :]).astype(jnp.int32)
    nt_e = jnp.maximum(jnp.sum(sel * ntiles[None, :], axis=1), 1)
    ts_e = jnp.sum(sel * tile_start[None, :], axis=1)
    cnt_e = jnp.sum(sel * counts[None, :], axis=1)
    r = ic_ - n_chunks * ts_e
    c = r // nt_e
    lt = r % nt_e
    t = ts_e + lt
    left = cnt_e - lt * tm
    nrows = jnp.where(valid, jnp.where(left <= half, half, tm), 0)
    first = valid & (lt == 0)
    sched = (e.astype(jnp.int32), c.astype(jnp.int32), t.astype(jnp.int32), nrows.astype(jnp.int32),
             first.astype(jnp.int32))
    return sched, src_tok, dest.reshape(n_tok, TOP_K).T.reshape(-1)


def _combine_kernel(x_ref, y0_ref, y1_ref, y2_ref, y3_ref, w_ref, g_ref, fg_ref, o_ref, *, final):
    w = w_ref[...]
    y = None
    for k, y_ref in enumerate((y0_ref, y1_ref, y2_ref, y3_ref)):
        term = w[:, k:k + 1] * y_ref[...].astype(F32)
        y = term if y is None else y + term
    x = x_ref[...] + g_ref[...] * y
    if final:
        x = x * lax.rsqrt(jnp.mean(x * x, axis=-1, keepdims=True) + NORM_EPS) * fg_ref[...]
    o_ref[...] = x


def _combine(x_new, ysg, wts, mods3, final_g, *, rows, tm, d, mod_idx, final):
    assert TOP_K == 4
    kern = functools.partial(_combine_kernel, final=final)
    nblk = rows // tm

    def y_spec(k):
        return pl.BlockSpec((tm, d), lambda i: (k * nblk + i, 0))

    return pl.pallas_call(
        kern,
        grid=(nblk,),
        in_specs=[pl.BlockSpec((tm, d), lambda i: (i, 0)),
                  y_spec(0), y_spec(1), y_spec(2), y_spec(3),
                  pl.BlockSpec((tm, TOP_K), lambda i: (i, 0)),
                  pl.BlockSpec((None, 1, d), lambda i: (mod_idx(i), 0, 5)),
                  pl.BlockSpec((1, d), lambda i: (0, 0))],
        out_specs=pl.BlockSpec((tm, d), lambda i: (i, 0)),
        out_shape=jax.ShapeDtypeStruct((rows, d), F32),
        compiler_params=_cparams(("arbitrary",)),
        name="moe_combine",
    )(x_new, ysg, ysg, ysg, ysg, wts, mods3, final_g.reshape(1, d))


def _rope_tables(batch, seq, n_ctx):
    rows = seq // GRID_W
    row = jnp.repeat(jnp.arange(rows, dtype=F32), GRID_W)
    col = jnp.tile(jnp.arange(GRID_W, dtype=F32), rows)
    quarter = HEAD_DIM // 4
    inv_freq = ROPE_THETA ** (-jnp.arange(quarter, dtype=F32) / quarter)
    ang = jnp.concatenate([row[:, None] * inv_freq, col[:, None] * inv_freq], axis=-1)
    cos, sin = jnp.cos(ang), jnp.sin(ang)
    cos_x = jnp.tile(jnp.concatenate([cos, cos], axis=-1), (batch, 1))
    sin_x = jnp.tile(jnp.concatenate([-sin, sin], axis=-1), (batch, 1))
    cos_f = jnp.concatenate([cos_x, jnp.ones((batch * n_ctx, HEAD_DIM), F32)], axis=0)
    sin_f = jnp.concatenate([sin_x, jnp.zeros((batch * n_ctx, HEAD_DIM), F32)], axis=0)
    return cos_f, sin_f


def _column_params(q_gain, k_gain, ncol):
    hp = _head_perm()
    scale = HEAD_DIM ** -0.5
    k_scale = RET_DK ** -0.5
    cs = jnp.ones((ncol,), F32)
    cs = cs.at[RK:RK + RET_HEADS * RET_DK].set(k_scale)
    cs = cs.at[GQ:GQ + GA_HEADS * HEAD_DIM].set(jnp.tile(q_gain.astype(F32)[hp] * (scale * LOG2_E), GA_HEADS))
    cs = cs.at[GK:GK + GA_KV_HEADS * HEAD_DIM].set(jnp.tile(k_gain.astype(F32)[hp], GA_KV_HEADS))
    cs = cs.at[WQ:WQ + WA_HEADS * HEAD_DIM].set(scale)
    rm = jnp.zeros((ncol,), F32).at[GQ:GQ + GA_HEADS * HEAD_DIM].set(1.0)
    rm = rm.at[GK:GK + GA_KV_HEADS * HEAD_DIM].set(1.0)
    return cs, rm


def kernel(x, c, ctx, c_ctx, ln1_g, ln2_g, w_ada, b_ada, w_in, ret_decay_logit, ga_q_gain,
           ga_k_gain, wa_sink, w_branch, w_out, w_router, b_router, w_gu, b_gu, w_down, b_down,
           final_g):
    batch, seq, d = x.shape
    n_ctx = ctx.shape[1]
    depth = w_ada.shape[0]
    n_experts = w_router.shape[2]
    ncol = w_in.shape[2]
    mx, mc = batch * seq, batch * n_ctx
    m = mx + mc
    tm = min(1024, mc)
    assert seq % tm == 0 and mc % tm == 0 and batch + 1 <= 8
    tm_stream = min(256, tm)
    assert w_gu.shape[3] // 2 == w_down.shape[3] == d

    def mod_idx_for(tile):
        def mod_idx(i):
            return jnp.where(i < mx // tile, i // (seq // tile), batch)
        return mod_idx

    mod_idx = mod_idx_for(tm)
    mod_idx_s = mod_idx_for(tm_stream)

    cc8 = jnp.zeros((8, d), F32).at[:batch].set(c).at[batch].set(c_ctx)
    mods = _modulation(cc8, w_ada, b_ada)
    cos_f, sin_f = _rope_tables(batch, seq, n_ctx)

    x_all = jnp.concatenate([x.reshape(mx, d), ctx.reshape(mc, d)], axis=0)
    e_tm = 512
    tf = min(1024, d)

    for l in range(depth):
        need_ctx = l < depth - 1
        rows = m if need_ctx else mx
        mods3 = mods[l].reshape(8, 1, 6 * d)
        w_in_p = _regroup_columns(w_in[l])
        colscale, rmsmask = _column_params(ga_q_gain[l], ga_k_gain[l], ncol)
        u = _in_proj(x_all, mods3, ln1_g[l], w_in_p, cos_f, sin_f, colscale, rmsmask,
                     tm=tm, mod_idx=mod_idx)
        o_ret = _retention(u, _retention_tables(ret_decay_logit[l]), batch=batch, seq=seq, n_ctx=n_ctx)
        o_ga = _global_attention(u, batch=batch, seq=seq, n_ctx=n_ctx, with_ctx=need_ctx)
        o_wa = _window_attention(u, wa_sink[l], batch=batch, seq=seq, n_ctx=n_ctx, with_ctx=need_ctx)
        mix = _merge(o_ret, o_ga, o_wa, u, w_branch[l].astype(BF16), rows=rows, tm=tm, d=d)
        x_new = _out_proj(mix, w_out[l].astype(BF16), x_all, mods3, rows=rows, tm=tm, d=d, mod_idx=mod_idx)
        h2, idx, wts = _router(x_new, mods3, ln2_g[l], w_router[l], b_router[l],
                               rows=rows, tm=tm_stream, d=d, mod_idx=mod_idx_s)
        sched, src_tok, pos = _moe_schedule(
            idx, n_experts=n_experts, tm=e_tm, n_chunks=w_down.shape[2] // tf)
        xs = h2.at[src_tok].get(mode="promise_in_bounds")
        act = _moe_up(sched, xs, w_gu, b_gu, layer=l, tm=e_tm, tf=tf)
        ys = _moe_down(sched, act, w_down, b_down, layer=l, tm=e_tm, tn=tf)
        ysg = ys.at[pos].get(mode="promise_in_bounds")
        x_all = _combine(x_new, ysg, wts, mods3, final_g, rows=rows, tm=tm_stream, d=d,
                         mod_idx=mod_idx_s, final=not need_ctx)
    return x_all[:mx].reshape(batch, seq, d)
```

```python
import functools

import jax
import jax.numpy as jnp
from jax import lax
from jax.experimental import pallas as pl
from jax.experimental.pallas import tpu as pltpu

GRID_W = 64
HEAD_DIM = 128
RET_HEADS = 4
RET_DK = 128
RET_DV = 256
GA_HEADS = 8
GA_KV_HEADS = 2
WA_HEADS = 8
WA_KV_HEADS = 2
WINDOW = 128
N_BRANCHES = 3
TOP_K = 4
SWIGLU_LIMIT = 7.0
SWIGLU_ALPHA = 1.702
ROPE_THETA = 10000.0
NORM_EPS = 1e-6
LOG2_E = 1.4426950408889634
MASK_BIAS = -1e30

LANES = 128
CHUNK = 256
GROUP = GA_HEADS // GA_KV_HEADS
MOE_TILE_PARTS = 4
GA_TQ = 256
VMEM_LIMIT = 56 * 1024 * 1024

RQ, RK, GQ, WQ, GK, WK = 0, 512, 1024, 2048, 3072, 3328
ROPE_COLS = 3584
GV, WV, RV, RG, GATE = 3584, 3840, 4096, 5120, 6144
_ORQ, _ORK, _ORV, _ORG, _OGQ, _OGK, _OGV, _OWQ, _OWK, _OWV, _OGATE = (
    0, 512, 1024, 2048, 3072, 4096, 4352, 4608, 5632, 5888, 6144)

F32 = jnp.float32
BF16 = jnp.bfloat16


def _cparams(sem, vmem=VMEM_LIMIT):
    return pltpu.CompilerParams(dimension_semantics=sem, vmem_limit_bytes=vmem)


def _head_perm():
    j = jnp.arange(HEAD_DIM)
    return jnp.where(j < HEAD_DIM // 2, 2 * j, 2 * (j - HEAD_DIM // 2) + 1)


def _regroup_columns(w):
    d_model = w.shape[0]
    half = HEAD_DIM // 2

    def rope_heads(start, n_heads):
        blk = w[:, start:start + n_heads * HEAD_DIM].reshape(d_model, n_heads, half, 2)
        return blk.transpose(0, 1, 3, 2).reshape(d_model, n_heads * HEAD_DIM)

    def plain(start, size):
        return w[:, start:start + size]

    return jnp.concatenate([
        rope_heads(_ORQ, RET_HEADS), rope_heads(_ORK, RET_HEADS),
        rope_heads(_OGQ, GA_HEADS), rope_heads(_OWQ, WA_HEADS),
        rope_heads(_OGK, GA_KV_HEADS), rope_heads(_OWK, WA_KV_HEADS),
        plain(_OGV, GA_KV_HEADS * HEAD_DIM), plain(_OWV, WA_KV_HEADS * HEAD_DIM),
        plain(_ORV, RET_HEADS * RET_DV), plain(_ORG, RET_HEADS * RET_DV),
        plain(_OGATE, w.shape[1] - _OGATE)], axis=1).astype(BF16)


def _mod_kernel(c_ref, w_ref, b_ref, o_ref):
    c = c_ref[...]
    s = (c * jax.nn.sigmoid(c)).astype(BF16)
    o_ref[...] = jnp.dot(s, w_ref[...].astype(BF16), preferred_element_type=F32) + b_ref[...]


def _modulation(cc8, w_ada, b_ada):
    n_layers, d, n = w_ada.shape
    tn = min(1024, n)
    return pl.pallas_call(
        _mod_kernel,
        grid=(n_layers, n // tn),
        in_specs=[pl.BlockSpec((8, d), lambda l, j: (0, 0)),
                  pl.BlockSpec((None, d, tn), lambda l, j: (l, 0, j)),
                  pl.BlockSpec((None, 1, tn), lambda l, j: (l, 0, j))],
        out_specs=pl.BlockSpec((None, 8, tn), lambda l, j: (l, 0, j)),
        out_shape=jax.ShapeDtypeStruct((n_layers, 8, n), F32),
        compiler_params=_cparams(("arbitrary", "arbitrary")),
        name="modulation",
    )(cc8, w_ada, b_ada.reshape(n_layers, 1, n))


def _in_proj_kernel(x_ref, sh_ref, sc_ref, g_ref, w_ref, cos_ref, sin_ref, cs_ref, rm_ref,
                    o_ref, h_ref, acc_ref, *, n_rope_tiles, n_tiles, n_steps, tn):
    s = pl.program_id(0)
    j = s % n_tiles

    @pl.when(s == 0)
    def _():
        acc_ref[1] = jnp.zeros(acc_ref.shape[1:], F32)

    @pl.when((j == 0) & (s < n_steps - 1))
    def _():
        x = x_ref[...]
        y = x * lax.rsqrt(jnp.mean(x * x, axis=-1, keepdims=True) + NORM_EPS) * g_ref[...]
        h_ref[...] = (y * (1.0 + sc_ref[...]) + sh_ref[...]).astype(BF16)

    cur = s % 2
    prev = acc_ref[1 - cur]
    is_rope = jnp.maximum(s - 1, 0) % n_tiles < n_rope_tiles
    cos = cos_ref[...]
    sin = sin_ref[...]
    for hh in range(tn // HEAD_DIM):
        sl = slice(hh * HEAD_DIM, (hh + 1) * HEAD_DIM)
        raw = prev[:, sl]
        r = lax.rsqrt(jnp.mean(raw * raw, axis=-1, keepdims=True) + NORM_EPS)
        rm = rm_ref[:, sl]
        t = raw * (rm * r + (1.0 - rm)) * cs_ref[:, sl]
        roped = t * cos + pltpu.roll(t, HEAD_DIM // 2, 1) * sin
        o_ref[:, sl] = jnp.where(is_rope, roped, raw).astype(BF16)
    acc_ref[cur] = jnp.dot(h_ref[...], w_ref[...], preferred_element_type=F32)


def _in_proj(x_all, mods3, ln_g, w_in_p, cos_f, sin_f, colscale, rmsmask, *, tm, mod_idx):
    m, d = x_all.shape
    ncol = w_in_p.shape[1]
    tn = 512
    n_tiles = ncol // tn
    n_rows = m // tm
    n_steps = n_rows * n_tiles + 1
    n_rope_tiles = ROPE_COLS // tn

    def row(s):
        return jnp.minimum(s // n_tiles, n_rows - 1)

    def col(s):
        return jnp.where(s < n_steps - 1, s % n_tiles, n_tiles - 1)

    def prow(s):
        return jnp.maximum(s - 1, 0) // n_tiles

    def pcol(s):
        return jnp.maximum(s - 1, 0) % n_tiles

    kern = functools.partial(_in_proj_kernel, n_rope_tiles=n_rope_tiles, n_tiles=n_tiles,
                             n_steps=n_steps, tn=tn)
    return pl.pallas_call(
        kern,
        grid=(n_steps,),
        in_specs=[pl.BlockSpec((tm, d), lambda s: (row(s), 0)),
                  pl.BlockSpec((None, 1, d), lambda s: (mod_idx(row(s)), 0, 0)),
                  pl.BlockSpec((None, 1, d), lambda s: (mod_idx(row(s)), 0, 1)),
                  pl.BlockSpec((1, d), lambda s: (0, 0)),
                  pl.BlockSpec((d, tn), lambda s: (0, col(s))),
                  pl.BlockSpec((tm, HEAD_DIM), lambda s: (prow(s), 0)),
                  pl.BlockSpec((tm, HEAD_DIM), lambda s: (prow(s), 0)),
                  pl.BlockSpec((1, tn), lambda s: (0, pcol(s))),
                  pl.BlockSpec((1, tn), lambda s: (0, pcol(s)))],
        out_specs=pl.BlockSpec((tm, tn), lambda s: (prow(s), pcol(s))),
        out_shape=jax.ShapeDtypeStruct((m, ncol), BF16),
        scratch_shapes=[pltpu.VMEM((tm, d), BF16), pltpu.VMEM((2, tm, tn), F32)],
        compiler_params=_cparams(("arbitrary",)),
        name="in_proj",
    )(x_all, mods3, mods3, ln_g.reshape(1, d), w_in_p, cos_f, sin_f,
      colscale.reshape(1, ncol), rmsmask.reshape(1, ncol))


def _retention_kernel(q_ref, k_ref, v_ref, g_ref, dm_ref, qd_ref, kd_ref, cd_ref,
                      o_ref, s_ref, ob_ref, *, n_lat_chunks):
    d = pl.program_id(1)
    t = pl.program_id(2)

    @pl.when(t == 0)
    def _():
        s_ref[...] = jnp.zeros_like(s_ref)

    cid = jnp.where(d == 1, t, jnp.where(t == 0, 0, 1 + n_lat_chunks - t))
    for h in range(RET_HEADS):
        q = q_ref[:, h * RET_DK:(h + 1) * RET_DK]
        k = k_ref[:, h * RET_DK:(h + 1) * RET_DK]
        v = v_ref[:, h * RET_DV:(h + 1) * RET_DV]
        state = s_ref[h]
        scores = lax.dot_general(q, k, (((1,), (1,)), ((), ())), preferred_element_type=F32)
        p = (scores * dm_ref[h]).astype(BF16)
        qd = (q.astype(F32) * qd_ref[h]).astype(BF16)
        o = (jnp.dot(p, v, preferred_element_type=F32)
             + jnp.dot(qd, state.astype(BF16), preferred_element_type=F32))
        kd = (k.astype(F32) * kd_ref[h]).astype(BF16)
        s_ref[h] = state * cd_ref[h] + lax.dot_general(
            kd, v, (((0,), (0,)), ((), ())), preferred_element_type=F32)
        sl = slice(h * RET_DV, (h + 1) * RET_DV)

        @pl.when(d == 0)
        def _():
            ob_ref[cid, :, sl] = o

        @pl.when(d == 1)
        def _():
            tot = o + ob_ref[cid, :, sl]
            tot = tot * lax.rsqrt(jnp.mean(tot * tot, axis=-1, keepdims=True) + NORM_EPS)
            g = g_ref[:, sl].astype(F32)
            o_ref[:, sl] = (tot * (g * jax.nn.sigmoid(g))).astype(BF16)


def _retention(u, tabs, *, batch, seq, n_ctx):
    m = u.shape[0]
    assert n_ctx == CHUNK and seq % CHUNK == 0
    nc = seq // CHUNK
    lat_blocks = batch * nc
    dmat, qdec, kdec, cdec = tabs

    def rowblk(b, d, t):
        lat = b * nc + jnp.where(d == 1, t - 1, nc - t)
        return jnp.where(t == 0, lat_blocks + b, lat)

    def in_map(col):
        return lambda b, d, t: (rowblk(b, d, t), col)

    def out_map(b, d, t):
        return (jnp.where(d == 1, rowblk(b, 1, t), lat_blocks + b), 0)

    def tab_map(b, d, t):
        return (d, 0, 0, 0)

    qk_w = RET_HEADS * RET_DK
    v_w = RET_HEADS * RET_DV
    kern = functools.partial(_retention_kernel, n_lat_chunks=nc)
    return pl.pallas_call(
        kern,
        grid=(batch, 2, nc + 1),
        in_specs=[pl.BlockSpec((CHUNK, qk_w), in_map(RQ // qk_w)),
                  pl.BlockSpec((CHUNK, qk_w), in_map(RK // qk_w)),
                  pl.BlockSpec((CHUNK, v_w), in_map(RV // v_w)),
                  pl.BlockSpec((CHUNK, v_w), in_map(RG // v_w)),
                  pl.BlockSpec((None, RET_HEADS, CHUNK, CHUNK), tab_map),
                  pl.BlockSpec((None, RET_HEADS, CHUNK, RET_DK), tab_map),
                  pl.BlockSpec((None, RET_HEADS, CHUNK, RET_DK), tab_map),
                  pl.BlockSpec((None, RET_HEADS, 1, RET_DV), tab_map)],
        out_specs=pl.BlockSpec((CHUNK, v_w), out_map),
        out_shape=jax.ShapeDtypeStruct((m, v_w), BF16),
        scratch_shapes=[pltpu.VMEM((RET_HEADS, RET_DK, RET_DV), F32),
                        pltpu.VMEM((nc + 1, CHUNK, v_w), F32)],
        compiler_params=_cparams(("arbitrary", "arbitrary", "arbitrary")),
        name="retention",
    )(u, u, u, u, dmat, qdec, kdec, cdec)


def _retention_tables(decay_logit):
    lg = jax.nn.log_sigmoid(decay_logit.astype(F32))
    lf, lb = lg[0][:, None, None], lg[1][:, None, None]
    pos = jnp.arange(CHUNK, dtype=F32)
    diff = pos[:, None] - pos[None, :]
    d_f = jnp.where(diff >= 0, jnp.exp(jnp.where(diff >= 0, diff, 0.0) * lf), 0.0)
    d_b = jnp.where(diff < 0, jnp.exp(jnp.where(diff < 0, -diff, 0.0) * lb), 0.0)
    ones_k = jnp.ones((1, 1, RET_DK), F32)
    q_f = jnp.exp((pos + 1.0)[None, :, None] * lf) * ones_k
    q_b = jnp.exp((CHUNK - pos)[None, :, None] * lb) * ones_k
    k_f = jnp.exp((CHUNK - 1.0 - pos)[None, :, None] * lf) * ones_k
    k_b = jnp.exp(pos[None, :, None] * lb) * ones_k
    ones_v = jnp.ones((1, 1, RET_DV), F32)
    c_f = jnp.exp(CHUNK * lf) * ones_v
    c_b = jnp.exp(CHUNK * lb) * ones_v
    return (jnp.stack([d_b, d_f]), jnp.stack([q_b, q_f]), jnp.stack([k_b, k_f]),
            jnp.stack([c_b, c_f]))


def _global_attn_kernel(q_ref, kx_ref, vx_ref, kc_ref, vc_ref, o_ref, vxe_ref, vce_ref, *, n_lat_tiles):
    i = pl.program_id(2)
    nt = (((1,), (1,)), ((), ()))

    @pl.when(i == 0)
    def _():
        vxe_ref[:, 0:HEAD_DIM] = vx_ref[...]
        vxe_ref[:, HEAD_DIM:] = jnp.ones_like(vx_ref)
        vce_ref[:, 0:HEAD_DIM] = vc_ref[...]
        vce_ref[:, HEAD_DIM:] = jnp.ones_like(vc_ref)

    def scores(h):
        q = q_ref[:, h * HEAD_DIM:(h + 1) * HEAD_DIM]
        return (lax.dot_general(q, kc_ref[...], nt, preferred_element_type=F32),
                lax.dot_general(q, kx_ref[...], nt, preferred_element_type=F32))

    def store(h, pv):
        o_ref[:, h * HEAD_DIM:(h + 1) * HEAD_DIM] = (pv[:, :HEAD_DIM] / pv[:, HEAD_DIM:]).astype(BF16)

    @pl.when(i < n_lat_tiles)
    def _():
        nxt = scores(0)
        for h in range(GROUP):
            s_c, s_x = nxt
            if h + 1 < GROUP:
                nxt = scores(h + 1)
            mx = jnp.maximum(jnp.max(s_c, axis=-1, keepdims=True), jnp.max(s_x, axis=-1, keepdims=True))
            p_c = jnp.exp2((s_c - mx).astype(BF16))
            p_x = jnp.exp2((s_x - mx).astype(BF16))
            store(h, jnp.dot(p_c, vce_ref[...], preferred_element_type=F32)
                  + jnp.dot(p_x, vxe_ref[...], preferred_element_type=F32))

    @pl.when(i >= n_lat_tiles)
    def _():
        for h in range(GROUP):
            s_c = lax.dot_general(q_ref[:, h * HEAD_DIM:(h + 1) * HEAD_DIM], kc_ref[...], nt,
                                  preferred_element_type=F32)
            p_c = jnp.exp2((s_c - jnp.max(s_c, axis=-1, keepdims=True)).astype(BF16))
            store(h, jnp.dot(p_c, vce_ref[...], preferred_element_type=F32))


def _global_attention(u, *, batch, seq, n_ctx, with_ctx):
    m = u.shape[0]
    tq = GA_TQ
    nq = seq // tq
    n_ctx_tiles = n_ctx // tq
    lat_blocks = batch * nq
    gw = GROUP * HEAD_DIM

    def qrow(b, i):
        return jnp.where(i < nq, b * nq + i, lat_blocks + b * n_ctx_tiles + (i - nq))

    kern = functools.partial(_global_attn_kernel, n_lat_tiles=nq)
    return pl.pallas_call(
        kern,
        grid=(batch, GA_KV_HEADS, nq + (n_ctx_tiles if with_ctx else 0)),
        in_specs=[pl.BlockSpec((tq, gw), lambda b, kh, i: (qrow(b, i), GQ // gw + kh)),
                  pl.BlockSpec((seq, HEAD_DIM), lambda b, kh, i: (b, GK // HEAD_DIM + kh)),
                  pl.BlockSpec((seq, HEAD_DIM), lambda b, kh, i: (b, GV // HEAD_DIM + kh)),
                  pl.BlockSpec((n_ctx, HEAD_DIM),
                               lambda b, kh, i: (batch * seq // n_ctx + b, GK // HEAD_DIM + kh)),
                  pl.BlockSpec((n_ctx, HEAD_DIM),
                               lambda b, kh, i: (batch * seq // n_ctx + b, GV // HEAD_DIM + kh))],
        out_specs=pl.BlockSpec((tq, gw), lambda b, kh, i: (qrow(b, i), kh)),
        out_shape=jax.ShapeDtypeStruct((m, GA_HEADS * HEAD_DIM), BF16),
        scratch_shapes=[pltpu.VMEM((seq, 2 * HEAD_DIM), BF16), pltpu.VMEM((n_ctx, 2 * HEAD_DIM), BF16)],
        compiler_params=_cparams(("arbitrary", "arbitrary", "arbitrary")),
        name="global_attention",
    )(u, u, u, u, u)


def _window_attn_kernel(sink_ref, q_ref, kx_ref, vx_ref, kc_ref, vc_ref, bias_ref, o_ref, vce_ref,
                        *, n_lat_tiles, seq):
    kh = pl.program_id(1)
    i = pl.program_id(2)
    span = CHUNK + 2 * WINDOW
    nt = (((1,), (1,)), ((), ()))

    @pl.when(i == 0)
    def _():
        vce_ref[:, 0:HEAD_DIM] = vc_ref[...]
        vce_ref[:, HEAD_DIM:] = jnp.ones_like(vc_ref)

    start = jnp.clip(i * CHUNK - WINDOW, 0, seq - span)
    start = pl.multiple_of(jnp.where(i < n_lat_tiles, start, 0), WINDOW)
    kw = kx_ref[pl.ds(start, span), :]
    vw = vx_ref[pl.ds(start, span), :]
    vwe = jnp.concatenate([vw, jnp.ones_like(vw)], axis=1)
    for h in range(GROUP):
        q = q_ref[:, h * HEAD_DIM:(h + 1) * HEAD_DIM]
        s_w = lax.dot_general(q, kw, nt, preferred_element_type=F32) + bias_ref[...]
        s_c = lax.dot_general(q, kc_ref[...], nt, preferred_element_type=F32)
        sink = sink_ref[kh * GROUP + h]
        mx = jnp.maximum(jnp.maximum(jnp.max(s_w, axis=-1, keepdims=True),
                                     jnp.max(s_c, axis=-1, keepdims=True)), sink)
        p_w = jnp.exp2((s_w - mx).astype(BF16))
        p_c = jnp.exp2((s_c - mx).astype(BF16))
        pv = (jnp.dot(p_w, vwe, preferred_element_type=F32)
              + jnp.dot(p_c, vce_ref[...], preferred_element_type=F32))
        den = pv[:, HEAD_DIM:] + jnp.exp2(sink - mx)
        o_ref[:, h * HEAD_DIM:(h + 1) * HEAD_DIM] = (pv[:, :HEAD_DIM] / den).astype(BF16)


def _window_bias():
    span = CHUNK + 2 * WINDOW
    r = jnp.arange(CHUNK)[:, None]
    col = jnp.arange(span)[None, :]
    offs = (0, WINDOW, 2 * WINDOW)
    tiles = [jnp.where(jnp.abs(col - off - r) <= WINDOW, 0.0, MASK_BIAS) for off in offs]
    tiles.append(jnp.full((CHUNK, span), MASK_BIAS))
    return jnp.stack(tiles).astype(F32)


def _window_attention(u, sink, *, batch, seq, n_ctx, with_ctx):
    m = u.shape[0]
    nq = seq // CHUNK
    lat_blocks = batch * nq
    gw = GROUP * HEAD_DIM
    span = CHUNK + 2 * WINDOW
    assert seq >= span and nq >= 2

    def qrow(b, i):
        return jnp.where(i < nq, b * nq + i, lat_blocks + b)

    def variant(i):
        return jnp.where(i == 0, 0, jnp.where(i < nq - 1, 1, jnp.where(i == nq - 1, 2, 3)))

    kern = functools.partial(_window_attn_kernel, n_lat_tiles=nq, seq=seq)
    return pl.pallas_call(
        kern,
        grid=(batch, WA_KV_HEADS, nq + (1 if with_ctx else 0)),
        in_specs=[pl.BlockSpec(memory_space=pltpu.SMEM),
                  pl.BlockSpec((CHUNK, gw), lambda b, kh, i: (qrow(b, i), WQ // gw + kh)),
                  pl.BlockSpec((seq, HEAD_DIM), lambda b, kh, i: (b, WK // HEAD_DIM + kh)),
                  pl.BlockSpec((seq, HEAD_DIM), lambda b, kh, i: (b, WV // HEAD_DIM + kh)),
                  pl.BlockSpec((n_ctx, HEAD_DIM),
                               lambda b, kh, i: (batch * seq // n_ctx + b, WK // HEAD_DIM + kh)),
                  pl.BlockSpec((n_ctx, HEAD_DIM),
                               lambda b, kh, i: (batch * seq // n_ctx + b, WV // HEAD_DIM + kh)),
                  pl.BlockSpec((None, CHUNK, span), lambda b, kh, i: (variant(i), 0, 0))],
        out_specs=pl.BlockSpec((CHUNK, gw), lambda b, kh, i: (qrow(b, i), kh)),
        out_shape=jax.ShapeDtypeStruct((m, WA_HEADS * HEAD_DIM), BF16),
        scratch_shapes=[pltpu.VMEM((n_ctx, 2 * HEAD_DIM), BF16)],
        compiler_params=_cparams(("arbitrary", "arbitrary", "arbitrary")),
        name="window_attention",
    )(sink.astype(F32) * LOG2_E, u, u, u, u, u, _window_bias())


def _merge_kernel(o0_ref, o1_ref, o2_ref, g0_ref, g1_ref, g2_ref, w0_ref, w1_ref, w2_ref, out_ref):
    tot = None
    for o_ref, g_ref, w_ref in ((o0_ref, g0_ref, w0_ref), (o1_ref, g1_ref, w1_ref),
                                (o2_ref, g2_ref, w2_ref)):
        term = jax.nn.sigmoid(g_ref[...].astype(F32)) * jnp.dot(
            o_ref[...], w_ref[...], preferred_element_type=F32)
        tot = term if tot is None else tot + term
    out_ref[...] = tot.astype(BF16)


def _merge(o_ret, o_ga, o_wa, u, w_branch, *, rows, tm, d):
    tn = 512
    bw = w_branch.shape[1]
    gate_blk = GATE // tn
    nd = d // tn

    def o_spec():
        return pl.BlockSpec((tm, bw), lambda i, j: (i, 0))

    def g_spec(br):
        return pl.BlockSpec((tm, tn), lambda i, j: (i, gate_blk + br * nd + j))

    def w_spec(br):
        return pl.BlockSpec((None, bw, tn), lambda i, j: (br, 0, j))

    return pl.pallas_call(
        _merge_kernel,
        grid=(rows // tm, nd),
        in_specs=[o_spec(), o_spec(), o_spec(), g_spec(0), g_spec(1), g_spec(2),
                  w_spec(0), w_spec(1), w_spec(2)],
        out_specs=pl.BlockSpec((tm, tn), lambda i, j: (i, j)),
        out_shape=jax.ShapeDtypeStruct((rows, d), BF16),
        compiler_params=_cparams(("arbitrary", "arbitrary")),
        name="branch_merge",
    )(o_ret, o_ga, o_wa, u, u, u, w_branch, w_branch, w_branch)


def _out_proj_kernel(m_ref, w_ref, x_ref, g_ref, o_ref):
    o_ref[...] = x_ref[...] + g_ref[...] * jnp.dot(m_ref[...], w_ref[...], preferred_element_type=F32)


def _out_proj(mix, w_out, x_all, mods3, *, rows, tm, d, mod_idx):
    tn = 512
    nd = d // tn
    return pl.pallas_call(
        _out_proj_kernel,
        grid=(rows // tm, nd),
        in_specs=[pl.BlockSpec((tm, d), lambda i, j: (i, 0)),
                  pl.BlockSpec((d, tn), lambda i, j: (0, j)),
                  pl.BlockSpec((tm, tn), lambda i, j: (i, j)),
                  pl.BlockSpec((None, 1, tn), lambda i, j: (mod_idx(i), 0, 2 * nd + j))],
        out_specs=pl.BlockSpec((tm, tn), lambda i, j: (i, j)),
        out_shape=jax.ShapeDtypeStruct((rows, d), F32),
        compiler_params=_cparams(("arbitrary", "arbitrary")),
        name="out_proj",
    )(mix, w_out, x_all, mods3)


def _router_kernel(x_ref, sh_ref, sc_ref, g_ref, wr_ref, br_ref, h_ref, idx_ref, wt_ref, *, n_experts):
    x = x_ref[...]
    y = x * lax.rsqrt(jnp.mean(x * x, axis=-1, keepdims=True) + NORM_EPS) * g_ref[...]
    h = y * (1.0 + sc_ref[...]) + sh_ref[...]
    h_ref[...] = h.astype(BF16)
    logits = jnp.dot(h, wr_ref[...], preferred_element_type=F32,
                     precision=lax.Precision.HIGHEST) + br_ref[...]
    lane = lax.broadcasted_iota(jnp.int32, logits.shape, 1).astype(F32)
    vals, ids = [], []
    cur = logits
    for _ in range(TOP_K):
        mx = jnp.max(cur, axis=-1, keepdims=True)
        sel = jnp.min(jnp.where(cur == mx, lane, float(n_experts)), axis=-1, keepdims=True)
        vals.append(mx)
        ids.append(sel)
        cur = jnp.where(lane == sel, -jnp.inf, cur)
    e = [jnp.exp(v - vals[0]) for v in vals]
    den = e[0] + e[1] + e[2] + e[3]
    for k in range(TOP_K):
        idx_ref[:, k:k + 1] = ids[k].astype(jnp.int32)
        wt_ref[:, k:k + 1] = e[k] / den


def _router(x_new, mods3, ln_g, w_router, b_router, *, rows, tm, d, mod_idx):
    n_experts = w_router.shape[1]
    kern = functools.partial(_router_kernel, n_experts=n_experts)
    return pl.pallas_call(
        kern,
        grid=(rows // tm,),
        in_specs=[pl.BlockSpec((tm, d), lambda i: (i, 0)),
                  pl.BlockSpec((None, 1, d), lambda i: (mod_idx(i), 0, 3)),
                  pl.BlockSpec((None, 1, d), lambda i: (mod_idx(i), 0, 4)),
                  pl.BlockSpec((1, d), lambda i: (0, 0)),
                  pl.BlockSpec((d, n_experts), lambda i: (0, 0)),
                  pl.BlockSpec((1, n_experts), lambda i: (0, 0))],
        out_specs=[pl.BlockSpec((tm, d), lambda i: (i, 0)),
                   pl.BlockSpec((tm, TOP_K), lambda i: (i, 0)),
                   pl.BlockSpec((tm, TOP_K), lambda i: (i, 0))],
        out_shape=[jax.ShapeDtypeStruct((rows, d), BF16),
                   jax.ShapeDtypeStruct((rows, TOP_K), jnp.int32),
                   jax.ShapeDtypeStruct((rows, TOP_K), F32)],
        compiler_params=_cparams(("arbitrary",)),
        name="norm2_router",
    )(x_new, mods3, mods3, ln_g.reshape(1, d), w_router, b_router.reshape(1, n_experts))


def _moe_up_kernel(ie_ref, ic_ref, it_ref, in_ref, if_ref, x_ref, wg_ref, wl_ref, bg_ref, bl_ref,
                   o_ref, wg_s, wl_s):
    i = pl.program_id(0)

    @pl.when(if_ref[i] == 1)
    def _():
        wg_s[...] = wg_ref[...].astype(BF16)
        wl_s[...] = wl_ref[...].astype(BF16)

    def compute(n):
        x = x_ref[0:n, :]
        glu = jnp.dot(x, wg_s[...], preferred_element_type=F32) + bg_ref[...]
        lin = jnp.dot(x, wl_s[...], preferred_element_type=F32) + bl_ref[...]
        glu = jnp.minimum(glu, SWIGLU_LIMIT)
        lin = jnp.clip(lin, -SWIGLU_LIMIT, SWIGLU_LIMIT)
        o_ref[0:n, :] = (glu * jax.nn.sigmoid(SWIGLU_ALPHA * glu) * (lin + 1.0)).astype(BF16)

    tm = x_ref.shape[0]
    for quarters in range(1, MOE_TILE_PARTS + 1):
        n = quarters * tm // MOE_TILE_PARTS
        pl.when(in_ref[i] == n)(functools.partial(compute, n))


def _moe_up(sched, xs, w_gu, b_gu, *, layer, tm, tf):
    r_pad, d = xs.shape
    n_layers, n_experts, _, f2 = w_gu.shape
    f = f2 // 2
    nc = f // tf
    n_items = sched[0].shape[0]
    grid_spec = pltpu.PrefetchScalarGridSpec(
        num_scalar_prefetch=5,
        grid=(n_items,),
        in_specs=[pl.BlockSpec((tm, d), lambda i, ie, ic, it, nr, fl: (it[i], 0)),
                  pl.BlockSpec((None, None, d, tf), lambda i, ie, ic, it, nr, fl: (layer, ie[i], 0, ic[i])),
                  pl.BlockSpec((None, None, d, tf),
                               lambda i, ie, ic, it, nr, fl: (layer, ie[i], 0, nc + ic[i])),
                  pl.BlockSpec((None, None, 1, tf), lambda i, ie, ic, it, nr, fl: (layer, ie[i], 0, ic[i])),
                  pl.BlockSpec((None, None, 1, tf),
                               lambda i, ie, ic, it, nr, fl: (layer, ie[i], 0, nc + ic[i]))],
        out_specs=pl.BlockSpec((tm, tf), lambda i, ie, ic, it, nr, fl: (it[i], ic[i])),
        scratch_shapes=[pltpu.VMEM((d, tf), BF16), pltpu.VMEM((d, tf), BF16)])
    b4 = b_gu.reshape(n_layers, n_experts, 1, f2)
    return pl.pallas_call(
        _moe_up_kernel,
        grid_spec=grid_spec,
        out_shape=jax.ShapeDtypeStruct((r_pad, f), BF16),
        compiler_params=_cparams(("arbitrary",)),
        name="moe_up",
    )(*sched, xs, w_gu, w_gu, b4, b4)


def _moe_down_kernel(ie_ref, ic_ref, it_ref, in_ref, if_ref, a_ref, w_ref, b_ref, o_ref, w_s):
    i = pl.program_id(0)

    @pl.when(if_ref[i] == 1)
    def _():
        w_s[...] = w_ref[...].astype(BF16)

    def compute(n):
        y = jnp.dot(a_ref[0:n, :], w_s[...], preferred_element_type=F32) + b_ref[...]
        o_ref[0:n, :] = y.astype(BF16)

    tm = a_ref.shape[0]
    for quarters in range(1, MOE_TILE_PARTS + 1):
        n = quarters * tm // MOE_TILE_PARTS
        pl.when(in_ref[i] == n)(functools.partial(compute, n))


def _moe_down(sched, act, w_down, b_down, *, layer, tm, tn):
    r_pad, f = act.shape
    n_layers, n_experts, _, d = w_down.shape
    n_items = sched[0].shape[0]
    grid_spec = pltpu.PrefetchScalarGridSpec(
        num_scalar_prefetch=5,
        grid=(n_items,),
        in_specs=[pl.BlockSpec((tm, f), lambda i, ie, ic, it, nr, fl: (it[i], 0)),
                  pl.BlockSpec((None, None, f, tn), lambda i, ie, ic, it, nr, fl: (layer, ie[i], 0, ic[i])),
                  pl.BlockSpec((None, None, 1, tn), lambda i, ie, ic, it, nr, fl: (layer, ie[i], 0, ic[i]))],
        out_specs=pl.BlockSpec((tm, tn), lambda i, ie, ic, it, nr, fl: (it[i], ic[i])),
        scratch_shapes=[pltpu.VMEM((f, tn), BF16)])
    return pl.pallas_call(
        _moe_down_kernel,
        grid_spec=grid_spec,
        out_shape=jax.ShapeDtypeStruct((r_pad, d), BF16),
        compiler_params=_cparams(("arbitrary",)),
        name="moe_down",
    )(*sched, act, w_down, b_down.reshape(n_layers, n_experts, 1, d))


def _moe_schedule(idx, *, n_experts, tm, n_chunks):
    n_tok = idx.shape[0]
    n_pairs = n_tok * TOP_K
    r_pad = n_pairs + n_experts * tm
    n_tiles = r_pad // tm
    part = tm // MOE_TILE_PARTS
    flat_e = idx.reshape(-1)
    experts = jnp.arange(n_experts, dtype=jnp.int32)
    onehot = (flat_e[:, None] == experts[None, :]).astype(jnp.int32)
    csum = jnp.cumsum(onehot, axis=0)
    rank = jnp.sum(csum * onehot, axis=1) - 1
    counts = csum[-1]
    ntiles = (counts + tm - 1) // tm
    tile_end = jnp.cumsum(ntiles)
    tile_start = tile_end - ntiles
    dest = jnp.sum(onehot * tile_start[None, :], axis=1) * tm + rank
    src_tok = (jnp.arange(r_pad, dtype=jnp.int32) % n_tok).at[dest].set(
        jnp.arange(n_pairs, dtype=jnp.int32) // TOP_K, unique_indices=True, mode="promise_in_bounds")
    n_used = tile_end[-1]
    n_items = n_chunks * n_tiles
    item = jnp.arange(n_items, dtype=jnp.int32)
    valid = item < n_chunks * n_used
    ic_ = jnp.minimum(item, jnp.maximum(n_chunks * n_used - 1, 0))
    e = jnp.minimum(jnp.sum((ic_[:, None] >= n_chunks * tile_end[None, :]).astype(jnp.int32), axis=1),
                    n_experts - 1)
    sel = (e[:, None] == experts[None, :]).astype(jnp.int32)
    nt_e = jnp.maximum(jnp.sum(sel * ntiles[None, :], axis=1), 1)
    ts_e = jnp.sum(sel * tile_start[None, :], axis=1)
    cnt_e = jnp.sum(sel * counts[None, :], axis=1)
    r = ic_ - n_chunks * ts_e
    c = r // nt_e
    lt = r % nt_e
    t = ts_e + lt
    left = cnt_e - lt * tm
    nrows = jnp.where(valid, jnp.minimum((left + part - 1) // part * part, tm), 0)
    first = valid & (lt == 0)
    sched = (e.astype(jnp.int32), c.astype(jnp.int32), t.astype(jnp.int32), nrows.astype(jnp.int32),
             first.astype(jnp.int32))
    return sched, src_tok, dest.reshape(n_tok, TOP_K).T.reshape(-1)


def _combine_kernel(x_ref, y0_ref, y1_ref, y2_ref, y3_ref, w_ref, g_ref, fg_ref, o_ref, *, final):
    w = w_ref[...]
    y = None
    for k, y_ref in enumerate((y0_ref, y1_ref, y2_ref, y3_ref)):
        term = w[:, k:k + 1] * y_ref[...].astype(F32)
        y = term if y is None else y + term
    x = x_ref[...] + g_ref[...] * y
    if final:
        x = x * lax.rsqrt(jnp.mean(x * x, axis=-1, keepdims=True) + NORM_EPS) * fg_ref[...]
    o_ref[...] = x


def _combine(x_new, ysg, wts, mods3, final_g, *, rows, tm, d, mod_idx, final):
    assert TOP_K == 4
    kern = functools.partial(_combine_kernel, final=final)
    nblk = rows // tm

    def y_spec(k):
        return pl.BlockSpec((tm, d), lambda i: (k * nblk + i, 0))

    return pl.pallas_call(
        kern,
        grid=(nblk,),
        in_specs=[pl.BlockSpec((tm, d), lambda i: (i, 0)),
                  y_spec(0), y_spec(1), y_spec(2), y_spec(3),
                  pl.BlockSpec((tm, TOP_K), lambda i: (i, 0)),
                  pl.BlockSpec((None, 1, d), lambda i: (mod_idx(i), 0, 5)),
                  pl.BlockSpec((1, d), lambda i: (0, 0))],
        out_specs=pl.BlockSpec((tm, d), lambda i: (i, 0)),
        out_shape=jax.ShapeDtypeStruct((rows, d), F32),
        compiler_params=_cparams(("arbitrary",)),
        name="moe_combine",
    )(x_new, ysg, ysg, ysg, ysg, wts, mods3, final_g.reshape(1, d))


def _rope_tables(batch, seq, n_ctx):
    rows = seq // GRID_W
    row = jnp.repeat(jnp.arange(rows, dtype=F32), GRID_W)
    col = jnp.tile(jnp.arange(GRID_W, dtype=F32), rows)
    quarter = HEAD_DIM // 4
    inv_freq = ROPE_THETA ** (-jnp.arange(quarter, dtype=F32) / quarter)
    ang = jnp.concatenate([row[:, None] * inv_freq, col[:, None] * inv_freq], axis=-1)
    cos, sin = jnp.cos(ang), jnp.sin(ang)
    cos_x = jnp.tile(jnp.concatenate([cos, cos], axis=-1), (batch, 1))
    sin_x = jnp.tile(jnp.concatenate([-sin, sin], axis=-1), (batch, 1))
    cos_f = jnp.concatenate([cos_x, jnp.ones((batch * n_ctx, HEAD_DIM), F32)], axis=0)
    sin_f = jnp.concatenate([sin_x, jnp.zeros((batch * n_ctx, HEAD_DIM), F32)], axis=0)
    return cos_f, sin_f


def _column_params(q_gain, k_gain, ncol):
    hp = _head_perm()
    scale = HEAD_DIM ** -0.5
    k_scale = RET_DK ** -0.5
    cs = jnp.ones((ncol,), F32)
    cs = cs.at[RK:RK + RET_HEADS * RET_DK].set(k_scale)
    cs = cs.at[GQ:GQ + GA_HEADS * HEAD_DIM].set(jnp.tile(q_gain.astype(F32)[hp] * (scale * LOG2_E), GA_HEADS))
    cs = cs.at[GK:GK + GA_KV_HEADS * HEAD_DIM].set(jnp.tile(k_gain.astype(F32)[hp], GA_KV_HEADS))
    cs = cs.at[WQ:WQ + WA_HEADS * HEAD_DIM].set(scale * LOG2_E)
    rm = jnp.zeros((ncol,), F32).at[GQ:GQ + GA_HEADS * HEAD_DIM].set(1.0)
    rm = rm.at[GK:GK + GA_KV_HEADS * HEAD_DIM].set(1.0)
    return cs, rm


def kernel(x, c, ctx, c_ctx, ln1_g, ln2_g, w_ada, b_ada, w_in, ret_decay_logit, ga_q_gain,
           ga_k_gain, wa_sink, w_branch, w_out, w_router, b_router, w_gu, b_gu, w_down, b_down,
           final_g):
    batch, seq, d = x.shape
    n_ctx = ctx.shape[1]
    depth = w_ada.shape[0]
    n_experts = w_router.shape[2]
    ncol = w_in.shape[2]
    mx, mc = batch * seq, batch * n_ctx
    m = mx + mc
    tm = min(1024, mc)
    assert seq % tm == 0 and mc % tm == 0 and batch + 1 <= 8
    tm_stream = min(256, tm)
    assert w_gu.shape[3] // 2 == w_down.shape[3] == d

    def mod_idx_for(tile):
        def mod_idx(i):
            return jnp.where(i < mx // tile, i // (seq // tile), batch)
        return mod_idx

    mod_idx = mod_idx_for(tm)
    mod_idx_s = mod_idx_for(tm_stream)

    cc8 = jnp.zeros((8, d), F32).at[:batch].set(c).at[batch].set(c_ctx)
    mods = _modulation(cc8, w_ada, b_ada)
    cos_f, sin_f = _rope_tables(batch, seq, n_ctx)

    x_all = jnp.concatenate([x.reshape(mx, d), ctx.reshape(mc, d)], axis=0)
    e_tm = 512
    tf = min(1024, d)

    for l in range(depth):
        need_ctx = l < depth - 1
        rows = m if need_ctx else mx
        mods3 = mods[l].reshape(8, 1, 6 * d)
        w_in_p = _regroup_columns(w_in[l])
        colscale, rmsmask = _column_params(ga_q_gain[l], ga_k_gain[l], ncol)
        u = _in_proj(x_all, mods3, ln1_g[l], w_in_p, cos_f, sin_f, colscale, rmsmask,
                     tm=tm, mod_idx=mod_idx)
        o_ret = _retention(u, _retention_tables(ret_decay_logit[l]), batch=batch, seq=seq, n_ctx=n_ctx)
        o_ga = _global_attention(u, batch=batch, seq=seq, n_ctx=n_ctx, with_ctx=need_ctx)
        o_wa = _window_attention(u, wa_sink[l], batch=batch, seq=seq, n_ctx=n_ctx, with_ctx=need_ctx)
        mix = _merge(o_ret, o_ga, o_wa, u, w_branch[l].astype(BF16), rows=rows, tm=tm, d=d)
        x_new = _out_proj(mix, w_out[l].astype(BF16), x_all, mods3, rows=rows, tm=tm, d=d, mod_idx=mod_idx)
        h2, idx, wts = _router(x_new, mods3, ln2_g[l], w_router[l], b_router[l],
                               rows=rows, tm=tm_stream, d=d, mod_idx=mod_idx_s)
        sched, src_tok, pos = _moe_schedule(
            idx, n_experts=n_experts, tm=e_tm, n_chunks=w_down.shape[2] // tf)
        xs = h2.at[src_tok].get(mode="promise_in_bounds")
        act = _moe_up(sched, xs, w_gu, b_gu, layer=l, tm=e_tm, tf=tf)
        ys = _moe_down(sched, act, w_down, b_down, layer=l, tm=e_tm, tn=tf)
        ysg = ys.at[pos].get(mode="promise_in_bounds")
        x_all = _combine(x_new, ysg, wts, mods3, final_g, rows=rows, tm=tm_stream, d=d,
                         mod_idx=mod_idx_s, final=not need_ctx)
    return x_all[:mx].reshape(batch, seq, d)
```

```python
import functools

import jax
import jax.numpy as jnp
from jax import lax
from jax.experimental import pallas as pl
from jax.experimental.pallas import tpu as pltpu

GRID_W = 64
HEAD_DIM = 128
RET_HEADS = 4
RET_DK = 128
RET_DV = 256
GA_HEADS = 8
GA_KV_HEADS = 2
WA_HEADS = 8
WA_KV_HEADS = 2
WINDOW = 128
N_BRANCHES = 3
TOP_K = 4
SWIGLU_LIMIT = 7.0
SWIGLU_ALPHA = 1.702
ROPE_THETA = 10000.0
NORM_EPS = 1e-6
LOG2_E = 1.4426950408889634
MASK_BIAS = -1e30

LANES = 128
CHUNK = 256
GROUP = GA_HEADS // GA_KV_HEADS
MOE_TILE_PARTS = 4
GA_TQ = 256
VMEM_LIMIT = 56 * 1024 * 1024

RQ, RK, GQ, WQ, GK, WK = 0, 512, 1024, 2048, 3072, 3328
ROPE_COLS = 3584
GV, WV, RV, RG, GATE = 3584, 3840, 4096, 5120, 6144
_ORQ, _ORK, _ORV, _ORG, _OGQ, _OGK, _OGV, _OWQ, _OWK, _OWV, _OGATE = (
    0, 512, 1024, 2048, 3072, 4096, 4352, 4608, 5632, 5888, 6144)

F32 = jnp.float32
BF16 = jnp.bfloat16


def _cparams(sem, vmem=VMEM_LIMIT):
    return pltpu.CompilerParams(dimension_semantics=sem, vmem_limit_bytes=vmem)


def _head_perm():
    j = jnp.arange(HEAD_DIM)
    return jnp.where(j < HEAD_DIM // 2, 2 * j, 2 * (j - HEAD_DIM // 2) + 1)


def _regroup_columns(w):
    d_model = w.shape[0]
    half = HEAD_DIM // 2

    def rope_heads(start, n_heads):
        blk = w[:, start:start + n_heads * HEAD_DIM].reshape(d_model, n_heads, half, 2)
        return blk.transpose(0, 1, 3, 2).reshape(d_model, n_heads * HEAD_DIM)

    def plain(start, size):
        return w[:, start:start + size]

    return jnp.concatenate([
        rope_heads(_ORQ, RET_HEADS), rope_heads(_ORK, RET_HEADS),
        rope_heads(_OGQ, GA_HEADS), rope_heads(_OWQ, WA_HEADS),
        rope_heads(_OGK, GA_KV_HEADS), rope_heads(_OWK, WA_KV_HEADS),
        plain(_OGV, GA_KV_HEADS * HEAD_DIM), plain(_OWV, WA_KV_HEADS * HEAD_DIM),
        plain(_ORV, RET_HEADS * RET_DV), plain(_ORG, RET_HEADS * RET_DV),
        plain(_OGATE, w.shape[1] - _OGATE)], axis=1).astype(BF16)


def _mod_kernel(c_ref, w_ref, b_ref, o_ref):
    c = c_ref[...]
    s = (c * jax.nn.sigmoid(c)).astype(BF16)
    o_ref[...] = jnp.dot(s, w_ref[...].astype(BF16), preferred_element_type=F32) + b_ref[...]


def _modulation(cc8, w_ada, b_ada):
    n_layers, d, n = w_ada.shape
    tn = min(1024, n)
    return pl.pallas_call(
        _mod_kernel,
        grid=(n_layers, n // tn),
        in_specs=[pl.BlockSpec((8, d), lambda l, j: (0, 0)),
                  pl.BlockSpec((None, d, tn), lambda l, j: (l, 0, j)),
                  pl.BlockSpec((None, 1, tn), lambda l, j: (l, 0, j))],
        out_specs=pl.BlockSpec((None, 8, tn), lambda l, j: (l, 0, j)),
        out_shape=jax.ShapeDtypeStruct((n_layers, 8, n), F32),
        compiler_params=_cparams(("arbitrary", "arbitrary")),
        name="modulation",
    )(cc8, w_ada, b_ada.reshape(n_layers, 1, n))


def _in_proj_kernel(x_ref, sh_ref, sc_ref, g_ref, w_ref, cos_ref, sin_ref, cs_ref, rm_ref,
                    o_ref, h_ref, acc_ref, *, n_tiles, n_steps, tn):
    s = pl.program_id(0)
    j = s % n_tiles

    @pl.when(s == 0)
    def _():
        acc_ref[1] = jnp.zeros(acc_ref.shape[1:], F32)

    @pl.when((j == 0) & (s < n_steps - 1))
    def _():
        x = x_ref[...]
        y = x * lax.rsqrt(jnp.mean(x * x, axis=-1, keepdims=True) + NORM_EPS) * g_ref[...]
        h_ref[...] = (y * (1.0 + sc_ref[...]) + sh_ref[...]).astype(BF16)

    cur = s % 2
    prev = acc_ref[1 - cur]
    col0 = (jnp.maximum(s - 1, 0) % n_tiles) * tn
    cos = cos_ref[...]
    sin = sin_ref[...]
    for hh in range(tn // HEAD_DIM):
        sl = slice(hh * HEAD_DIM, (hh + 1) * HEAD_DIM)
        raw = prev[:, sl]
        r = lax.rsqrt(jnp.mean(raw * raw, axis=-1, keepdims=True) + NORM_EPS)
        rm = rm_ref[:, sl]
        t = raw * (rm * r + (1.0 - rm)) * cs_ref[:, sl]
        roped = t * cos + pltpu.roll(t, HEAD_DIM // 2, 1) * sin
        is_rope = col0 + hh * HEAD_DIM < ROPE_COLS
        o_ref[:, sl] = jnp.where(is_rope, roped, raw).astype(BF16)
    acc_ref[cur] = jnp.dot(h_ref[...], w_ref[...], preferred_element_type=F32)


def _in_proj(x_all, mods3, ln_g, w_in_p, cos_f, sin_f, colscale, rmsmask, *, tm, mod_idx):
    m, d = x_all.shape
    ncol = w_in_p.shape[1]
    tn = 512
    n_tiles = ncol // tn
    n_rows = m // tm
    n_steps = n_rows * n_tiles + 1

    def row(s):
        return jnp.minimum(s // n_tiles, n_rows - 1)

    def col(s):
        return jnp.where(s < n_steps - 1, s % n_tiles, n_tiles - 1)

    def prow(s):
        return jnp.maximum(s - 1, 0) // n_tiles

    def pcol(s):
        return jnp.maximum(s - 1, 0) % n_tiles

    kern = functools.partial(_in_proj_kernel, n_tiles=n_tiles,
                             n_steps=n_steps, tn=tn)
    return pl.pallas_call(
        kern,
        grid=(n_steps,),
        in_specs=[pl.BlockSpec((tm, d), lambda s: (row(s), 0)),
                  pl.BlockSpec((None, 1, d), lambda s: (mod_idx(row(s)), 0, 0)),
                  pl.BlockSpec((None, 1, d), lambda s: (mod_idx(row(s)), 0, 1)),
                  pl.BlockSpec((1, d), lambda s: (0, 0)),
                  pl.BlockSpec((d, tn), lambda s: (0, col(s))),
                  pl.BlockSpec((tm, HEAD_DIM), lambda s: (prow(s), 0)),
                  pl.BlockSpec((tm, HEAD_DIM), lambda s: (prow(s), 0)),
                  pl.BlockSpec((1, tn), lambda s: (0, pcol(s))),
                  pl.BlockSpec((1, tn), lambda s: (0, pcol(s)))],
        out_specs=pl.BlockSpec((tm, tn), lambda s: (prow(s), pcol(s))),
        out_shape=jax.ShapeDtypeStruct((m, ncol), BF16),
        scratch_shapes=[pltpu.VMEM((tm, d), BF16), pltpu.VMEM((2, tm, tn), F32)],
        compiler_params=_cparams(("arbitrary",)),
        name="in_proj",
    )(x_all, mods3, mods3, ln_g.reshape(1, d), w_in_p, cos_f, sin_f,
      colscale.reshape(1, ncol), rmsmask.reshape(1, ncol))


def _retention_kernel(q_ref, k_ref, v_ref, g_ref, dm_ref, qd_ref, kd_ref, cd_ref,
                      o_ref, s_ref, ob_ref, *, n_lat_chunks):
    d = pl.program_id(1)
    t = pl.program_id(2)

    @pl.when(t == 0)
    def _():
        s_ref[...] = jnp.zeros_like(s_ref)

    cid = jnp.where(d == 1, t, jnp.where(t == 0, 0, 1 + n_lat_chunks - t))
    for h in range(RET_HEADS):
        q = q_ref[:, h * RET_DK:(h + 1) * RET_DK]
        k = k_ref[:, h * RET_DK:(h + 1) * RET_DK]
        v = v_ref[:, h * RET_DV:(h + 1) * RET_DV]
        state = s_ref[h]
        scores = lax.dot_general(q, k, (((1,), (1,)), ((), ())), preferred_element_type=F32)
        p = (scores * dm_ref[h]).astype(BF16)
        qd = (q.astype(F32) * qd_ref[h]).astype(BF16)
        o = (jnp.dot(p, v, preferred_element_type=F32)
             + jnp.dot(qd, state.astype(BF16), preferred_element_type=F32))
        kd = (k.astype(F32) * kd_ref[h]).astype(BF16)
        s_ref[h] = state * cd_ref[h] + lax.dot_general(
            kd, v, (((0,), (0,)), ((), ())), preferred_element_type=F32)
        sl = slice(h * RET_DV, (h + 1) * RET_DV)

        @pl.when(d == 0)
        def _():
            ob_ref[cid, :, sl] = o

        @pl.when(d == 1)
        def _():
            tot = o + ob_ref[cid, :, sl]
            tot = tot * lax.rsqrt(jnp.mean(tot * tot, axis=-1, keepdims=True) + NORM_EPS)
            g = g_ref[:, sl].astype(F32)
            o_ref[:, sl] = (tot * (g * jax.nn.sigmoid(g))).astype(BF16)


def _retention(u, tabs, *, batch, seq, n_ctx):
    m = u.shape[0]
    assert n_ctx == CHUNK and seq % CHUNK == 0
    nc = seq // CHUNK
    lat_blocks = batch * nc
    dmat, qdec, kdec, cdec = tabs

    def rowblk(b, d, t):
        lat = b * nc + jnp.where(d == 1, t - 1, nc - t)
        return jnp.where(t == 0, lat_blocks + b, lat)

    def in_map(col):
        return lambda b, d, t: (rowblk(b, d, t), col)

    def out_map(b, d, t):
        return (jnp.where(d == 1, rowblk(b, 1, t), lat_blocks + b), 0)

    def tab_map(b, d, t):
        return (d, 0, 0, 0)

    qk_w = RET_HEADS * RET_DK
    v_w = RET_HEADS * RET_DV
    kern = functools.partial(_retention_kernel, n_lat_chunks=nc)
    return pl.pallas_call(
        kern,
        grid=(batch, 2, nc + 1),
        in_specs=[pl.BlockSpec((CHUNK, qk_w), in_map(RQ // qk_w)),
                  pl.BlockSpec((CHUNK, qk_w), in_map(RK // qk_w)),
                  pl.BlockSpec((CHUNK, v_w), in_map(RV // v_w)),
                  pl.BlockSpec((CHUNK, v_w), in_map(RG // v_w)),
                  pl.BlockSpec((None, RET_HEADS, CHUNK, CHUNK), tab_map),
                  pl.BlockSpec((None, RET_HEADS, CHUNK, RET_DK), tab_map),
                  pl.BlockSpec((None, RET_HEADS, CHUNK, RET_DK), tab_map),
                  pl.BlockSpec((None, RET_HEADS, 1, RET_DV), tab_map)],
        out_specs=pl.BlockSpec((CHUNK, v_w), out_map),
        out_shape=jax.ShapeDtypeStruct((m, v_w), BF16),
        scratch_shapes=[pltpu.VMEM((RET_HEADS, RET_DK, RET_DV), F32),
                        pltpu.VMEM((nc + 1, CHUNK, v_w), F32)],
        compiler_params=_cparams(("arbitrary", "arbitrary", "arbitrary")),
        name="retention",
    )(u, u, u, u, dmat, qdec, kdec, cdec)


def _retention_tables(decay_logit):
    lg = jax.nn.log_sigmoid(decay_logit.astype(F32))
    lf, lb = lg[0][:, None, None], lg[1][:, None, None]
    pos = jnp.arange(CHUNK, dtype=F32)
    diff = pos[:, None] - pos[None, :]
    d_f = jnp.where(diff >= 0, jnp.exp(jnp.where(diff >= 0, diff, 0.0) * lf), 0.0)
    d_b = jnp.where(diff < 0, jnp.exp(jnp.where(diff < 0, -diff, 0.0) * lb), 0.0)
    ones_k = jnp.ones((1, 1, RET_DK), F32)
    q_f = jnp.exp((pos + 1.0)[None, :, None] * lf) * ones_k
    q_b = jnp.exp((CHUNK - pos)[None, :, None] * lb) * ones_k
    k_f = jnp.exp((CHUNK - 1.0 - pos)[None, :, None] * lf) * ones_k
    k_b = jnp.exp(pos[None, :, None] * lb) * ones_k
    ones_v = jnp.ones((1, 1, RET_DV), F32)
    c_f = jnp.exp(CHUNK * lf) * ones_v
    c_b = jnp.exp(CHUNK * lb) * ones_v
    return (jnp.stack([d_b, d_f]), jnp.stack([q_b, q_f]), jnp.stack([k_b, k_f]),
            jnp.stack([c_b, c_f]))


def _global_attn_kernel(q_ref, kx_ref, vx_ref, kc_ref, vc_ref, o_ref, vxe_ref, vce_ref, *, n_lat_tiles):
    i = pl.program_id(2)
    nt = (((1,), (1,)), ((), ()))

    @pl.when(i == 0)
    def _():
        vxe_ref[:, 0:HEAD_DIM] = vx_ref[...]
        vxe_ref[:, HEAD_DIM:] = jnp.ones_like(vx_ref)
        vce_ref[:, 0:HEAD_DIM] = vc_ref[...]
        vce_ref[:, HEAD_DIM:] = jnp.ones_like(vc_ref)

    def scores(h):
        q = q_ref[:, h * HEAD_DIM:(h + 1) * HEAD_DIM]
        return (lax.dot_general(q, kc_ref[...], nt, preferred_element_type=F32),
                lax.dot_general(q, kx_ref[...], nt, preferred_element_type=F32))

    def store(h, pv):
        o_ref[:, h * HEAD_DIM:(h + 1) * HEAD_DIM] = (pv[:, :HEAD_DIM] / pv[:, HEAD_DIM:]).astype(BF16)

    @pl.when(i < n_lat_tiles)
    def _():
        nxt = scores(0)
        for h in range(GROUP):
            s_c, s_x = nxt
            if h + 1 < GROUP:
                nxt = scores(h + 1)
            mx = jnp.maximum(jnp.max(s_c, axis=-1, keepdims=True), jnp.max(s_x, axis=-1, keepdims=True))
            p_c = jnp.exp2((s_c - mx).astype(BF16))
            p_x = jnp.exp2((s_x - mx).astype(BF16))
            store(h, jnp.dot(p_c, vce_ref[...], preferred_element_type=F32)
                  + jnp.dot(p_x, vxe_ref[...], preferred_element_type=F32))

    @pl.when(i >= n_lat_tiles)
    def _():
        for h in range(GROUP):
            s_c = lax.dot_general(q_ref[:, h * HEAD_DIM:(h + 1) * HEAD_DIM], kc_ref[...], nt,
                                  preferred_element_type=F32)
            p_c = jnp.exp2((s_c - jnp.max(s_c, axis=-1, keepdims=True)).astype(BF16))
            store(h, jnp.dot(p_c, vce_ref[...], preferred_element_type=F32))


def _global_attention(u, *, batch, seq, n_ctx, with_ctx):
    m = u.shape[0]
    tq = GA_TQ
    nq = seq // tq
    n_ctx_tiles = n_ctx // tq
    lat_blocks = batch * nq
    gw = GROUP * HEAD_DIM

    def qrow(b, i):
        return jnp.where(i < nq, b * nq + i, lat_blocks + b * n_ctx_tiles + (i - nq))

    kern = functools.partial(_global_attn_kernel, n_lat_tiles=nq)
    return pl.pallas_call(
        kern,
        grid=(batch, GA_KV_HEADS, nq + (n_ctx_tiles if with_ctx else 0)),
        in_specs=[pl.BlockSpec((tq, gw), lambda b, kh, i: (qrow(b, i), GQ // gw + kh)),
                  pl.BlockSpec((seq, HEAD_DIM), lambda b, kh, i: (b, GK // HEAD_DIM + kh)),
                  pl.BlockSpec((seq, HEAD_DIM), lambda b, kh, i: (b, GV // HEAD_DIM + kh)),
                  pl.BlockSpec((n_ctx, HEAD_DIM),
                               lambda b, kh, i: (batch * seq // n_ctx + b, GK // HEAD_DIM + kh)),
                  pl.BlockSpec((n_ctx, HEAD_DIM),
                               lambda b, kh, i: (batch * seq // n_ctx + b, GV // HEAD_DIM + kh))],
        out_specs=pl.BlockSpec((tq, gw), lambda b, kh, i: (qrow(b, i), kh)),
        out_shape=jax.ShapeDtypeStruct((m, GA_HEADS * HEAD_DIM), BF16),
        scratch_shapes=[pltpu.VMEM((seq, 2 * HEAD_DIM), BF16), pltpu.VMEM((n_ctx, 2 * HEAD_DIM), BF16)],
        compiler_params=_cparams(("arbitrary", "arbitrary", "arbitrary")),
        name="global_attention",
    )(u, u, u, u, u)


def _window_attn_kernel(sink_ref, q_ref, kx_ref, vx_ref, kc_ref, vc_ref, bias_ref, o_ref, vce_ref,
                        *, n_lat_tiles, seq):
    kh = pl.program_id(1)
    i = pl.program_id(2)
    span = CHUNK + 2 * WINDOW
    nt = (((1,), (1,)), ((), ()))

    @pl.when(i == 0)
    def _():
        vce_ref[:, 0:HEAD_DIM] = vc_ref[...]
        vce_ref[:, HEAD_DIM:] = jnp.ones_like(vc_ref)

    start = jnp.clip(i * CHUNK - WINDOW, 0, seq - span)
    start = pl.multiple_of(jnp.where(i < n_lat_tiles, start, 0), WINDOW)
    kw = kx_ref[pl.ds(start, span), :]
    vw = vx_ref[pl.ds(start, span), :]
    vwe = jnp.concatenate([vw, jnp.ones_like(vw)], axis=1)
    for h in range(GROUP):
        q = q_ref[:, h * HEAD_DIM:(h + 1) * HEAD_DIM]
        s_w = lax.dot_general(q, kw, nt, preferred_element_type=F32) + bias_ref[...]
        s_c = lax.dot_general(q, kc_ref[...], nt, preferred_element_type=F32)
        sink = sink_ref[kh * GROUP + h]
        mx = jnp.maximum(jnp.maximum(jnp.max(s_w, axis=-1, keepdims=True),
                                     jnp.max(s_c, axis=-1, keepdims=True)), sink)
        p_w = jnp.exp2((s_w - mx).astype(BF16))
        p_c = jnp.exp2((s_c - mx).astype(BF16))
        pv = (jnp.dot(p_w, vwe, preferred_element_type=F32)
              + jnp.dot(p_c, vce_ref[...], preferred_element_type=F32))
        den = pv[:, HEAD_DIM:] + jnp.exp2(sink - mx)
        o_ref[:, h * HEAD_DIM:(h + 1) * HEAD_DIM] = (pv[:, :HEAD_DIM] / den).astype(BF16)


def _window_bias():
    span = CHUNK + 2 * WINDOW
    r = jnp.arange(CHUNK)[:, None]
    col = jnp.arange(span)[None, :]
    offs = (0, WINDOW, 2 * WINDOW)
    tiles = [jnp.where(jnp.abs(col - off - r) <= WINDOW, 0.0, MASK_BIAS) for off in offs]
    tiles.append(jnp.full((CHUNK, span), MASK_BIAS))
    return jnp.stack(tiles).astype(F32)


def _window_attention(u, sink, *, batch, seq, n_ctx, with_ctx):
    m = u.shape[0]
    nq = seq // CHUNK
    lat_blocks = batch * nq
    gw = GROUP * HEAD_DIM
    span = CHUNK + 2 * WINDOW
    assert seq >= span and nq >= 2

    def qrow(b, i):
        return jnp.where(i < nq, b * nq + i, lat_blocks + b)

    def variant(i):
        return jnp.where(i == 0, 0, jnp.where(i < nq - 1, 1, jnp.where(i == nq - 1, 2, 3)))

    kern = functools.partial(_window_attn_kernel, n_lat_tiles=nq, seq=seq)
    return pl.pallas_call(
        kern,
        grid=(batch, WA_KV_HEADS, nq + (1 if with_ctx else 0)),
        in_specs=[pl.BlockSpec(memory_space=pltpu.SMEM),
                  pl.BlockSpec((CHUNK, gw), lambda b, kh, i: (qrow(b, i), WQ // gw + kh)),
                  pl.BlockSpec((seq, HEAD_DIM), lambda b, kh, i: (b, WK // HEAD_DIM + kh)),
                  pl.BlockSpec((seq, HEAD_DIM), lambda b, kh, i: (b, WV // HEAD_DIM + kh)),
                  pl.BlockSpec((n_ctx, HEAD_DIM),
                               lambda b, kh, i: (batch * seq // n_ctx + b, WK // HEAD_DIM + kh)),
                  pl.BlockSpec((n_ctx, HEAD_DIM),
                               lambda b, kh, i: (batch * seq // n_ctx + b, WV // HEAD_DIM + kh)),
                  pl.BlockSpec((None, CHUNK, span), lambda b, kh, i: (variant(i), 0, 0))],
        out_specs=pl.BlockSpec((CHUNK, gw), lambda b, kh, i: (qrow(b, i), kh)),
        out_shape=jax.ShapeDtypeStruct((m, WA_HEADS * HEAD_DIM), BF16),
        scratch_shapes=[pltpu.VMEM((n_ctx, 2 * HEAD_DIM), BF16)],
        compiler_params=_cparams(("arbitrary", "arbitrary", "arbitrary")),
        name="window_attention",
    )(sink.astype(F32) * LOG2_E, u, u, u, u, u, _window_bias())


def _merge_kernel(o0_ref, o1_ref, o2_ref, g0_ref, g1_ref, g2_ref, w0_ref, w1_ref, w2_ref, out_ref):
    tot = None
    for o_ref, g_ref, w_ref in ((o0_ref, g0_ref, w0_ref), (o1_ref, g1_ref, w1_ref),
                                (o2_ref, g2_ref, w2_ref)):
        term = jax.nn.sigmoid(g_ref[...].astype(F32)) * jnp.dot(
            o_ref[...], w_ref[...], preferred_element_type=F32)
        tot = term if tot is None else tot + term
    out_ref[...] = tot.astype(BF16)


def _merge(o_ret, o_ga, o_wa, u, w_branch, *, rows, tm, d):
    tn = 512
    bw = w_branch.shape[1]
    gate_blk = GATE // tn
    nd = d // tn

    def o_spec():
        return pl.BlockSpec((tm, bw), lambda i, j: (i, 0))

    def g_spec(br):
        return pl.BlockSpec((tm, tn), lambda i, j: (i, gate_blk + br * nd + j))

    def w_spec(br):
        return pl.BlockSpec((None, bw, tn), lambda i, j: (br, 0, j))

    return pl.pallas_call(
        _merge_kernel,
        grid=(rows // tm, nd),
        in_specs=[o_spec(), o_spec(), o_spec(), g_spec(0), g_spec(1), g_spec(2),
                  w_spec(0), w_spec(1), w_spec(2)],
        out_specs=pl.BlockSpec((tm, tn), lambda i, j: (i, j)),
        out_shape=jax.ShapeDtypeStruct((rows, d), BF16),
        compiler_params=_cparams(("arbitrary", "arbitrary")),
        name="branch_merge",
    )(o_ret, o_ga, o_wa, u, u, u, w_branch, w_branch, w_branch)


def _out_proj_kernel(m_ref, w_ref, x_ref, g_ref, o_ref):
    o_ref[...] = x_ref[...] + g_ref[...] * jnp.dot(m_ref[...], w_ref[...], preferred_element_type=F32)


def _out_proj(mix, w_out, x_all, mods3, *, rows, tm, d, mod_idx):
    tn = 512
    nd = d // tn
    return pl.pallas_call(
        _out_proj_kernel,
        grid=(rows // tm, nd),
        in_specs=[pl.BlockSpec((tm, d), lambda i, j: (i, 0)),
                  pl.BlockSpec((d, tn), lambda i, j: (0, j)),
                  pl.BlockSpec((tm, tn), lambda i, j: (i, j)),
                  pl.BlockSpec((None, 1, tn), lambda i, j: (mod_idx(i), 0, 2 * nd + j))],
        out_specs=pl.BlockSpec((tm, tn), lambda i, j: (i, j)),
        out_shape=jax.ShapeDtypeStruct((rows, d), F32),
        compiler_params=_cparams(("arbitrary", "arbitrary")),
        name="out_proj",
    )(mix, w_out, x_all, mods3)


def _router_kernel(x_ref, sh_ref, sc_ref, g_ref, whi_ref, wlo_ref, br_ref, h_ref, idx_ref, wt_ref,
                   *, n_experts):
    x = x_ref[...]
    y = x * lax.rsqrt(jnp.mean(x * x, axis=-1, keepdims=True) + NORM_EPS) * g_ref[...]
    h = y * (1.0 + sc_ref[...]) + sh_ref[...]
    h_hi = h.astype(BF16)
    h_ref[...] = h_hi
    h_lo = (h - h_hi.astype(F32)).astype(BF16)
    logits = (jnp.dot(h_hi, whi_ref[...], preferred_element_type=F32)
              + jnp.dot(h_lo, whi_ref[...], preferred_element_type=F32)
              + jnp.dot(h_hi, wlo_ref[...], preferred_element_type=F32)) + br_ref[...]
    lane = lax.broadcasted_iota(jnp.int32, logits.shape, 1).astype(F32)
    vals, ids = [], []
    cur = logits
    for _ in range(TOP_K):
        mx = jnp.max(cur, axis=-1, keepdims=True)
        sel = jnp.min(jnp.where(cur == mx, lane, float(n_experts)), axis=-1, keepdims=True)
        vals.append(mx)
        ids.append(sel)
        cur = jnp.where(lane == sel, -jnp.inf, cur)
    e = [jnp.exp(v - vals[0]) for v in vals]
    den = e[0] + e[1] + e[2] + e[3]
    for k in range(TOP_K):
        idx_ref[:, k:k + 1] = ids[k].astype(jnp.int32)
        wt_ref[:, k:k + 1] = e[k] / den


def _router(x_new, mods3, ln_g, w_router, b_router, *, rows, tm, d, mod_idx):
    n_experts = w_router.shape[1]
    w_hi = w_router.astype(BF16)
    w_lo = (w_router - w_hi.astype(F32)).astype(BF16)
    kern = functools.partial(_router_kernel, n_experts=n_experts)
    return pl.pallas_call(
        kern,
        grid=(rows // tm,),
        in_specs=[pl.BlockSpec((tm, d), lambda i: (i, 0)),
                  pl.BlockSpec((None, 1, d), lambda i: (mod_idx(i), 0, 3)),
                  pl.BlockSpec((None, 1, d), lambda i: (mod_idx(i), 0, 4)),
                  pl.BlockSpec((1, d), lambda i: (0, 0)),
                  pl.BlockSpec((d, n_experts), lambda i: (0, 0)),
                  pl.BlockSpec((d, n_experts), lambda i: (0, 0)),
                  pl.BlockSpec((1, n_experts), lambda i: (0, 0))],
        out_specs=[pl.BlockSpec((tm, d), lambda i: (i, 0)),
                   pl.BlockSpec((tm, TOP_K), lambda i: (i, 0)),
                   pl.BlockSpec((tm, TOP_K), lambda i: (i, 0))],
        out_shape=[jax.ShapeDtypeStruct((rows, d), BF16),
                   jax.ShapeDtypeStruct((rows, TOP_K), jnp.int32),
                   jax.ShapeDtypeStruct((rows, TOP_K), F32)],
        compiler_params=_cparams(("arbitrary",)),
        name="norm2_router",
    )(x_new, mods3, mods3, ln_g.reshape(1, d), w_hi, w_lo, b_router.reshape(1, n_experts))


def _moe_up_kernel(ie_ref, ic_ref, it_ref, in_ref, x_ref, wg_ref, wl_ref, bg_ref, bl_ref, o_ref):
    i = pl.program_id(0)

    def compute(n):
        x = x_ref[0:n, :]
        glu = jnp.dot(x, wg_ref[...].astype(BF16), preferred_element_type=F32) + bg_ref[...]
        lin = jnp.dot(x, wl_ref[...].astype(BF16), preferred_element_type=F32) + bl_ref[...]
        glu = jnp.minimum(glu, SWIGLU_LIMIT)
        lin = jnp.clip(lin, -SWIGLU_LIMIT, SWIGLU_LIMIT)
        o_ref[0:n, :] = (glu * jax.nn.sigmoid(SWIGLU_ALPHA * glu) * (lin + 1.0)).astype(BF16)

    tm = x_ref.shape[0]
    for quarters in range(1, MOE_TILE_PARTS + 1):
        n = quarters * tm // MOE_TILE_PARTS
        pl.when(in_ref[i] == n)(functools.partial(compute, n))


def _moe_up(sched, xs, w_gu, b_gu, *, layer, tm, tf):
    r_pad, d = xs.shape
    n_layers, n_experts, _, f2 = w_gu.shape
    f = f2 // 2
    nc = f // tf
    n_items = sched[0].shape[0]
    grid_spec = pltpu.PrefetchScalarGridSpec(
        num_scalar_prefetch=4,
        grid=(n_items,),
        in_specs=[pl.BlockSpec((tm, d), lambda i, ie, ic, it, nr: (it[i], 0)),
                  pl.BlockSpec((None, None, d, tf), lambda i, ie, ic, it, nr: (layer, ie[i], 0, ic[i])),
                  pl.BlockSpec((None, None, d, tf),
                               lambda i, ie, ic, it, nr: (layer, ie[i], 0, nc + ic[i])),
                  pl.BlockSpec((None, None, 1, tf), lambda i, ie, ic, it, nr: (layer, ie[i], 0, ic[i])),
                  pl.BlockSpec((None, None, 1, tf),
                               lambda i, ie, ic, it, nr: (layer, ie[i], 0, nc + ic[i]))],
        out_specs=pl.BlockSpec((tm, tf), lambda i, ie, ic, it, nr: (it[i], ic[i])))
    b4 = b_gu.reshape(n_layers, n_experts, 1, f2)
    return pl.pallas_call(
        _moe_up_kernel,
        grid_spec=grid_spec,
        out_shape=jax.ShapeDtypeStruct((r_pad, f), BF16),
        compiler_params=_cparams(("arbitrary",)),
        name="moe_up",
    )(*sched, xs, w_gu, w_gu, b4, b4)


def _moe_down_kernel(ie_ref, ic_ref, it_ref, in_ref, a_ref, w_ref, b_ref, o_ref):
    i = pl.program_id(0)

    def compute(n):
        y = jnp.dot(a_ref[0:n, :], w_ref[...].astype(BF16), preferred_element_type=F32) + b_ref[...]
        o_ref[0:n, :] = y.astype(BF16)

    tm = a_ref.shape[0]
    for quarters in range(1, MOE_TILE_PARTS + 1):
        n = quarters * tm // MOE_TILE_PARTS
        pl.when(in_ref[i] == n)(functools.partial(compute, n))


def _moe_down(sched, act, w_down, b_down, *, layer, tm, tn):
    r_pad, f = act.shape
    n_layers, n_experts, _, d = w_down.shape
    n_items = sched[0].shape[0]
    grid_spec = pltpu.PrefetchScalarGridSpec(
        num_scalar_prefetch=4,
        grid=(n_items,),
        in_specs=[pl.BlockSpec((tm, f), lambda i, ie, ic, it, nr: (it[i], 0)),
                  pl.BlockSpec((None, None, f, tn), lambda i, ie, ic, it, nr: (layer, ie[i], 0, ic[i])),
                  pl.BlockSpec((None, None, 1, tn), lambda i, ie, ic, it, nr: (layer, ie[i], 0, ic[i]))],
        out_specs=pl.BlockSpec((tm, tn), lambda i, ie, ic, it, nr: (it[i], ic[i])))
    return pl.pallas_call(
        _moe_down_kernel,
        grid_spec=grid_spec,
        out_shape=jax.ShapeDtypeStruct((r_pad, d), BF16),
        compiler_params=_cparams(("arbitrary",)),
        name="moe_down",
    )(*sched, act, w_down, b_down.reshape(n_layers, n_experts, 1, d))


def _moe_schedule(idx, *, n_experts, tm, n_chunks):
    n_tok = idx.shape[0]
    n_pairs = n_tok * TOP_K
    r_pad = n_pairs + n_experts * tm
    n_tiles = r_pad // tm
    part = tm // MOE_TILE_PARTS
    flat_e = idx.reshape(-1)
    experts = jnp.arange(n_experts, dtype=jnp.int32)
    onehot = (flat_e[:, None] == experts[None, :]).astype(jnp.int32)
    csum = jnp.cumsum(onehot, axis=0)
    rank = jnp.sum(csum * onehot, axis=1) - 1
    counts = csum[-1]
    ntiles = (counts + tm - 1) // tm
    tile_end = jnp.cumsum(ntiles)
    tile_start = tile_end - ntiles
    dest = jnp.sum(onehot * tile_start[None, :], axis=1) * tm + rank
    src_tok = (jnp.arange(r_pad, dtype=jnp.int32) % n_tok).at[dest].set(
        jnp.arange(n_pairs, dtype=jnp.int32) // TOP_K, unique_indices=True, mode="promise_in_bounds")
    n_used = tile_end[-1]

    def schedule(n_chunks):
        n_items = n_chunks * n_tiles
        item = jnp.arange(n_items, dtype=jnp.int32)
        valid = item < n_chunks * n_used
        ic_ = jnp.minimum(item, jnp.maximum(n_chunks * n_used - 1, 0))
        e = jnp.minimum(
            jnp.sum((ic_[:, None] >= n_chunks * tile_end[None, :]).astype(jnp.int32), axis=1),
            n_experts - 1)
        sel = (e[:, None] == experts[None, :]).astype(jnp.int32)
        nt_e = jnp.maximum(jnp.sum(sel * ntiles[None, :], axis=1), 1)
        ts_e = jnp.sum(sel * tile_start[None, :], axis=1)
        cnt_e = jnp.sum(sel * counts[None, :], axis=1)
        r = ic_ - n_chunks * ts_e
        c = r // nt_e
        lt = nt_e - 1 - r % nt_e
        t = ts_e + lt
        left = cnt_e - lt * tm
        nrows = jnp.where(valid, jnp.minimum((left + part - 1) // part * part, tm), 0)
        return (e.astype(jnp.int32), c.astype(jnp.int32), t.astype(jnp.int32),
                nrows.astype(jnp.int32))

    pos = dest.reshape(n_tok, TOP_K).T.reshape(-1)
    return [schedule(n) for n in n_chunks], src_tok, pos


def _combine_kernel(x_ref, y0_ref, y1_ref, y2_ref, y3_ref, w_ref, g_ref, fg_ref, o_ref, *, final):
    w = w_ref[...]
    y = None
    for k, y_ref in enumerate((y0_ref, y1_ref, y2_ref, y3_ref)):
        term = w[:, k:k + 1] * y_ref[...].astype(F32)
        y = term if y is None else y + term
    x = x_ref[...] + g_ref[...] * y
    if final:
        x = x * lax.rsqrt(jnp.mean(x * x, axis=-1, keepdims=True) + NORM_EPS) * fg_ref[...]
    o_ref[...] = x


def _combine(x_new, ysg, wts, mods3, final_g, *, rows, tm, d, mod_idx, final):
    assert TOP_K == 4
    kern = functools.partial(_combine_kernel, final=final)
    nblk = rows // tm

    def y_spec(k):
        return pl.BlockSpec((tm, d), lambda i: (k * nblk + i, 0))

    return pl.pallas_call(
        kern,
        grid=(nblk,),
        in_specs=[pl.BlockSpec((tm, d), lambda i: (i, 0)),
                  y_spec(0), y_spec(1), y_spec(2), y_spec(3),
                  pl.BlockSpec((tm, TOP_K), lambda i: (i, 0)),
                  pl.BlockSpec((None, 1, d), lambda i: (mod_idx(i), 0, 5)),
                  pl.BlockSpec((1, d), lambda i: (0, 0))],
        out_specs=pl.BlockSpec((tm, d), lambda i: (i, 0)),
        out_shape=jax.ShapeDtypeStruct((rows, d), F32),
        compiler_params=_cparams(("arbitrary",)),
        name="moe_combine",
    )(x_new, ysg, ysg, ysg, ysg, wts, mods3, final_g.reshape(1, d))


def _rope_tables(batch, seq, n_ctx):
    rows = seq // GRID_W
    row = jnp.repeat(jnp.arange(rows, dtype=F32), GRID_W)
    col = jnp.tile(jnp.arange(GRID_W, dtype=F32), rows)
    quarter = HEAD_DIM // 4
    inv_freq = ROPE_THETA ** (-jnp.arange(quarter, dtype=F32) / quarter)
    ang = jnp.concatenate([row[:, None] * inv_freq, col[:, None] * inv_freq], axis=-1)
    cos, sin = jnp.cos(ang), jnp.sin(ang)
    cos_x = jnp.tile(jnp.concatenate([cos, cos], axis=-1), (batch, 1))
    sin_x = jnp.tile(jnp.concatenate([-sin, sin], axis=-1), (batch, 1))
    cos_f = jnp.concatenate([cos_x, jnp.ones((batch * n_ctx, HEAD_DIM), F32)], axis=0)
    sin_f = jnp.concatenate([sin_x, jnp.zeros((batch * n_ctx, HEAD_DIM), F32)], axis=0)
    return cos_f, sin_f


def _column_params(q_gain, k_gain, ncol):
    hp = _head_perm()
    scale = HEAD_DIM ** -0.5
    k_scale = RET_DK ** -0.5
    cs = jnp.ones((ncol,), F32)
    cs = cs.at[RK:RK + RET_HEADS * RET_DK].set(k_scale)
    cs = cs.at[GQ:GQ + GA_HEADS * HEAD_DIM].set(jnp.tile(q_gain.astype(F32)[hp] * (scale * LOG2_E), GA_HEADS))
    cs = cs.at[GK:GK + GA_KV_HEADS * HEAD_DIM].set(jnp.tile(k_gain.astype(F32)[hp], GA_KV_HEADS))
    cs = cs.at[WQ:WQ + WA_HEADS * HEAD_DIM].set(scale * LOG2_E)
    rm = jnp.zeros((ncol,), F32).at[GQ:GQ + GA_HEADS * HEAD_DIM].set(1.0)
    rm = rm.at[GK:GK + GA_KV_HEADS * HEAD_DIM].set(1.0)
    return cs, rm


def kernel(x, c, ctx, c_ctx, ln1_g, ln2_g, w_ada, b_ada, w_in, ret_decay_logit, ga_q_gain,
           ga_k_gain, wa_sink, w_branch, w_out, w_router, b_router, w_gu, b_gu, w_down, b_down,
           final_g):
    batch, seq, d = x.shape
    n_ctx = ctx.shape[1]
    depth = w_ada.shape[0]
    n_experts = w_router.shape[2]
    ncol = w_in.shape[2]
    mx, mc = batch * seq, batch * n_ctx
    m = mx + mc
    tm = min(1024, mc)
    assert seq % tm == 0 and mc % tm == 0 and batch + 1 <= 8
    tm_stream = min(256, tm)
    assert w_gu.shape[3] // 2 == w_down.shape[3] == d

    def mod_idx_for(tile):
        def mod_idx(i):
            return jnp.where(i < mx // tile, i // (seq // tile), batch)
        return mod_idx

    mod_idx = mod_idx_for(tm)
    mod_idx_s = mod_idx_for(tm_stream)

    cc8 = jnp.zeros((8, d), F32).at[:batch].set(c).at[batch].set(c_ctx)
    mods = _modulation(cc8, w_ada, b_ada)
    cos_f, sin_f = _rope_tables(batch, seq, n_ctx)

    x_all = jnp.concatenate([x.reshape(mx, d), ctx.reshape(mc, d)], axis=0)
    e_tm = 512
    tf = min(1024, d)

    for l in range(depth):
        need_ctx = l < depth - 1
        rows = m if need_ctx else mx
        mods3 = mods[l].reshape(8, 1, 6 * d)
        w_in_p = _regroup_columns(w_in[l])
        colscale, rmsmask = _column_params(ga_q_gain[l], ga_k_gain[l], ncol)
        u = _in_proj(x_all, mods3, ln1_g[l], w_in_p, cos_f, sin_f, colscale, rmsmask,
                     tm=tm, mod_idx=mod_idx)
        o_ret = _retention(u, _retention_tables(ret_decay_logit[l]), batch=batch, seq=seq, n_ctx=n_ctx)
        o_ga = _global_attention(u, batch=batch, seq=seq, n_ctx=n_ctx, with_ctx=need_ctx)
        o_wa = _window_attention(u, wa_sink[l], batch=batch, seq=seq, n_ctx=n_ctx, with_ctx=need_ctx)
        mix = _merge(o_ret, o_ga, o_wa, u, w_branch[l].astype(BF16), rows=rows, tm=tm, d=d)
        x_new = _out_proj(mix, w_out[l].astype(BF16), x_all, mods3, rows=rows, tm=tm, d=d, mod_idx=mod_idx)
        h2, idx, wts = _router(x_new, mods3, ln2_g[l], w_router[l], b_router[l],
                               rows=rows, tm=tm_stream, d=d, mod_idx=mod_idx_s)
        (sched_up, sched_down), src_tok, pos = _moe_schedule(
            idx, n_experts=n_experts, tm=e_tm, n_chunks=(d // tf, 1))
        xs = h2.at[src_tok].get(mode="promise_in_bounds")
        act = _moe_up(sched_up, xs, w_gu, b_gu, layer=l, tm=e_tm, tf=tf)
        ys = _moe_down(sched_down, act, w_down, b_down, layer=l, tm=e_tm, tn=d)
        ysg = ys.at[pos].get(mode="promise_in_bounds")
        x_all = _combine(x_new, ysg, wts, mods3, final_g, rows=rows, tm=tm_stream, d=d,
                         mod_idx=mod_idx_s, final=not need_ctx)
    return x_all[:mx].reshape(batch, seq, d)
```

```python
import functools

import jax
import jax.numpy as jnp
from jax import lax
from jax.experimental import pallas as pl
from jax.experimental.pallas import tpu as pltpu

GRID_W = 64
HEAD_DIM = 128
RET_HEADS = 4
RET_DK = 128
RET_DV = 256
GA_HEADS = 8
GA_KV_HEADS = 2
WA_HEADS = 8
WA_KV_HEADS = 2
WINDOW = 128
N_BRANCHES = 3
TOP_K = 4
SWIGLU_LIMIT = 7.0
SWIGLU_ALPHA = 1.702
ROPE_THETA = 10000.0
NORM_EPS = 1e-6
LOG2_E = 1.4426950408889634
MASK_BIAS = -1e30

LANES = 128
CHUNK = 256
GROUP = GA_HEADS // GA_KV_HEADS
IN_PROJ_MAX_ROWS = 1152
MOE_TILE_PARTS = 4
GA_TQ = 256
VMEM_LIMIT = 56 * 1024 * 1024

RQ, RK, GQ, WQ, GK, WK = 0, 512, 1024, 2048, 3072, 3328
ROPE_COLS = 3584
GV, WV, RV, RG, GATE = 3584, 3840, 4096, 5120, 6144
_ORQ, _ORK, _ORV, _ORG, _OGQ, _OGK, _OGV, _OWQ, _OWK, _OWV, _OGATE = (
    0, 512, 1024, 2048, 3072, 4096, 4352, 4608, 5632, 5888, 6144)

F32 = jnp.float32
BF16 = jnp.bfloat16


def _cparams(sem, vmem=VMEM_LIMIT):
    return pltpu.CompilerParams(dimension_semantics=sem, vmem_limit_bytes=vmem)


def _head_perm():
    j = jnp.arange(HEAD_DIM)
    return jnp.where(j < HEAD_DIM // 2, 2 * j, 2 * (j - HEAD_DIM // 2) + 1)


def _regroup_columns(w):
    d_model = w.shape[0]
    half = HEAD_DIM // 2

    def rope_heads(start, n_heads):
        blk = w[:, start:start + n_heads * HEAD_DIM].reshape(d_model, n_heads, half, 2)
        return blk.transpose(0, 1, 3, 2).reshape(d_model, n_heads * HEAD_DIM)

    def plain(start, size):
        return w[:, start:start + size]

    return jnp.concatenate([
        rope_heads(_ORQ, RET_HEADS), rope_heads(_ORK, RET_HEADS),
        rope_heads(_OGQ, GA_HEADS), rope_heads(_OWQ, WA_HEADS),
        rope_heads(_OGK, GA_KV_HEADS), rope_heads(_OWK, WA_KV_HEADS),
        plain(_OGV, GA_KV_HEADS * HEAD_DIM), plain(_OWV, WA_KV_HEADS * HEAD_DIM),
        plain(_ORV, RET_HEADS * RET_DV), plain(_ORG, RET_HEADS * RET_DV),
        plain(_OGATE, w.shape[1] - _OGATE)], axis=1).astype(BF16)


def _mod_kernel(c_ref, w_ref, b_ref, o_ref):
    c = c_ref[...]
    s = (c * jax.nn.sigmoid(c)).astype(BF16)
    o_ref[...] = jnp.dot(s, w_ref[...].astype(BF16), preferred_element_type=F32) + b_ref[...]


def _modulation(cc8, w_ada, b_ada):
    n_layers, d, n = w_ada.shape
    tn = min(1024, n)
    return pl.pallas_call(
        _mod_kernel,
        grid=(n_layers, n // tn),
        in_specs=[pl.BlockSpec((8, d), lambda l, j: (0, 0)),
                  pl.BlockSpec((None, d, tn), lambda l, j: (l, 0, j)),
                  pl.BlockSpec((None, 1, tn), lambda l, j: (l, 0, j))],
        out_specs=pl.BlockSpec((None, 8, tn), lambda l, j: (l, 0, j)),
        out_shape=jax.ShapeDtypeStruct((n_layers, 8, n), F32),
        compiler_params=_cparams(("arbitrary", "arbitrary")),
        name="modulation",
    )(cc8, w_ada, b_ada.reshape(n_layers, 1, n))


def _prenorm_kernel(x_ref, sh_ref, sc_ref, g_ref, h_ref):
    x = x_ref[...]
    y = x * lax.rsqrt(jnp.mean(x * x, axis=-1, keepdims=True) + NORM_EPS) * g_ref[...]
    h_ref[...] = (y * (1.0 + sc_ref[...]) + sh_ref[...]).astype(BF16)


def _prenorm(x_all, mods3, ln_g, *, tm, mod_idx):
    m, d = x_all.shape
    return pl.pallas_call(
        _prenorm_kernel,
        grid=(m // tm,),
        in_specs=[pl.BlockSpec((tm, d), lambda i: (i, 0)),
                  pl.BlockSpec((None, 1, d), lambda i: (mod_idx(i), 0, 0)),
                  pl.BlockSpec((None, 1, d), lambda i: (mod_idx(i), 0, 1)),
                  pl.BlockSpec((1, d), lambda i: (0, 0))],
        out_specs=pl.BlockSpec((tm, d), lambda i: (i, 0)),
        out_shape=jax.ShapeDtypeStruct((m, d), BF16),
        compiler_params=_cparams(("arbitrary",)),
        name="prenorm",
    )(x_all, mods3, mods3, ln_g.reshape(1, d))


def _in_proj_kernel(h_ref, w_ref, cos_ref, sin_ref, cs_ref, rm_ref, o_ref, acc_ref, *, n_tiles, tn):
    s = pl.program_id(0)

    @pl.when(s == 0)
    def _():
        acc_ref[1] = jnp.zeros(acc_ref.shape[1:], F32)

    cur = s % 2
    prev = acc_ref[1 - cur]
    col0 = (jnp.maximum(s - 1, 0) % n_tiles) * tn
    cos = cos_ref[...]
    sin = sin_ref[...]
    for hh in range(tn // HEAD_DIM):
        sl = slice(hh * HEAD_DIM, (hh + 1) * HEAD_DIM)
        raw = prev[:, sl]
        r = lax.rsqrt(jnp.mean(raw * raw, axis=-1, keepdims=True) + NORM_EPS)
        rm = rm_ref[:, sl]
        t = raw * (rm * r + (1.0 - rm)) * cs_ref[:, sl]
        roped = t * cos + pltpu.roll(t, HEAD_DIM // 2, 1) * sin
        is_rope = col0 + hh * HEAD_DIM < ROPE_COLS
        o_ref[:, sl] = jnp.where(is_rope, roped, raw).astype(BF16)
    acc_ref[cur] = jnp.dot(h_ref[...], w_ref[...], preferred_element_type=F32)


def _in_proj(h, w_in_p, cos_f, sin_f, colscale, rmsmask):
    m, d = h.shape
    ncol = w_in_p.shape[1]
    tn = 512
    tm = max(t for t in range(LANES, IN_PROJ_MAX_ROWS + 1, LANES) if m % t == 0)
    n_tiles = ncol // tn
    n_rows = m // tm
    n_steps = n_rows * n_tiles + 1

    def row(s):
        return jnp.minimum(s // n_tiles, n_rows - 1)

    def col(s):
        return jnp.where(s < n_steps - 1, s % n_tiles, n_tiles - 1)

    def prow(s):
        return jnp.maximum(s - 1, 0) // n_tiles

    def pcol(s):
        return jnp.maximum(s - 1, 0) % n_tiles

    kern = functools.partial(_in_proj_kernel, n_tiles=n_tiles, tn=tn)
    return pl.pallas_call(
        kern,
        grid=(n_steps,),
        in_specs=[pl.BlockSpec((tm, d), lambda s: (row(s), 0)),
                  pl.BlockSpec((d, tn), lambda s: (0, col(s))),
                  pl.BlockSpec((tm, HEAD_DIM), lambda s: (prow(s), 0)),
                  pl.BlockSpec((tm, HEAD_DIM), lambda s: (prow(s), 0)),
                  pl.BlockSpec((1, tn), lambda s: (0, pcol(s))),
                  pl.BlockSpec((1, tn), lambda s: (0, pcol(s)))],
        out_specs=pl.BlockSpec((tm, tn), lambda s: (prow(s), pcol(s))),
        out_shape=jax.ShapeDtypeStruct((m, ncol), BF16),
        scratch_shapes=[pltpu.VMEM((2, tm, tn), F32)],
        compiler_params=_cparams(("arbitrary",)),
        name="in_proj",
    )(h, w_in_p, cos_f, sin_f, colscale.reshape(1, ncol), rmsmask.reshape(1, ncol))


def _retention_kernel(q_ref, k_ref, v_ref, g_ref, dm_ref, qd_ref, kd_ref, cd_ref,
                      o_ref, s_ref, ob_ref, *, n_lat_chunks):
    d = pl.program_id(1)
    t = pl.program_id(2)

    @pl.when(t == 0)
    def _():
        s_ref[...] = jnp.zeros_like(s_ref)

    cid = jnp.where(d == 1, t, jnp.where(t == 0, 0, 1 + n_lat_chunks - t))
    outs = []
    for h in range(RET_HEADS):
        q = q_ref[:, h * RET_DK:(h + 1) * RET_DK]
        k = k_ref[:, h * RET_DK:(h + 1) * RET_DK]
        v = v_ref[:, h * RET_DV:(h + 1) * RET_DV]
        state = s_ref[h]
        scores = lax.dot_general(q, k, (((1,), (1,)), ((), ())), preferred_element_type=F32)
        p = (scores * dm_ref[h]).astype(BF16)
        qd = (q.astype(F32) * qd_ref[h]).astype(BF16)
        outs.append(jnp.dot(p, v, preferred_element_type=F32)
                    + jnp.dot(qd, state.astype(BF16), preferred_element_type=F32))
        kd = (k.astype(F32) * kd_ref[h]).astype(BF16)
        s_ref[h] = state * cd_ref[h] + lax.dot_general(
            kd, v, (((0,), (0,)), ((), ())), preferred_element_type=F32)

    @pl.when(d == 0)
    def _():
        for h in range(RET_HEADS):
            ob_ref[cid, :, h * RET_DV:(h + 1) * RET_DV] = outs[h]

    @pl.when(d == 1)
    def _():
        for h in range(RET_HEADS):
            sl = slice(h * RET_DV, (h + 1) * RET_DV)
            tot = outs[h] + ob_ref[cid, :, sl]
            tot = tot * lax.rsqrt(jnp.mean(tot * tot, axis=-1, keepdims=True) + NORM_EPS)
            g = g_ref[:, sl].astype(F32)
            o_ref[:, sl] = (tot * (g * jax.nn.sigmoid(g))).astype(BF16)


def _retention(u, tabs, *, batch, seq, n_ctx):
    m = u.shape[0]
    assert n_ctx == CHUNK and seq % CHUNK == 0
    nc = seq // CHUNK
    lat_blocks = batch * nc
    dmat, qdec, kdec, cdec = tabs

    def rowblk(b, d, t):
        lat = b * nc + jnp.where(d == 1, t - 1, nc - t)
        return jnp.where(t == 0, lat_blocks + b, lat)

    def in_map(col):
        return lambda b, d, t: (rowblk(b, d, t), col)

    def out_map(b, d, t):
        return (jnp.where(d == 1, rowblk(b, 1, t), lat_blocks + b), 0)

    def tab_map(b, d, t):
        return (d, 0, 0, 0)

    qk_w = RET_HEADS * RET_DK
    v_w = RET_HEADS * RET_DV
    kern = functools.partial(_retention_kernel, n_lat_chunks=nc)
    return pl.pallas_call(
        kern,
        grid=(batch, 2, nc + 1),
        in_specs=[pl.BlockSpec((CHUNK, qk_w), in_map(RQ // qk_w)),
                  pl.BlockSpec((CHUNK, qk_w), in_map(RK // qk_w)),
                  pl.BlockSpec((CHUNK, v_w), in_map(RV // v_w)),
                  pl.BlockSpec((CHUNK, v_w), in_map(RG // v_w)),
                  pl.BlockSpec((None, RET_HEADS, CHUNK, CHUNK), tab_map),
                  pl.BlockSpec((None, RET_HEADS, CHUNK, RET_DK), tab_map),
                  pl.BlockSpec((None, RET_HEADS, CHUNK, RET_DK), tab_map),
                  pl.BlockSpec((None, RET_HEADS, 1, RET_DV), tab_map)],
        out_specs=pl.BlockSpec((CHUNK, v_w), out_map),
        out_shape=jax.ShapeDtypeStruct((m, v_w), BF16),
        scratch_shapes=[pltpu.VMEM((RET_HEADS, RET_DK, RET_DV), F32),
                        pltpu.VMEM((nc + 1, CHUNK, v_w), F32)],
        compiler_params=_cparams(("arbitrary", "arbitrary", "arbitrary")),
        name="retention",
    )(u, u, u, u, dmat, qdec, kdec, cdec)


def _retention_tables(decay_logit):
    lg = jax.nn.log_sigmoid(decay_logit.astype(F32))
    lf, lb = lg[0][:, None, None], lg[1][:, None, None]
    pos = jnp.arange(CHUNK, dtype=F32)
    diff = pos[:, None] - pos[None, :]
    d_f = jnp.where(diff >= 0, jnp.exp(jnp.where(diff >= 0, diff, 0.0) * lf), 0.0)
    d_b = jnp.where(diff < 0, jnp.exp(jnp.where(diff < 0, -diff, 0.0) * lb), 0.0)
    ones_k = jnp.ones((1, 1, RET_DK), F32)
    q_f = jnp.exp((pos + 1.0)[None, :, None] * lf) * ones_k
    q_b = jnp.exp((CHUNK - pos)[None, :, None] * lb) * ones_k
    k_f = jnp.exp((CHUNK - 1.0 - pos)[None, :, None] * lf) * ones_k
    k_b = jnp.exp(pos[None, :, None] * lb) * ones_k
    ones_v = jnp.ones((1, 1, RET_DV), F32)
    c_f = jnp.exp(CHUNK * lf) * ones_v
    c_b = jnp.exp(CHUNK * lb) * ones_v
    return (jnp.stack([d_b, d_f]), jnp.stack([q_b, q_f]), jnp.stack([k_b, k_f]),
            jnp.stack([c_b, c_f]))


def _global_attn_kernel(q_ref, kx_ref, vx_ref, kc_ref, vc_ref, o_ref, vxe_ref, vce_ref, *, n_lat_tiles):
    i = pl.program_id(2)
    nt = (((1,), (1,)), ((), ()))

    @pl.when(i == 0)
    def _():
        vxe_ref[:, 0:HEAD_DIM] = vx_ref[...]
        vxe_ref[:, HEAD_DIM:] = jnp.ones_like(vx_ref)
        vce_ref[:, 0:HEAD_DIM] = vc_ref[...]
        vce_ref[:, HEAD_DIM:] = jnp.ones_like(vc_ref)

    def scores(h):
        q = q_ref[:, h * HEAD_DIM:(h + 1) * HEAD_DIM]
        return (lax.dot_general(q, kc_ref[...], nt, preferred_element_type=F32),
                lax.dot_general(q, kx_ref[...], nt, preferred_element_type=F32))

    def store(h, pv):
        o_ref[:, h * HEAD_DIM:(h + 1) * HEAD_DIM] = (pv[:, :HEAD_DIM] / pv[:, HEAD_DIM:]).astype(BF16)

    @pl.when(i < n_lat_tiles)
    def _():
        nxt = scores(0)
        for h in range(GROUP):
            s_c, s_x = nxt
            if h + 1 < GROUP:
                nxt = scores(h + 1)
            mx = jnp.maximum(jnp.max(s_c, axis=-1, keepdims=True), jnp.max(s_x, axis=-1, keepdims=True))
            p_c = jnp.exp2((s_c - mx).astype(BF16))
            p_x = jnp.exp2((s_x - mx).astype(BF16))
            store(h, jnp.dot(p_c, vce_ref[...], preferred_element_type=F32)
                  + jnp.dot(p_x, vxe_ref[...], preferred_element_type=F32))

    @pl.when(i >= n_lat_tiles)
    def _():
        for h in range(GROUP):
            s_c = lax.dot_general(q_ref[:, h * HEAD_DIM:(h + 1) * HEAD_DIM], kc_ref[...], nt,
                                  preferred_element_type=F32)
            p_c = jnp.exp2((s_c - jnp.max(s_c, axis=-1, keepdims=True)).astype(BF16))
            store(h, jnp.dot(p_c, vce_ref[...], preferred_element_type=F32))


def _global_attention(u, *, batch, seq, n_ctx, with_ctx):
    m = u.shape[0]
    tq = GA_TQ
    nq = seq // tq
    n_ctx_tiles = n_ctx // tq
    lat_blocks = batch * nq
    gw = GROUP * HEAD_DIM

    def qrow(b, i):
        return jnp.where(i < nq, b * nq + i, lat_blocks + b * n_ctx_tiles + (i - nq))

    kern = functools.partial(_global_attn_kernel, n_lat_tiles=nq)
    return pl.pallas_call(
        kern,
        grid=(batch, GA_KV_HEADS, nq + (n_ctx_tiles if with_ctx else 0)),
        in_specs=[pl.BlockSpec((tq, gw), lambda b, kh, i: (qrow(b, i), GQ // gw + kh)),
                  pl.BlockSpec((seq, HEAD_DIM), lambda b, kh, i: (b, GK // HEAD_DIM + kh)),
                  pl.BlockSpec((seq, HEAD_DIM), lambda b, kh, i: (b, GV // HEAD_DIM + kh)),
                  pl.BlockSpec((n_ctx, HEAD_DIM),
                               lambda b, kh, i: (batch * seq // n_ctx + b, GK // HEAD_DIM + kh)),
                  pl.BlockSpec((n_ctx, HEAD_DIM),
                               lambda b, kh, i: (batch * seq // n_ctx + b, GV // HEAD_DIM + kh))],
        out_specs=pl.BlockSpec((tq, gw), lambda b, kh, i: (qrow(b, i), kh)),
        out_shape=jax.ShapeDtypeStruct((m, GA_HEADS * HEAD_DIM), BF16),
        scratch_shapes=[pltpu.VMEM((seq, 2 * HEAD_DIM), BF16), pltpu.VMEM((n_ctx, 2 * HEAD_DIM), BF16)],
        compiler_params=_cparams(("arbitrary", "arbitrary", "arbitrary")),
        name="global_attention",
    )(u, u, u, u, u)


def _window_attn_kernel(sink_ref, q_ref, kx_ref, vx_ref, kc_ref, vc_ref, bias_ref, o_ref, vce_ref,
                        *, n_lat_tiles, seq):
    kh = pl.program_id(1)
    i = pl.program_id(2)
    span = CHUNK + 2 * WINDOW
    nt = (((1,), (1,)), ((), ()))

    @pl.when(i == 0)
    def _():
        vce_ref[:, 0:HEAD_DIM] = vc_ref[...]
        vce_ref[:, HEAD_DIM:] = jnp.ones_like(vc_ref)

    start = jnp.clip(i * CHUNK - WINDOW, 0, seq - span)
    start = pl.multiple_of(jnp.where(i < n_lat_tiles, start, 0), WINDOW)
    kw = kx_ref[pl.ds(start, span), :]
    vw = vx_ref[pl.ds(start, span), :]
    vwe = jnp.concatenate([vw, jnp.ones_like(vw)], axis=1)
    for h in range(GROUP):
        q = q_ref[:, h * HEAD_DIM:(h + 1) * HEAD_DIM]
        s_w = lax.dot_general(q, kw, nt, preferred_element_type=F32) + bias_ref[...]
        s_c = lax.dot_general(q, kc_ref[...], nt, preferred_element_type=F32)
        sink = sink_ref[kh * GROUP + h]
        mx = jnp.maximum(jnp.maximum(jnp.max(s_w, axis=-1, keepdims=True),
                                     jnp.max(s_c, axis=-1, keepdims=True)), sink)
        p_w = jnp.exp2((s_w - mx).astype(BF16))
        p_c = jnp.exp2((s_c - mx).astype(BF16))
        pv = (jnp.dot(p_w, vwe, preferred_element_type=F32)
              + jnp.dot(p_c, vce_ref[...], preferred_element_type=F32))
        den = pv[:, HEAD_DIM:] + jnp.exp2(sink - mx)
        o_ref[:, h * HEAD_DIM:(h + 1) * HEAD_DIM] = (pv[:, :HEAD_DIM] / den).astype(BF16)


def _window_bias():
    span = CHUNK + 2 * WINDOW
    r = jnp.arange(CHUNK)[:, None]
    col = jnp.arange(span)[None, :]
    offs = (0, WINDOW, 2 * WINDOW)
    tiles = [jnp.where(jnp.abs(col - off - r) <= WINDOW, 0.0, MASK_BIAS) for off in offs]
    tiles.append(jnp.full((CHUNK, span), MASK_BIAS))
    return jnp.stack(tiles).astype(F32)


def _window_attention(u, sink, *, batch, seq, n_ctx, with_ctx):
    m = u.shape[0]
    nq = seq // CHUNK
    lat_blocks = batch * nq
    gw = GROUP * HEAD_DIM
    span = CHUNK + 2 * WINDOW
    assert seq >= span and nq >= 2

    def qrow(b, i):
        return jnp.where(i < nq, b * nq + i, lat_blocks + b)

    def variant(i):
        return jnp.where(i == 0, 0, jnp.where(i < nq - 1, 1, jnp.where(i == nq - 1, 2, 3)))

    kern = functools.partial(_window_attn_kernel, n_lat_tiles=nq, seq=seq)
    return pl.pallas_call(
        kern,
        grid=(batch, WA_KV_HEADS, nq + (1 if with_ctx else 0)),
        in_specs=[pl.BlockSpec(memory_space=pltpu.SMEM),
                  pl.BlockSpec((CHUNK, gw), lambda b, kh, i: (qrow(b, i), WQ // gw + kh)),
                  pl.BlockSpec((seq, HEAD_DIM), lambda b, kh, i: (b, WK // HEAD_DIM + kh)),
                  pl.BlockSpec((seq, HEAD_DIM), lambda b, kh, i: (b, WV // HEAD_DIM + kh)),
                  pl.BlockSpec((n_ctx, HEAD_DIM),
                               lambda b, kh, i: (batch * seq // n_ctx + b, WK // HEAD_DIM + kh)),
                  pl.BlockSpec((n_ctx, HEAD_DIM),
                               lambda b, kh, i: (batch * seq // n_ctx + b, WV // HEAD_DIM + kh)),
                  pl.BlockSpec((None, CHUNK, span), lambda b, kh, i: (variant(i), 0, 0))],
        out_specs=pl.BlockSpec((CHUNK, gw), lambda b, kh, i: (qrow(b, i), kh)),
        out_shape=jax.ShapeDtypeStruct((m, WA_HEADS * HEAD_DIM), BF16),
        scratch_shapes=[pltpu.VMEM((n_ctx, 2 * HEAD_DIM), BF16)],
        compiler_params=_cparams(("arbitrary", "arbitrary", "arbitrary")),
        name="window_attention",
    )(sink.astype(F32) * LOG2_E, u, u, u, u, u, _window_bias())


def _merge_kernel(o0_ref, o1_ref, o2_ref, g0_ref, g1_ref, g2_ref, w0_ref, w1_ref, w2_ref, out_ref):
    tot = None
    for o_ref, g_ref, w_ref in ((o0_ref, g0_ref, w0_ref), (o1_ref, g1_ref, w1_ref),
                                (o2_ref, g2_ref, w2_ref)):
        term = jax.nn.sigmoid(g_ref[...].astype(F32)) * jnp.dot(
            o_ref[...], w_ref[...], preferred_element_type=F32)
        tot = term if tot is None else tot + term
    out_ref[...] = tot.astype(BF16)


def _merge(o_ret, o_ga, o_wa, u, w_branch, *, rows, tm, d):
    tn = 512
    bw = w_branch.shape[1]
    gate_blk = GATE // tn
    nd = d // tn

    def o_spec():
        return pl.BlockSpec((tm, bw), lambda i, j: (i, 0))

    def g_spec(br):
        return pl.BlockSpec((tm, tn), lambda i, j: (i, gate_blk + br * nd + j))

    def w_spec(br):
        return pl.BlockSpec((None, bw, tn), lambda i, j: (br, 0, j))

    return pl.pallas_call(
        _merge_kernel,
        grid=(rows // tm, nd),
        in_specs=[o_spec(), o_spec(), o_spec(), g_spec(0), g_spec(1), g_spec(2),
                  w_spec(0), w_spec(1), w_spec(2)],
        out_specs=pl.BlockSpec((tm, tn), lambda i, j: (i, j)),
        out_shape=jax.ShapeDtypeStruct((rows, d), BF16),
        compiler_params=_cparams(("arbitrary", "arbitrary")),
        name="branch_merge",
    )(o_ret, o_ga, o_wa, u, u, u, w_branch, w_branch, w_branch)


def _out_proj_kernel(m_ref, w_ref, x_ref, g_ref, o_ref):
    o_ref[...] = x_ref[...] + g_ref[...] * jnp.dot(m_ref[...], w_ref[...], preferred_element_type=F32)


def _out_proj(mix, w_out, x_all, mods3, *, rows, tm, d, mod_idx):
    tn = 512
    nd = d // tn
    return pl.pallas_call(
        _out_proj_kernel,
        grid=(rows // tm, nd),
        in_specs=[pl.BlockSpec((tm, d), lambda i, j: (i, 0)),
                  pl.BlockSpec((d, tn), lambda i, j: (0, j)),
                  pl.BlockSpec((tm, tn), lambda i, j: (i, j)),
                  pl.BlockSpec((None, 1, tn), lambda i, j: (mod_idx(i), 0, 2 * nd + j))],
        out_specs=pl.BlockSpec((tm, tn), lambda i, j: (i, j)),
        out_shape=jax.ShapeDtypeStruct((rows, d), F32),
        compiler_params=_cparams(("arbitrary", "arbitrary")),
        name="out_proj",
    )(mix, w_out, x_all, mods3)


def _router_kernel(x_ref, sh_ref, sc_ref, g_ref, wr_ref, br_ref, h_ref, idx_ref, wt_ref, *, n_experts):
    x = x_ref[...]
    y = x * lax.rsqrt(jnp.mean(x * x, axis=-1, keepdims=True) + NORM_EPS) * g_ref[...]
    h = y * (1.0 + sc_ref[...]) + sh_ref[...]
    h_hi = h.astype(BF16)
    h_ref[...] = h_hi
    h_lo = (h - h_hi.astype(F32)).astype(BF16)
    w = wr_ref[...]
    w_hi = w.astype(BF16)
    w_lo = (w - w_hi.astype(F32)).astype(BF16)
    logits = (jnp.dot(h_hi, w_hi, preferred_element_type=F32)
              + jnp.dot(h_lo, w_hi, preferred_element_type=F32)
              + jnp.dot(h_hi, w_lo, preferred_element_type=F32)) + br_ref[...]
    lane = lax.broadcasted_iota(jnp.int32, logits.shape, 1).astype(F32)
    vals, ids = [], []
    cur = logits
    for _ in range(TOP_K):
        mx = jnp.max(cur, axis=-1, keepdims=True)
        sel = jnp.min(jnp.where(cur == mx, lane, float(n_experts)), axis=-1, keepdims=True)
        vals.append(mx)
        ids.append(sel)
        cur = jnp.where(lane == sel, -jnp.inf, cur)
    e = [jnp.exp(v - vals[0]) for v in vals]
    den = e[0] + e[1] + e[2] + e[3]
    for k in range(TOP_K):
        idx_ref[:, k:k + 1] = ids[k].astype(jnp.int32)
        wt_ref[:, k:k + 1] = e[k] / den


def _router(x_new, mods3, ln_g, w_router, b_router, *, rows, tm, d, mod_idx):
    n_experts = w_router.shape[1]
    kern = functools.partial(_router_kernel, n_experts=n_experts)
    return pl.pallas_call(
        kern,
        grid=(rows // tm,),
        in_specs=[pl.BlockSpec((tm, d), lambda i: (i, 0)),
                  pl.BlockSpec((None, 1, d), lambda i: (mod_idx(i), 0, 3)),
                  pl.BlockSpec((None, 1, d), lambda i: (mod_idx(i), 0, 4)),
                  pl.BlockSpec((1, d), lambda i: (0, 0)),
                  pl.BlockSpec((d, n_experts), lambda i: (0, 0)),
                  pl.BlockSpec((1, n_experts), lambda i: (0, 0))],
        out_specs=[pl.BlockSpec((tm, d), lambda i: (i, 0)),
                   pl.BlockSpec((tm, TOP_K), lambda i: (i, 0)),
                   pl.BlockSpec((tm, TOP_K), lambda i: (i, 0))],
        out_shape=[jax.ShapeDtypeStruct((rows, d), BF16),
                   jax.ShapeDtypeStruct((rows, TOP_K), jnp.int32),
                   jax.ShapeDtypeStruct((rows, TOP_K), F32)],
        compiler_params=_cparams(("arbitrary",)),
        name="norm2_router",
    )(x_new, mods3, mods3, ln_g.reshape(1, d), w_router, b_router.reshape(1, n_experts))


def _moe_up_kernel(ie_ref, ic_ref, it_ref, in_ref, x_ref, wg_ref, wl_ref, bg_ref, bl_ref, o_ref):
    i = pl.program_id(0)

    def compute(n):
        x = x_ref[0:n, :]
        glu = jnp.dot(x, wg_ref[...].astype(BF16), preferred_element_type=F32) + bg_ref[...]
        lin = jnp.dot(x, wl_ref[...].astype(BF16), preferred_element_type=F32) + bl_ref[...]
        glu = jnp.minimum(glu, SWIGLU_LIMIT)
        lin = jnp.clip(lin, -SWIGLU_LIMIT, SWIGLU_LIMIT)
        o_ref[0:n, :] = (glu * jax.nn.sigmoid(SWIGLU_ALPHA * glu) * (lin + 1.0)).astype(BF16)

    tm = x_ref.shape[0]
    for quarters in range(1, MOE_TILE_PARTS + 1):
        n = quarters * tm // MOE_TILE_PARTS
        pl.when(in_ref[i] == n)(functools.partial(compute, n))


def _moe_up(sched, xs, w_gu, b_gu, *, layer, tm, tf):
    r_pad, d = xs.shape
    n_layers, n_experts, _, f2 = w_gu.shape
    f = f2 // 2
    nc = f // tf
    n_items = sched[0].shape[0]
    grid_spec = pltpu.PrefetchScalarGridSpec(
        num_scalar_prefetch=4,
        grid=(n_items,),
        in_specs=[pl.BlockSpec((tm, d), lambda i, ie, ic, it, nr: (it[i], 0)),
                  pl.BlockSpec((None, None, d, tf), lambda i, ie, ic, it, nr: (layer, ie[i], 0, ic[i])),
                  pl.BlockSpec((None, None, d, tf),
                               lambda i, ie, ic, it, nr: (layer, ie[i], 0, nc + ic[i])),
                  pl.BlockSpec((None, None, 1, tf), lambda i, ie, ic, it, nr: (layer, ie[i], 0, ic[i])),
                  pl.BlockSpec((None, None, 1, tf),
                               lambda i, ie, ic, it, nr: (layer, ie[i], 0, nc + ic[i]))],
        out_specs=pl.BlockSpec((tm, tf), lambda i, ie, ic, it, nr: (it[i], ic[i])))
    b4 = b_gu.reshape(n_layers, n_experts, 1, f2)
    return pl.pallas_call(
        _moe_up_kernel,
        grid_spec=grid_spec,
        out_shape=jax.ShapeDtypeStruct((r_pad, f), BF16),
        compiler_params=_cparams(("arbitrary",)),
        name="moe_up",
    )(*sched, xs, w_gu, w_gu, b4, b4)


def _moe_down_kernel(ie_ref, ic_ref, it_ref, in_ref, a_ref, w_ref, b_ref, o_ref):
    i = pl.program_id(0)

    def compute(n):
        y = jnp.dot(a_ref[0:n, :], w_ref[...].astype(BF16), preferred_element_type=F32) + b_ref[...]
        o_ref[0:n, :] = y.astype(BF16)

    tm = a_ref.shape[0]
    for quarters in range(1, MOE_TILE_PARTS + 1):
        n = quarters * tm // MOE_TILE_PARTS
        pl.when(in_ref[i] == n)(functools.partial(compute, n))


def _moe_down(sched, act, w_down, b_down, *, layer, tm, tn):
    r_pad, f = act.shape
    n_layers, n_experts, _, d = w_down.shape
    n_items = sched[0].shape[0]
    grid_spec = pltpu.PrefetchScalarGridSpec(
        num_scalar_prefetch=4,
        grid=(n_items,),
        in_specs=[pl.BlockSpec((tm, f), lambda i, ie, ic, it, nr: (it[i], 0)),
                  pl.BlockSpec((None, None, f, tn), lambda i, ie, ic, it, nr: (layer, ie[i], 0, ic[i])),
                  pl.BlockSpec((None, None, 1, tn), lambda i, ie, ic, it, nr: (layer, ie[i], 0, ic[i]))],
        out_specs=pl.BlockSpec((tm, tn), lambda i, ie, ic, it, nr: (it[i], ic[i])))
    return pl.pallas_call(
        _moe_down_kernel,
        grid_spec=grid_spec,
        out_shape=jax.ShapeDtypeStruct((r_pad, d), BF16),
        compiler_params=_cparams(("arbitrary",)),
        name="moe_down",
    )(*sched, act, w_down, b_down.reshape(n_layers, n_experts, 1, d))


def _moe_schedule(idx, *, n_experts, tm, n_chunks):
    n_tok = idx.shape[0]
    n_pairs = n_tok * TOP_K
    r_pad = n_pairs + n_experts * tm
    n_tiles = r_pad // tm
    part = tm // MOE_TILE_PARTS
    flat_e = idx.reshape(-1)
    experts = jnp.arange(n_experts, dtype=jnp.int32)
    onehot = (flat_e[:, None] == experts[None, :]).astype(jnp.int32)
    csum = jnp.cumsum(onehot, axis=0)
    rank = jnp.sum(csum * onehot, axis=1) - 1
    counts = csum[-1]
    ntiles = (counts + tm - 1) // tm
    tile_end = jnp.cumsum(ntiles)
    tile_start = tile_end - ntiles
    dest = jnp.sum(onehot * tile_start[None, :], axis=1) * tm + rank
    src_tok = (jnp.arange(r_pad, dtype=jnp.int32) % n_tok).at[dest].set(
        jnp.arange(n_pairs, dtype=jnp.int32) // TOP_K, unique_indices=True, mode="promise_in_bounds")
    n_used = tile_end[-1]

    def schedule(n_chunks):
        n_items = n_chunks * n_tiles
        item = jnp.arange(n_items, dtype=jnp.int32)
        valid = item < n_chunks * n_used
        ic_ = jnp.minimum(item, jnp.maximum(n_chunks * n_used - 1, 0))
        e = jnp.minimum(
            jnp.sum((ic_[:, None] >= n_chunks * tile_end[None, :]).astype(jnp.int32), axis=1),
            n_experts - 1)
        sel = (e[:, None] == experts[None, :]).astype(jnp.int32)
        nt_e = jnp.maximum(jnp.sum(sel * ntiles[None, :], axis=1), 1)
        ts_e = jnp.sum(sel * tile_start[None, :], axis=1)
        cnt_e = jnp.sum(sel * counts[None, :], axis=1)
        r = ic_ - n_chunks * ts_e
        c = r // nt_e
        lt = nt_e - 1 - r % nt_e
        t = ts_e + lt
        left = cnt_e - lt * tm
        nrows = jnp.where(valid, jnp.minimum((left + part - 1) // part * part, tm), 0)
        return (e.astype(jnp.int32), c.astype(jnp.int32), t.astype(jnp.int32),
                nrows.astype(jnp.int32))

    pos = dest.reshape(n_tok, TOP_K).T.reshape(-1)
    return [schedule(n) for n in n_chunks], src_tok, pos


def _combine_kernel(x_ref, y0_ref, y1_ref, y2_ref, y3_ref, w_ref, g_ref, fg_ref, o_ref, *, final):
    w = w_ref[...]
    y = None
    for k, y_ref in enumerate((y0_ref, y1_ref, y2_ref, y3_ref)):
        term = w[:, k:k + 1] * y_ref[...].astype(F32)
        y = term if y is None else y + term
    x = x_ref[...] + g_ref[...] * y
    if final:
        x = x * lax.rsqrt(jnp.mean(x * x, axis=-1, keepdims=True) + NORM_EPS) * fg_ref[...]
    o_ref[...] = x


def _combine(x_new, ysg, wts, mods3, final_g, *, rows, tm, d, mod_idx, final):
    assert TOP_K == 4
    kern = functools.partial(_combine_kernel, final=final)
    nblk = rows // tm

    def y_spec(k):
        return pl.BlockSpec((tm, d), lambda i: (k * nblk + i, 0))

    return pl.pallas_call(
        kern,
        grid=(nblk,),
        in_specs=[pl.BlockSpec((tm, d), lambda i: (i, 0)),
                  y_spec(0), y_spec(1), y_spec(2), y_spec(3),
                  pl.BlockSpec((tm, TOP_K), lambda i: (i, 0)),
                  pl.BlockSpec((None, 1, d), lambda i: (mod_idx(i), 0, 5)),
                  pl.BlockSpec((1, d), lambda i: (0, 0))],
        out_specs=pl.BlockSpec((tm, d), lambda i: (i, 0)),
        out_shape=jax.ShapeDtypeStruct((rows, d), F32),
        compiler_params=_cparams(("arbitrary",)),
        name="moe_combine",
    )(x_new, ysg, ysg, ysg, ysg, wts, mods3, final_g.reshape(1, d))


def _rope_tables(batch, seq, n_ctx):
    rows = seq // GRID_W
    row = jnp.repeat(jnp.arange(rows, dtype=F32), GRID_W)
    col = jnp.tile(jnp.arange(GRID_W, dtype=F32), rows)
    quarter = HEAD_DIM // 4
    inv_freq = ROPE_THETA ** (-jnp.arange(quarter, dtype=F32) / quarter)
    ang = jnp.concatenate([row[:, None] * inv_freq, col[:, None] * inv_freq], axis=-1)
    cos, sin = jnp.cos(ang), jnp.sin(ang)
    cos_x = jnp.tile(jnp.concatenate([cos, cos], axis=-1), (batch, 1))
    sin_x = jnp.tile(jnp.concatenate([-sin, sin], axis=-1), (batch, 1))
    cos_f = jnp.concatenate([cos_x, jnp.ones((batch * n_ctx, HEAD_DIM), F32)], axis=0)
    sin_f = jnp.concatenate([sin_x, jnp.zeros((batch * n_ctx, HEAD_DIM), F32)], axis=0)
    return cos_f, sin_f


def _column_params(q_gain, k_gain, ncol):
    hp = _head_perm()
    scale = HEAD_DIM ** -0.5
    k_scale = RET_DK ** -0.5
    cs = jnp.ones((ncol,), F32)
    cs = cs.at[RK:RK + RET_HEADS * RET_DK].set(k_scale)
    cs = cs.at[GQ:GQ + GA_HEADS * HEAD_DIM].set(jnp.tile(q_gain.astype(F32)[hp] * (scale * LOG2_E), GA_HEADS))
    cs = cs.at[GK:GK + GA_KV_HEADS * HEAD_DIM].set(jnp.tile(k_gain.astype(F32)[hp], GA_KV_HEADS))
    cs = cs.at[WQ:WQ + WA_HEADS * HEAD_DIM].set(scale * LOG2_E)
    rm = jnp.zeros((ncol,), F32).at[GQ:GQ + GA_HEADS * HEAD_DIM].set(1.0)
    rm = rm.at[GK:GK + GA_KV_HEADS * HEAD_DIM].set(1.0)
    return cs, rm


def kernel(x, c, ctx, c_ctx, ln1_g, ln2_g, w_ada, b_ada, w_in, ret_decay_logit, ga_q_gain,
           ga_k_gain, wa_sink, w_branch, w_out, w_router, b_router, w_gu, b_gu, w_down, b_down,
           final_g):
    batch, seq, d = x.shape
    n_ctx = ctx.shape[1]
    depth = w_ada.shape[0]
    n_experts = w_router.shape[2]
    ncol = w_in.shape[2]
    mx, mc = batch * seq, batch * n_ctx
    m = mx + mc
    tm = min(1024, mc)
    assert seq % tm == 0 and mc % tm == 0 and batch + 1 <= 8
    tm_stream = min(256, tm)
    assert w_gu.shape[3] // 2 == w_down.shape[3] == d

    def mod_idx_for(tile):
        def mod_idx(i):
            return jnp.where(i < mx // tile, i // (seq // tile), batch)
        return mod_idx

    mod_idx = mod_idx_for(tm)
    mod_idx_s = mod_idx_for(tm_stream)

    cc8 = jnp.zeros((8, d), F32).at[:batch].set(c).at[batch].set(c_ctx)
    mods = _modulation(cc8, w_ada, b_ada)
    cos_f, sin_f = _rope_tables(batch, seq, n_ctx)

    x_all = jnp.concatenate([x.reshape(mx, d), ctx.reshape(mc, d)], axis=0)
    e_tm = 512
    tf = min(1024, d)

    for l in range(depth):
        need_ctx = l < depth - 1
        rows = m if need_ctx else mx
        mods3 = mods[l].reshape(8, 1, 6 * d)
        w_in_p = _regroup_columns(w_in[l])
        colscale, rmsmask = _column_params(ga_q_gain[l], ga_k_gain[l], ncol)
        h1 = _prenorm(x_all, mods3, ln1_g[l], tm=tm_stream, mod_idx=mod_idx_s)
        u = _in_proj(h1, w_in_p, cos_f, sin_f, colscale, rmsmask)
        o_ret = _retention(u, _retention_tables(ret_decay_logit[l]), batch=batch, seq=seq, n_ctx=n_ctx)
        o_ga = _global_attention(u, batch=batch, seq=seq, n_ctx=n_ctx, with_ctx=need_ctx)
        o_wa = _window_attention(u, wa_sink[l], batch=batch, seq=seq, n_ctx=n_ctx, with_ctx=need_ctx)
        mix = _merge(o_ret, o_ga, o_wa, u, w_branch[l].astype(BF16), rows=rows, tm=tm, d=d)
        x_new = _out_proj(mix, w_out[l].astype(BF16), x_all, mods3, rows=rows, tm=tm, d=d, mod_idx=mod_idx)
        h2, idx, wts = _router(x_new, mods3, ln2_g[l], w_router[l], b_router[l],
                               rows=rows, tm=tm_stream, d=d, mod_idx=mod_idx_s)
        (sched_up, sched_down), src_tok, pos = _moe_schedule(
            idx, n_experts=n_experts, tm=e_tm, n_chunks=(d // tf, 1))
        xs = h2.at[src_tok].get(mode="promise_in_bounds")
        act = _moe_up(sched_up, xs, w_gu, b_gu, layer=l, tm=e_tm, tf=tf)
        ys = _moe_down(sched_down, act, w_down, b_down, layer=l, tm=e_tm, tn=d)
        ysg = ys.at[pos].get(mode="promise_in_bounds")
        x_all = _combine(x_new, ysg, wts, mods3, final_g, rows=rows, tm=tm_stream, d=d,
                         mod_idx=mod_idx_s, final=not need_ctx)
    return x_all[:mx].reshape(batch, seq, d)
```

```python
import functools

import jax
import jax.numpy as jnp
from jax import lax
from jax.experimental import pallas as pl
from jax.experimental.pallas import tpu as pltpu

GRID_W = 64
HEAD_DIM = 128
RET_HEADS = 4
RET_DK = 128
RET_DV = 256
GA_HEADS = 8
GA_KV_HEADS = 2
WA_HEADS = 8
WA_KV_HEADS = 2
WINDOW = 128
N_BRANCHES = 3
TOP_K = 4
SWIGLU_LIMIT = 7.0
SWIGLU_ALPHA = 1.702
ROPE_THETA = 10000.0
NORM_EPS = 1e-6
LOG2_E = 1.4426950408889634
MASK_BIAS = -1e30

LANES = 128
CHUNK = 256
GROUP = GA_HEADS // GA_KV_HEADS
IN_PROJ_MAX_ROWS = 1152
MOE_TILE_PARTS = 4
GA_TQ = 256
VMEM_LIMIT = 56 * 1024 * 1024

RQ, RK, GQ, WQ, GK, WK = 0, 512, 1024, 2048, 3072, 3328
ROPE_COLS = 3584
GV, WV, RV, RG, GATE = 3584, 3840, 4096, 5120, 6144
_ORQ, _ORK, _ORV, _ORG, _OGQ, _OGK, _OGV, _OWQ, _OWK, _OWV, _OGATE = (
    0, 512, 1024, 2048, 3072, 4096, 4352, 4608, 5632, 5888, 6144)

F32 = jnp.float32
BF16 = jnp.bfloat16


def _cparams(sem, vmem=VMEM_LIMIT):
    return pltpu.CompilerParams(dimension_semantics=sem, vmem_limit_bytes=vmem)


def _head_perm():
    j = jnp.arange(HEAD_DIM)
    return jnp.where(j < HEAD_DIM // 2, 2 * j, 2 * (j - HEAD_DIM // 2) + 1)


def _regroup_columns(w):
    d_model = w.shape[0]
    half = HEAD_DIM // 2

    def rope_heads(start, n_heads):
        blk = w[:, start:start + n_heads * HEAD_DIM].reshape(d_model, n_heads, half, 2)
        return blk.transpose(0, 1, 3, 2).reshape(d_model, n_heads * HEAD_DIM)

    def plain(start, size):
        return w[:, start:start + size]

    return jnp.concatenate([
        rope_heads(_ORQ, RET_HEADS), rope_heads(_ORK, RET_HEADS),
        rope_heads(_OGQ, GA_HEADS), rope_heads(_OWQ, WA_HEADS),
        rope_heads(_OGK, GA_KV_HEADS), rope_heads(_OWK, WA_KV_HEADS),
        plain(_OGV, GA_KV_HEADS * HEAD_DIM), plain(_OWV, WA_KV_HEADS * HEAD_DIM),
        plain(_ORV, RET_HEADS * RET_DV), plain(_ORG, RET_HEADS * RET_DV),
        plain(_OGATE, w.shape[1] - _OGATE)], axis=1).astype(BF16)


def _mod_kernel(c_ref, w_ref, b_ref, o_ref):
    c = c_ref[...]
    s = (c * jax.nn.sigmoid(c)).astype(BF16)
    o_ref[...] = jnp.dot(s, w_ref[...].astype(BF16), preferred_element_type=F32) + b_ref[...]


def _modulation(cc8, w_ada, b_ada):
    n_layers, d, n = w_ada.shape
    tn = min(1024, n)
    return pl.pallas_call(
        _mod_kernel,
        grid=(n_layers, n // tn),
        in_specs=[pl.BlockSpec((8, d), lambda l, j: (0, 0)),
                  pl.BlockSpec((None, d, tn), lambda l, j: (l, 0, j)),
                  pl.BlockSpec((None, 1, tn), lambda l, j: (l, 0, j))],
        out_specs=pl.BlockSpec((None, 8, tn), lambda l, j: (l, 0, j)),
        out_shape=jax.ShapeDtypeStruct((n_layers, 8, n), F32),
        compiler_params=_cparams(("arbitrary", "arbitrary")),
        name="modulation",
    )(cc8, w_ada, b_ada.reshape(n_layers, 1, n))


def _prenorm_kernel(x_ref, sh_ref, sc_ref, g_ref, h_ref):
    x = x_ref[...]
    y = x * lax.rsqrt(jnp.mean(x * x, axis=-1, keepdims=True) + NORM_EPS) * g_ref[...]
    h_ref[...] = (y * (1.0 + sc_ref[...]) + sh_ref[...]).astype(BF16)


def _prenorm(x_all, mods3, ln_g, *, tm, mod_idx):
    m, d = x_all.shape
    return pl.pallas_call(
        _prenorm_kernel,
        grid=(m // tm,),
        in_specs=[pl.BlockSpec((tm, d), lambda i: (i, 0)),
                  pl.BlockSpec((None, 1, d), lambda i: (mod_idx(i), 0, 0)),
                  pl.BlockSpec((None, 1, d), lambda i: (mod_idx(i), 0, 1)),
                  pl.BlockSpec((1, d), lambda i: (0, 0))],
        out_specs=pl.BlockSpec((tm, d), lambda i: (i, 0)),
        out_shape=jax.ShapeDtypeStruct((m, d), BF16),
        compiler_params=_cparams(("arbitrary",)),
        name="prenorm",
    )(x_all, mods3, mods3, ln_g.reshape(1, d))


def _in_proj_kernel(h_ref, w_ref, cos_ref, sin_ref, cs_ref, rm_ref, o_ref, acc_ref, *, n_tiles, tn):
    s = pl.program_id(0)

    @pl.when(s == 0)
    def _():
        acc_ref[1] = jnp.zeros(acc_ref.shape[1:], F32)

    cur = s % 2
    prev = acc_ref[1 - cur]
    col0 = (jnp.maximum(s - 1, 0) % n_tiles) * tn
    cos = cos_ref[...]
    sin = sin_ref[...]
    for hh in range(tn // HEAD_DIM):
        sl = slice(hh * HEAD_DIM, (hh + 1) * HEAD_DIM)
        raw = prev[:, sl]
        r = lax.rsqrt(jnp.mean(raw * raw, axis=-1, keepdims=True) + NORM_EPS)
        rm = rm_ref[:, sl]
        t = raw * (rm * r + (1.0 - rm)) * cs_ref[:, sl]
        roped = t * cos + pltpu.roll(t, HEAD_DIM // 2, 1) * sin
        is_rope = col0 + hh * HEAD_DIM < ROPE_COLS
        o_ref[:, sl] = jnp.where(is_rope, roped, raw).astype(BF16)
    acc_ref[cur] = jnp.dot(h_ref[...], w_ref[...], preferred_element_type=F32)


def _in_proj(h, w_in_p, cos_f, sin_f, colscale, rmsmask):
    m, d = h.shape
    ncol = w_in_p.shape[1]
    tn = 512
    tm = max(t for t in range(LANES, IN_PROJ_MAX_ROWS + 1, LANES) if m % t == 0)
    n_tiles = ncol // tn
    n_rows = m // tm
    n_steps = n_rows * n_tiles + 1

    def row(s):
        return jnp.minimum(s // n_tiles, n_rows - 1)

    def col(s):
        return jnp.where(s < n_steps - 1, s % n_tiles, n_tiles - 1)

    def prow(s):
        return jnp.maximum(s - 1, 0) // n_tiles

    def pcol(s):
        return jnp.maximum(s - 1, 0) % n_tiles

    kern = functools.partial(_in_proj_kernel, n_tiles=n_tiles, tn=tn)
    return pl.pallas_call(
        kern,
        grid=(n_steps,),
        in_specs=[pl.BlockSpec((tm, d), lambda s: (row(s), 0)),
                  pl.BlockSpec((d, tn), lambda s: (0, col(s))),
                  pl.BlockSpec((tm, HEAD_DIM), lambda s: (prow(s), 0)),
                  pl.BlockSpec((tm, HEAD_DIM), lambda s: (prow(s), 0)),
                  pl.BlockSpec((1, tn), lambda s: (0, pcol(s))),
                  pl.BlockSpec((1, tn), lambda s: (0, pcol(s)))],
        out_specs=pl.BlockSpec((tm, tn), lambda s: (prow(s), pcol(s))),
        out_shape=jax.ShapeDtypeStruct((m, ncol), BF16),
        scratch_shapes=[pltpu.VMEM((2, tm, tn), F32)],
        compiler_params=_cparams(("arbitrary",)),
        name="in_proj",
    )(h, w_in_p, cos_f, sin_f, colscale.reshape(1, ncol), rmsmask.reshape(1, ncol))


def _retention_kernel(q_ref, k_ref, v_ref, g_ref, dm_ref, qd_ref, kd_ref, cd_ref,
                      o_ref, s_ref, ob_ref, *, n_lat_chunks):
    d = pl.program_id(1)
    t = pl.program_id(2)

    @pl.when(t == 0)
    def _():
        s_ref[...] = jnp.zeros_like(s_ref)

    cid = jnp.where(d == 1, t, jnp.where(t == 0, 0, 1 + n_lat_chunks - t))
    outs = []
    for h in range(RET_HEADS):
        q = q_ref[:, h * RET_DK:(h + 1) * RET_DK]
        k = k_ref[:, h * RET_DK:(h + 1) * RET_DK]
        v = v_ref[:, h * RET_DV:(h + 1) * RET_DV]
        state = s_ref[h]
        scores = lax.dot_general(q, k, (((1,), (1,)), ((), ())), preferred_element_type=F32)
        p = (scores * dm_ref[h]).astype(BF16)
        qd = (q.astype(F32) * qd_ref[h]).astype(BF16)
        outs.append(jnp.dot(p, v, preferred_element_type=F32)
                    + jnp.dot(qd, state.astype(BF16), preferred_element_type=F32))
        kd = (k.astype(F32) * kd_ref[h]).astype(BF16)
        s_ref[h] = state * cd_ref[h] + lax.dot_general(
            kd, v, (((0,), (0,)), ((), ())), preferred_element_type=F32)

    @pl.when(d == 0)
    def _():
        for h in range(RET_HEADS):
            ob_ref[cid, :, h * RET_DV:(h + 1) * RET_DV] = outs[h]

    @pl.when(d == 1)
    def _():
        for h in range(RET_HEADS):
            sl = slice(h * RET_DV, (h + 1) * RET_DV)
            tot = outs[h] + ob_ref[cid, :, sl]
            tot = tot * lax.rsqrt(jnp.mean(tot * tot, axis=-1, keepdims=True) + NORM_EPS)
            g = g_ref[:, sl].astype(F32)
            o_ref[:, sl] = (tot * (g * jax.nn.sigmoid(g))).astype(BF16)


def _retention(u, tabs, *, batch, seq, n_ctx):
    m = u.shape[0]
    assert n_ctx == CHUNK and seq % CHUNK == 0
    nc = seq // CHUNK
    lat_blocks = batch * nc
    dmat, qdec, kdec, cdec = tabs

    def rowblk(b, d, t):
        lat = b * nc + jnp.where(d == 1, t - 1, nc - t)
        return jnp.where(t == 0, lat_blocks + b, lat)

    def in_map(col):
        return lambda b, d, t: (rowblk(b, d, t), col)

    def out_map(b, d, t):
        return (jnp.where(d == 1, rowblk(b, 1, t), lat_blocks + b), 0)

    def tab_map(b, d, t):
        return (d, 0, 0, 0)

    qk_w = RET_HEADS * RET_DK
    v_w = RET_HEADS * RET_DV
    kern = functools.partial(_retention_kernel, n_lat_chunks=nc)
    return pl.pallas_call(
        kern,
        grid=(batch, 2, nc + 1),
        in_specs=[pl.BlockSpec((CHUNK, qk_w), in_map(RQ // qk_w)),
                  pl.BlockSpec((CHUNK, qk_w), in_map(RK // qk_w)),
                  pl.BlockSpec((CHUNK, v_w), in_map(RV // v_w)),
                  pl.BlockSpec((CHUNK, v_w), in_map(RG // v_w)),
                  pl.BlockSpec((None, RET_HEADS, CHUNK, CHUNK), tab_map),
                  pl.BlockSpec((None, RET_HEADS, CHUNK, RET_DK), tab_map),
                  pl.BlockSpec((None, RET_HEADS, CHUNK, RET_DK), tab_map),
                  pl.BlockSpec((None, RET_HEADS, 1, RET_DV), tab_map)],
        out_specs=pl.BlockSpec((CHUNK, v_w), out_map),
        out_shape=jax.ShapeDtypeStruct((m, v_w), BF16),
        scratch_shapes=[pltpu.VMEM((RET_HEADS, RET_DK, RET_DV), F32),
                        pltpu.VMEM((nc + 1, CHUNK, v_w), F32)],
        compiler_params=_cparams(("arbitrary", "arbitrary", "arbitrary")),
        name="retention",
    )(u, u, u, u, dmat, qdec, kdec, cdec)


def _retention_tables(decay_logit):
    lg = jax.nn.log_sigmoid(decay_logit.astype(F32))
    lf, lb = lg[0][:, None, None], lg[1][:, None, None]
    pos = jnp.arange(CHUNK, dtype=F32)
    diff = pos[:, None] - pos[None, :]
    d_f = jnp.where(diff >= 0, jnp.exp(jnp.where(diff >= 0, diff, 0.0) * lf), 0.0)
    d_b = jnp.where(diff < 0, jnp.exp(jnp.where(diff < 0, -diff, 0.0) * lb), 0.0)
    ones_k = jnp.ones((1, 1, RET_DK), F32)
    q_f = jnp.exp((pos + 1.0)[None, :, None] * lf) * ones_k
    q_b = jnp.exp((CHUNK - pos)[None, :, None] * lb) * ones_k
    k_f = jnp.exp((CHUNK - 1.0 - pos)[None, :, None] * lf) * ones_k
    k_b = jnp.exp(pos[None, :, None] * lb) * ones_k
    ones_v = jnp.ones((1, 1, RET_DV), F32)
    c_f = jnp.exp(CHUNK * lf) * ones_v
    c_b = jnp.exp(CHUNK * lb) * ones_v
    return (jnp.stack([d_b, d_f]), jnp.stack([q_b, q_f]), jnp.stack([k_b, k_f]),
            jnp.stack([c_b, c_f]))


def _global_attn_kernel(q_ref, kx_ref, vx_ref, kc_ref, vc_ref, o_ref, vxe_ref, vce_ref, *, n_lat_tiles):
    i = pl.program_id(2)
    nt = (((1,), (1,)), ((), ()))

    @pl.when(i == 0)
    def _():
        vxe_ref[:, 0:HEAD_DIM] = vx_ref[...]
        vxe_ref[:, HEAD_DIM:] = jnp.ones_like(vx_ref)
        vce_ref[:, 0:HEAD_DIM] = vc_ref[...]
        vce_ref[:, HEAD_DIM:] = jnp.ones_like(vc_ref)

    def scores(h):
        q = q_ref[:, h * HEAD_DIM:(h + 1) * HEAD_DIM]
        return (lax.dot_general(q, kc_ref[...], nt, preferred_element_type=F32),
                lax.dot_general(q, kx_ref[...], nt, preferred_element_type=F32))

    def store(h, pv):
        o_ref[:, h * HEAD_DIM:(h + 1) * HEAD_DIM] = (pv[:, :HEAD_DIM] / pv[:, HEAD_DIM:]).astype(BF16)

    @pl.when(i < n_lat_tiles)
    def _():
        nxt = scores(0)
        for h in range(GROUP):
            s_c, s_x = nxt
            if h + 1 < GROUP:
                nxt = scores(h + 1)
            mx = jnp.maximum(jnp.max(s_c, axis=-1, keepdims=True), jnp.max(s_x, axis=-1, keepdims=True))
            p_c = jnp.exp2((s_c - mx).astype(BF16))
            p_x = jnp.exp2((s_x - mx).astype(BF16))
            store(h, jnp.dot(p_c, vce_ref[...], preferred_element_type=F32)
                  + jnp.dot(p_x, vxe_ref[...], preferred_element_type=F32))

    @pl.when(i >= n_lat_tiles)
    def _():
        for h in range(GROUP):
            s_c = lax.dot_general(q_ref[:, h * HEAD_DIM:(h + 1) * HEAD_DIM], kc_ref[...], nt,
                                  preferred_element_type=F32)
            p_c = jnp.exp2((s_c - jnp.max(s_c, axis=-1, keepdims=True)).astype(BF16))
            store(h, jnp.dot(p_c, vce_ref[...], preferred_element_type=F32))


def _global_attention(u, *, batch, seq, n_ctx, with_ctx):
    m = u.shape[0]
    tq = GA_TQ
    nq = seq // tq
    n_ctx_tiles = n_ctx // tq
    lat_blocks = batch * nq
    gw = GROUP * HEAD_DIM

    def qrow(b, i):
        return jnp.where(i < nq, b * nq + i, lat_blocks + b * n_ctx_tiles + (i - nq))

    kern = functools.partial(_global_attn_kernel, n_lat_tiles=nq)
    return pl.pallas_call(
        kern,
        grid=(batch, GA_KV_HEADS, nq + (n_ctx_tiles if with_ctx else 0)),
        in_specs=[pl.BlockSpec((tq, gw), lambda b, kh, i: (qrow(b, i), GQ // gw + kh)),
                  pl.BlockSpec((seq, HEAD_DIM), lambda b, kh, i: (b, GK // HEAD_DIM + kh)),
                  pl.BlockSpec((seq, HEAD_DIM), lambda b, kh, i: (b, GV // HEAD_DIM + kh)),
                  pl.BlockSpec((n_ctx, HEAD_DIM),
                               lambda b, kh, i: (batch * seq // n_ctx + b, GK // HEAD_DIM + kh)),
                  pl.BlockSpec((n_ctx, HEAD_DIM),
                               lambda b, kh, i: (batch * seq // n_ctx + b, GV // HEAD_DIM + kh))],
        out_specs=pl.BlockSpec((tq, gw), lambda b, kh, i: (qrow(b, i), kh)),
        out_shape=jax.ShapeDtypeStruct((m, GA_HEADS * HEAD_DIM), BF16),
        scratch_shapes=[pltpu.VMEM((seq, 2 * HEAD_DIM), BF16), pltpu.VMEM((n_ctx, 2 * HEAD_DIM), BF16)],
        compiler_params=_cparams(("arbitrary", "arbitrary", "arbitrary")),
        name="global_attention",
    )(u, u, u, u, u)


def _window_attn_kernel(sink_ref, q_ref, kx_ref, vx_ref, kc_ref, vc_ref, bias_ref, o_ref, vce_ref,
                        *, n_lat_tiles, seq):
    kh = pl.program_id(1)
    i = pl.program_id(2)
    span = CHUNK + 2 * WINDOW
    nt = (((1,), (1,)), ((), ()))

    @pl.when(i == 0)
    def _():
        vce_ref[:, 0:HEAD_DIM] = vc_ref[...]
        vce_ref[:, HEAD_DIM:] = jnp.ones_like(vc_ref)

    start = jnp.clip(i * CHUNK - WINDOW, 0, seq - span)
    start = pl.multiple_of(jnp.where(i < n_lat_tiles, start, 0), WINDOW)
    kw = kx_ref[pl.ds(start, span), :]
    vw = vx_ref[pl.ds(start, span), :]
    vwe = jnp.concatenate([vw, jnp.ones_like(vw)], axis=1)
    for h in range(GROUP):
        q = q_ref[:, h * HEAD_DIM:(h + 1) * HEAD_DIM]
        s_w = lax.dot_general(q, kw, nt, preferred_element_type=F32) + bias_ref[...]
        s_c = lax.dot_general(q, kc_ref[...], nt, preferred_element_type=F32)
        sink = sink_ref[kh * GROUP + h]
        mx = jnp.maximum(jnp.maximum(jnp.max(s_w, axis=-1, keepdims=True),
                                     jnp.max(s_c, axis=-1, keepdims=True)), sink)
        p_w = jnp.exp2((s_w - mx).astype(BF16))
        p_c = jnp.exp2((s_c - mx).astype(BF16))
        pv = (jnp.dot(p_w, vwe, preferred_element_type=F32)
              + jnp.dot(p_c, vce_ref[...], preferred_element_type=F32))
        den = pv[:, HEAD_DIM:] + jnp.exp2(sink - mx)
        o_ref[:, h * HEAD_DIM:(h + 1) * HEAD_DIM] = (pv[:, :HEAD_DIM] / den).astype(BF16)


def _window_bias():
    span = CHUNK + 2 * WINDOW
    r = jnp.arange(CHUNK)[:, None]
    col = jnp.arange(span)[None, :]
    offs = (0, WINDOW, 2 * WINDOW)
    tiles = [jnp.where(jnp.abs(col - off - r) <= WINDOW, 0.0, MASK_BIAS) for off in offs]
    tiles.append(jnp.full((CHUNK, span), MASK_BIAS))
    return jnp.stack(tiles).astype(F32)


def _window_attention(u, sink, *, batch, seq, n_ctx, with_ctx):
    m = u.shape[0]
    nq = seq // CHUNK
    lat_blocks = batch * nq
    gw = GROUP * HEAD_DIM
    span = CHUNK + 2 * WINDOW
    assert seq >= span and nq >= 2

    def qrow(b, i):
        return jnp.where(i < nq, b * nq + i, lat_blocks + b)

    def variant(i):
        return jnp.where(i == 0, 0, jnp.where(i < nq - 1, 1, jnp.where(i == nq - 1, 2, 3)))

    kern = functools.partial(_window_attn_kernel, n_lat_tiles=nq, seq=seq)
    return pl.pallas_call(
        kern,
        grid=(batch, WA_KV_HEADS, nq + (1 if with_ctx else 0)),
        in_specs=[pl.BlockSpec(memory_space=pltpu.SMEM),
                  pl.BlockSpec((CHUNK, gw), lambda b, kh, i: (qrow(b, i), WQ // gw + kh)),
                  pl.BlockSpec((seq, HEAD_DIM), lambda b, kh, i: (b, WK // HEAD_DIM + kh)),
                  pl.BlockSpec((seq, HEAD_DIM), lambda b, kh, i: (b, WV // HEAD_DIM + kh)),
                  pl.BlockSpec((n_ctx, HEAD_DIM),
                               lambda b, kh, i: (batch * seq // n_ctx + b, WK // HEAD_DIM + kh)),
                  pl.BlockSpec((n_ctx, HEAD_DIM),
                               lambda b, kh, i: (batch * seq // n_ctx + b, WV // HEAD_DIM + kh)),
                  pl.BlockSpec((None, CHUNK, span), lambda b, kh, i: (variant(i), 0, 0))],
        out_specs=pl.BlockSpec((CHUNK, gw), lambda b, kh, i: (qrow(b, i), kh)),
        out_shape=jax.ShapeDtypeStruct((m, WA_HEADS * HEAD_DIM), BF16),
        scratch_shapes=[pltpu.VMEM((n_ctx, 2 * HEAD_DIM), BF16)],
        compiler_params=_cparams(("arbitrary", "arbitrary", "arbitrary")),
        name="window_attention",
    )(sink.astype(F32) * LOG2_E, u, u, u, u, u, _window_bias())


def _merge_kernel(o0_ref, o1_ref, o2_ref, g0_ref, g1_ref, g2_ref, w0_ref, w1_ref, w2_ref, out_ref):
    tot = None
    for o_ref, g_ref, w_ref in ((o0_ref, g0_ref, w0_ref), (o1_ref, g1_ref, w1_ref),
                                (o2_ref, g2_ref, w2_ref)):
        term = jax.nn.sigmoid(g_ref[...].astype(F32)) * jnp.dot(
            o_ref[...], w_ref[...], preferred_element_type=F32)
        tot = term if tot is None else tot + term
    out_ref[...] = tot.astype(BF16)


def _merge(o_ret, o_ga, o_wa, u, w_branch, *, rows, tm, d):
    tn = 512
    bw = w_branch.shape[1]
    gate_blk = GATE // tn
    nd = d // tn

    def o_spec():
        return pl.BlockSpec((tm, bw), lambda i, j: (i, 0))

    def g_spec(br):
        return pl.BlockSpec((tm, tn), lambda i, j: (i, gate_blk + br * nd + j))

    def w_spec(br):
        return pl.BlockSpec((None, bw, tn), lambda i, j: (br, 0, j))

    return pl.pallas_call(
        _merge_kernel,
        grid=(rows // tm, nd),
        in_specs=[o_spec(), o_spec(), o_spec(), g_spec(0), g_spec(1), g_spec(2),
                  w_spec(0), w_spec(1), w_spec(2)],
        out_specs=pl.BlockSpec((tm, tn), lambda i, j: (i, j)),
        out_shape=jax.ShapeDtypeStruct((rows, d), BF16),
        compiler_params=_cparams(("arbitrary", "arbitrary")),
        name="branch_merge",
    )(o_ret, o_ga, o_wa, u, u, u, w_branch, w_branch, w_branch)


def _out_proj_kernel(m_ref, w_ref, x_ref, g_ref, o_ref):
    o_ref[...] = x_ref[...] + g_ref[...] * jnp.dot(m_ref[...], w_ref[...], preferred_element_type=F32)


def _out_proj(mix, w_out, x_all, mods3, *, rows, tm, d, mod_idx):
    tn = 512
    nd = d // tn
    return pl.pallas_call(
        _out_proj_kernel,
        grid=(rows // tm, nd),
        in_specs=[pl.BlockSpec((tm, d), lambda i, j: (i, 0)),
                  pl.BlockSpec((d, tn), lambda i, j: (0, j)),
                  pl.BlockSpec((tm, tn), lambda i, j: (i, j)),
                  pl.BlockSpec((None, 1, tn), lambda i, j: (mod_idx(i), 0, 2 * nd + j))],
        out_specs=pl.BlockSpec((tm, tn), lambda i, j: (i, j)),
        out_shape=jax.ShapeDtypeStruct((rows, d), F32),
        compiler_params=_cparams(("arbitrary", "arbitrary")),
        name="out_proj",
    )(mix, w_out, x_all, mods3)


def _router_kernel(x_ref, sh_ref, sc_ref, g_ref, wr_ref, br_ref, h_ref, idx_ref, wt_ref, *, n_experts):
    x = x_ref[...]
    y = x * lax.rsqrt(jnp.mean(x * x, axis=-1, keepdims=True) + NORM_EPS) * g_ref[...]
    h = y * (1.0 + sc_ref[...]) + sh_ref[...]
    h_hi = h.astype(BF16)
    h_ref[...] = h_hi
    h_lo = (h - h_hi.astype(F32)).astype(BF16)
    w = wr_ref[...]
    w_hi = w.astype(BF16)
    w_lo = (w - w_hi.astype(F32)).astype(BF16)
    logits = (jnp.dot(h_hi, w_hi, preferred_element_type=F32)
              + jnp.dot(h_lo, w_hi, preferred_element_type=F32)
              + jnp.dot(h_hi, w_lo, preferred_element_type=F32)) + br_ref[...]
    lane = lax.broadcasted_iota(jnp.int32, logits.shape, 1).astype(F32)
    vals, ids = [], []
    cur = logits
    for _ in range(TOP_K):
        mx = jnp.max(cur, axis=-1, keepdims=True)
        sel = jnp.min(jnp.where(cur == mx, lane, float(n_experts)), axis=-1, keepdims=True)
        vals.append(mx)
        ids.append(sel)
        cur = jnp.where(lane == sel, -jnp.inf, cur)
    e = [jnp.exp(v - vals[0]) for v in vals]
    den = e[0] + e[1] + e[2] + e[3]
    for k in range(TOP_K):
        idx_ref[:, k:k + 1] = ids[k].astype(jnp.int32)
        wt_ref[:, k:k + 1] = e[k] / den


def _router(x_new, mods3, ln_g, w_router, b_router, *, rows, tm, d, mod_idx):
    n_experts = w_router.shape[1]
    kern = functools.partial(_router_kernel, n_experts=n_experts)
    return pl.pallas_call(
        kern,
        grid=(rows // tm,),
        in_specs=[pl.BlockSpec((tm, d), lambda i: (i, 0)),
                  pl.BlockSpec((None, 1, d), lambda i: (mod_idx(i), 0, 3)),
                  pl.BlockSpec((None, 1, d), lambda i: (mod_idx(i), 0, 4)),
                  pl.BlockSpec((1, d), lambda i: (0, 0)),
                  pl.BlockSpec((d, n_experts), lambda i: (0, 0)),
                  pl.BlockSpec((1, n_experts), lambda i: (0, 0))],
        out_specs=[pl.BlockSpec((tm, d), lambda i: (i, 0)),
                   pl.BlockSpec((tm, TOP_K), lambda i: (i, 0)),
                   pl.BlockSpec((tm, TOP_K), lambda i: (i, 0))],
        out_shape=[jax.ShapeDtypeStruct((rows, d), BF16),
                   jax.ShapeDtypeStruct((rows, TOP_K), jnp.int32),
                   jax.ShapeDtypeStruct((rows, TOP_K), F32)],
        compiler_params=_cparams(("arbitrary",)),
        name="norm2_router",
    )(x_new, mods3, mods3, ln_g.reshape(1, d), w_router, b_router.reshape(1, n_experts))


def _moe_up_kernel(ie_ref, ic_ref, it_ref, in_ref, x_ref, wg_ref, wl_ref, bg_ref, bl_ref, o_ref):
    i = pl.program_id(0)

    def compute(n):
        x = x_ref[0:n, :]
        glu = jnp.dot(x, wg_ref[...].astype(BF16), preferred_element_type=F32) + bg_ref[...]
        lin = jnp.dot(x, wl_ref[...].astype(BF16), preferred_element_type=F32) + bl_ref[...]
        glu = jnp.minimum(glu, SWIGLU_LIMIT)
        lin = jnp.clip(lin, -SWIGLU_LIMIT, SWIGLU_LIMIT)
        o_ref[0:n, :] = (glu * jax.nn.sigmoid(SWIGLU_ALPHA * glu) * (lin + 1.0)).astype(BF16)

    tm = x_ref.shape[0]
    for quarters in range(1, MOE_TILE_PARTS + 1):
        n = quarters * tm // MOE_TILE_PARTS
        pl.when(in_ref[i] == n)(functools.partial(compute, n))


def _moe_up(sched, xs, w_gu, b_gu, *, layer, tm, tf):
    r_pad, d = xs.shape
    n_layers, n_experts, _, f2 = w_gu.shape
    f = f2 // 2
    nc = f // tf
    n_items = sched[0].shape[0]
    grid_spec = pltpu.PrefetchScalarGridSpec(
        num_scalar_prefetch=4,
        grid=(n_items,),
        in_specs=[pl.BlockSpec((tm, d), lambda i, ie, ic, it, nr: (it[i], 0)),
                  pl.BlockSpec((None, None, d, tf), lambda i, ie, ic, it, nr: (layer, ie[i], 0, ic[i])),
                  pl.BlockSpec((None, None, d, tf),
                               lambda i, ie, ic, it, nr: (layer, ie[i], 0, nc + ic[i])),
                  pl.BlockSpec((None, None, 1, tf), lambda i, ie, ic, it, nr: (layer, ie[i], 0, ic[i])),
                  pl.BlockSpec((None, None, 1, tf),
                               lambda i, ie, ic, it, nr: (layer, ie[i], 0, nc + ic[i]))],
        out_specs=pl.BlockSpec((tm, tf), lambda i, ie, ic, it, nr: (it[i], ic[i])))
    b4 = b_gu.reshape(n_layers, n_experts, 1, f2)
    return pl.pallas_call(
        _moe_up_kernel,
        grid_spec=grid_spec,
        out_shape=jax.ShapeDtypeStruct((r_pad, f), BF16),
        compiler_params=_cparams(("arbitrary",)),
        name="moe_up",
    )(*sched, xs, w_gu, w_gu, b4, b4)


def _moe_down_kernel(ie_ref, ic_ref, it_ref, in_ref, a_ref, w_ref, b_ref, o_ref):
    i = pl.program_id(0)

    def compute(n):
        y = jnp.dot(a_ref[0:n, :], w_ref[...].astype(BF16), preferred_element_type=F32) + b_ref[...]
        o_ref[0:n, :] = y.astype(BF16)

    tm = a_ref.shape[0]
    for quarters in range(1, MOE_TILE_PARTS + 1):
        n = quarters * tm // MOE_TILE_PARTS
        pl.when(in_ref[i] == n)(functools.partial(compute, n))


def _moe_down(sched, act, w_down, b_down, *, layer, tm, tn):
    r_pad, f = act.shape
    n_layers, n_experts, _, d = w_down.shape
    n_items = sched[0].shape[0]
    grid_spec = pltpu.PrefetchScalarGridSpec(
        num_scalar_prefetch=4,
        grid=(n_items,),
        in_specs=[pl.BlockSpec((tm, f), lambda i, ie, ic, it, nr: (it[i], 0)),
                  pl.BlockSpec((None, None, f, tn), lambda i, ie, ic, it, nr: (layer, ie[i], 0, ic[i])),
                  pl.BlockSpec((None, None, 1, tn), lambda i, ie, ic, it, nr: (layer, ie[i], 0, ic[i]))],
        out_specs=pl.BlockSpec((tm, tn), lambda i, ie, ic, it, nr: (it[i], ic[i])))
    return pl.pallas_call(
        _moe_down_kernel,
        grid_spec=grid_spec,
        out_shape=jax.ShapeDtypeStruct((r_pad, d), BF16),
        compiler_params=_cparams(("arbitrary",)),
        name="moe_down",
    )(*sched, act, w_down, b_down.reshape(n_layers, n_experts, 1, d))


def _moe_schedule(idx, *, n_experts, tm, n_chunks):
    n_tok = idx.shape[0]
    n_pairs = n_tok * TOP_K
    r_pad = n_pairs + n_experts * tm
    n_tiles = r_pad // tm
    part = tm // MOE_TILE_PARTS
    flat_e = idx.reshape(-1)
    experts = jnp.arange(n_experts, dtype=jnp.int32)
    onehot = (flat_e[:, None] == experts[None, :]).astype(jnp.int32)
    csum = jnp.cumsum(onehot, axis=0)
    rank = jnp.sum(csum * onehot, axis=1) - 1
    counts = csum[-1]
    ntiles = (counts + tm - 1) // tm
    tile_end = jnp.cumsum(ntiles)
    tile_start = tile_end - ntiles
    dest = jnp.sum(onehot * tile_start[None, :], axis=1) * tm + rank
    src_tok = (jnp.arange(r_pad, dtype=jnp.int32) % n_tok).at[dest].set(
        jnp.arange(n_pairs, dtype=jnp.int32) // TOP_K, unique_indices=True, mode="promise_in_bounds")
    n_used = tile_end[-1]

    def schedule(n_chunks):
        n_items = n_chunks * n_tiles
        item = jnp.arange(n_items, dtype=jnp.int32)
        valid = item < n_chunks * n_used
        ic_ = jnp.minimum(item, jnp.maximum(n_chunks * n_used - 1, 0))
        e = jnp.minimum(
            jnp.sum((ic_[:, None] >= n_chunks * tile_end[None, :]).astype(jnp.int32), axis=1),
            n_experts - 1)
        sel = (e[:, None] == experts[None, :]).astype(jnp.int32)
        nt_e = jnp.maximum(jnp.sum(sel * ntiles[None, :], axis=1), 1)
        ts_e = jnp.sum(sel * tile_start[None, :], axis=1)
        cnt_e = jnp.sum(sel * counts[None, :], axis=1)
        r = ic_ - n_chunks * ts_e
        c = r // nt_e
        lt = nt_e - 1 - r % nt_e
        t = ts_e + lt
        left = cnt_e - lt * tm
        nrows = jnp.where(valid, jnp.minimum((left + part - 1) // part * part, tm), 0)
        return (e.astype(jnp.int32), c.astype(jnp.int32), t.astype(jnp.int32),
                nrows.astype(jnp.int32))

    pos = dest.reshape(n_tok, TOP_K).T.reshape(-1)
    return [schedule(n) for n in n_chunks], src_tok, pos


def _combine_kernel(x_ref, y0_ref, y1_ref, y2_ref, y3_ref, w_ref, g_ref, ng_ref, sh_ref, sc_ref,
                    *o_refs, final):
    w = w_ref[...]
    y = None
    for k, y_ref in enumerate((y0_ref, y1_ref, y2_ref, y3_ref)):
        term = w[:, k:k + 1] * y_ref[...].astype(F32)
        y = term if y is None else y + term
    x = x_ref[...] + g_ref[...] * y
    normed = x * lax.rsqrt(jnp.mean(x * x, axis=-1, keepdims=True) + NORM_EPS) * ng_ref[...]
    if final:
        o_refs[0][...] = normed
    else:
        o_refs[0][...] = x
        o_refs[1][...] = (normed * (1.0 + sc_ref[...]) + sh_ref[...]).astype(BF16)


def _combine(x_new, ysg, wts, mods3, norm_g, mods3_next, *, rows, tm, d, mod_idx, final):
    assert TOP_K == 4
    kern = functools.partial(_combine_kernel, final=final)
    nblk = rows // tm

    def y_spec(k):
        return pl.BlockSpec((tm, d), lambda i: (k * nblk + i, 0))

    row_spec = pl.BlockSpec((tm, d), lambda i: (i, 0))
    out_specs = [row_spec] if final else [row_spec, row_spec]
    out_shape = [jax.ShapeDtypeStruct((rows, d), F32)]
    if not final:
        out_shape.append(jax.ShapeDtypeStruct((rows, d), BF16))
    return pl.pallas_call(
        kern,
        grid=(nblk,),
        in_specs=[row_spec,
                  y_spec(0), y_spec(1), y_spec(2), y_spec(3),
                  pl.BlockSpec((tm, TOP_K), lambda i: (i, 0)),
                  pl.BlockSpec((None, 1, d), lambda i: (mod_idx(i), 0, 5)),
                  pl.BlockSpec((1, d), lambda i: (0, 0)),
                  pl.BlockSpec((None, 1, d), lambda i: (mod_idx(i), 0, 0)),
                  pl.BlockSpec((None, 1, d), lambda i: (mod_idx(i), 0, 1))],
        out_specs=out_specs,
        out_shape=out_shape,
        compiler_params=_cparams(("arbitrary",)),
        name="moe_combine",
    )(x_new, ysg, ysg, ysg, ysg, wts, mods3, norm_g.reshape(1, d), mods3_next, mods3_next)


def _rope_tables(batch, seq, n_ctx):
    rows = seq // GRID_W
    row = jnp.repeat(jnp.arange(rows, dtype=F32), GRID_W)
    col = jnp.tile(jnp.arange(GRID_W, dtype=F32), rows)
    quarter = HEAD_DIM // 4
    inv_freq = ROPE_THETA ** (-jnp.arange(quarter, dtype=F32) / quarter)
    ang = jnp.concatenate([row[:, None] * inv_freq, col[:, None] * inv_freq], axis=-1)
    cos, sin = jnp.cos(ang), jnp.sin(ang)
    cos_x = jnp.tile(jnp.concatenate([cos, cos], axis=-1), (batch, 1))
    sin_x = jnp.tile(jnp.concatenate([-sin, sin], axis=-1), (batch, 1))
    cos_f = jnp.concatenate([cos_x, jnp.ones((batch * n_ctx, HEAD_DIM), F32)], axis=0)
    sin_f = jnp.concatenate([sin_x, jnp.zeros((batch * n_ctx, HEAD_DIM), F32)], axis=0)
    return cos_f, sin_f


def _column_params(q_gain, k_gain, ncol):
    hp = _head_perm()
    scale = HEAD_DIM ** -0.5
    k_scale = RET_DK ** -0.5
    cs = jnp.ones((ncol,), F32)
    cs = cs.at[RK:RK + RET_HEADS * RET_DK].set(k_scale)
    cs = cs.at[GQ:GQ + GA_HEADS * HEAD_DIM].set(jnp.tile(q_gain.astype(F32)[hp] * (scale * LOG2_E), GA_HEADS))
    cs = cs.at[GK:GK + GA_KV_HEADS * HEAD_DIM].set(jnp.tile(k_gain.astype(F32)[hp], GA_KV_HEADS))
    cs = cs.at[WQ:WQ + WA_HEADS * HEAD_DIM].set(scale * LOG2_E)
    rm = jnp.zeros((ncol,), F32).at[GQ:GQ + GA_HEADS * HEAD_DIM].set(1.0)
    rm = rm.at[GK:GK + GA_KV_HEADS * HEAD_DIM].set(1.0)
    return cs, rm


def kernel(x, c, ctx, c_ctx, ln1_g, ln2_g, w_ada, b_ada, w_in, ret_decay_logit, ga_q_gain,
           ga_k_gain, wa_sink, w_branch, w_out, w_router, b_router, w_gu, b_gu, w_down, b_down,
           final_g):
    batch, seq, d = x.shape
    n_ctx = ctx.shape[1]
    depth = w_ada.shape[0]
    n_experts = w_router.shape[2]
    ncol = w_in.shape[2]
    mx, mc = batch * seq, batch * n_ctx
    m = mx + mc
    tm = min(1024, mc)
    assert seq % tm == 0 and mc % tm == 0 and batch + 1 <= 8
    tm_stream = min(512, tm)
    assert w_gu.shape[3] // 2 == w_down.shape[3] == d

    def mod_idx_for(tile):
        def mod_idx(i):
            return jnp.where(i < mx // tile, i // (seq // tile), batch)
        return mod_idx

    mod_idx = mod_idx_for(tm)
    mod_idx_s = mod_idx_for(tm_stream)

    cc8 = jnp.zeros((8, d), F32).at[:batch].set(c).at[batch].set(c_ctx)
    mods = _modulation(cc8, w_ada, b_ada)
    cos_f, sin_f = _rope_tables(batch, seq, n_ctx)

    x_all = jnp.concatenate([x.reshape(mx, d), ctx.reshape(mc, d)], axis=0)
    e_tm = 512
    tf = min(1024, d)

    for l in range(depth):
        need_ctx = l < depth - 1
        rows = m if need_ctx else mx
        mods3 = mods[l].reshape(8, 1, 6 * d)
        w_in_p = _regroup_columns(w_in[l])
        colscale, rmsmask = _column_params(ga_q_gain[l], ga_k_gain[l], ncol)
        if l == 0:
            h1 = _prenorm(x_all, mods3, ln1_g[l], tm=tm_stream, mod_idx=mod_idx_s)
        u = _in_proj(h1, w_in_p, cos_f, sin_f, colscale, rmsmask)
        o_ret = _retention(u, _retention_tables(ret_decay_logit[l]), batch=batch, seq=seq, n_ctx=n_ctx)
        o_ga = _global_attention(u, batch=batch, seq=seq, n_ctx=n_ctx, with_ctx=need_ctx)
        o_wa = _window_attention(u, wa_sink[l], batch=batch, seq=seq, n_ctx=n_ctx, with_ctx=need_ctx)
        mix = _merge(o_ret, o_ga, o_wa, u, w_branch[l].astype(BF16), rows=rows, tm=tm, d=d)
        x_new = _out_proj(mix, w_out[l].astype(BF16), x_all, mods3, rows=rows, tm=tm, d=d, mod_idx=mod_idx)
        h2, idx, wts = _router(x_new, mods3, ln2_g[l], w_router[l], b_router[l],
                               rows=rows, tm=tm_stream, d=d, mod_idx=mod_idx_s)
        (sched_up, sched_down), src_tok, pos = _moe_schedule(
            idx, n_experts=n_experts, tm=e_tm, n_chunks=(d // tf, 1))
        xs = h2.at[src_tok].get(mode="promise_in_bounds")
        act = _moe_up(sched_up, xs, w_gu, b_gu, layer=l, tm=e_tm, tf=tf)
        ys = _moe_down(sched_down, act, w_down, b_down, layer=l, tm=e_tm, tn=d)
        ysg = ys.at[pos].get(mode="promise_in_bounds")
        if need_ctx:
            x_all, h1 = _combine(x_new, ysg, wts, mods3, ln1_g[l + 1],
                                 mods[l + 1].reshape(8, 1, 6 * d), rows=rows, tm=tm_stream, d=d,
                                 mod_idx=mod_idx_s, final=False)
        else:
            (x_all,) = _combine(x_new, ysg, wts, mods3, final_g, mods3, rows=rows, tm=tm_stream,
                                d=d, mod_idx=mod_idx_s, final=True)
    return x_all[:mx].reshape(batch, seq, d)
```

```python
import functools

import jax
import jax.numpy as jnp
from jax import lax
from jax.experimental import pallas as pl
from jax.experimental.pallas import tpu as pltpu

GRID_W = 64
HEAD_DIM = 128
RET_HEADS = 4
RET_DK = 128
RET_DV = 256
GA_HEADS = 8
GA_KV_HEADS = 2
WA_HEADS = 8
WA_KV_HEADS = 2
WINDOW = 128
N_BRANCHES = 3
TOP_K = 4
SWIGLU_LIMIT = 7.0
SWIGLU_ALPHA = 1.702
ROPE_THETA = 10000.0
NORM_EPS = 1e-6
LOG2_E = 1.4426950408889634
MASK_BIAS = -1e30

LANES = 128
CHUNK = 256
GROUP = GA_HEADS // GA_KV_HEADS
IN_PROJ_MAX_ROWS = 1152
MOE_TILE_PARTS = 4
GA_TQ = 256
VMEM_LIMIT = 56 * 1024 * 1024

RQ, RK, GQ, WQ, GK, WK = 0, 512, 1024, 2048, 3072, 3328
ROPE_COLS = 3584
GV, WV, RV, RG, GATE = 3584, 3840, 4096, 5120, 6144
_ORQ, _ORK, _ORV, _ORG, _OGQ, _OGK, _OGV, _OWQ, _OWK, _OWV, _OGATE = (
    0, 512, 1024, 2048, 3072, 4096, 4352, 4608, 5632, 5888, 6144)

F32 = jnp.float32
BF16 = jnp.bfloat16


def _cparams(sem, vmem=VMEM_LIMIT):
    return pltpu.CompilerParams(dimension_semantics=sem, vmem_limit_bytes=vmem)


def _head_perm():
    j = jnp.arange(HEAD_DIM)
    return jnp.where(j < HEAD_DIM // 2, 2 * j, 2 * (j - HEAD_DIM // 2) + 1)


def _regroup_columns(w):
    d_model = w.shape[0]
    half = HEAD_DIM // 2

    def rope_heads(start, n_heads):
        blk = w[:, start:start + n_heads * HEAD_DIM].reshape(d_model, n_heads, half, 2)
        return blk.transpose(0, 1, 3, 2).reshape(d_model, n_heads * HEAD_DIM)

    def plain(start, size):
        return w[:, start:start + size]

    return jnp.concatenate([
        rope_heads(_ORQ, RET_HEADS), rope_heads(_ORK, RET_HEADS),
        rope_heads(_OGQ, GA_HEADS), rope_heads(_OWQ, WA_HEADS),
        rope_heads(_OGK, GA_KV_HEADS), rope_heads(_OWK, WA_KV_HEADS),
        plain(_OGV, GA_KV_HEADS * HEAD_DIM), plain(_OWV, WA_KV_HEADS * HEAD_DIM),
        plain(_ORV, RET_HEADS * RET_DV), plain(_ORG, RET_HEADS * RET_DV),
        plain(_OGATE, w.shape[1] - _OGATE)], axis=1).astype(BF16)


def _mod_kernel(c_ref, w_ref, b_ref, o_ref):
    c = c_ref[...]
    s = (c * jax.nn.sigmoid(c)).astype(BF16)
    o_ref[...] = jnp.dot(s, w_ref[...].astype(BF16), preferred_element_type=F32) + b_ref[...]


def _modulation(cc8, w_ada, b_ada):
    n_layers, d, n = w_ada.shape
    tn = min(1024, n)
    return pl.pallas_call(
        _mod_kernel,
        grid=(n_layers, n // tn),
        in_specs=[pl.BlockSpec((8, d), lambda l, j: (0, 0)),
                  pl.BlockSpec((None, d, tn), lambda l, j: (l, 0, j)),
                  pl.BlockSpec((None, 1, tn), lambda l, j: (l, 0, j))],
        out_specs=pl.BlockSpec((None, 8, tn), lambda l, j: (l, 0, j)),
        out_shape=jax.ShapeDtypeStruct((n_layers, 8, n), F32),
        compiler_params=_cparams(("arbitrary", "arbitrary")),
        name="modulation",
    )(cc8, w_ada, b_ada.reshape(n_layers, 1, n))


def _prenorm_kernel(x_ref, sh_ref, sc_ref, g_ref, h_ref):
    x = x_ref[...]
    y = x * lax.rsqrt(jnp.mean(x * x, axis=-1, keepdims=True) + NORM_EPS) * g_ref[...]
    h_ref[...] = (y * (1.0 + sc_ref[...]) + sh_ref[...]).astype(BF16)


def _prenorm(x_all, mods3, ln_g, *, tm, mod_idx):
    m, d = x_all.shape
    return pl.pallas_call(
        _prenorm_kernel,
        grid=(m // tm,),
        in_specs=[pl.BlockSpec((tm, d), lambda i: (i, 0)),
                  pl.BlockSpec((None, 1, d), lambda i: (mod_idx(i), 0, 0)),
                  pl.BlockSpec((None, 1, d), lambda i: (mod_idx(i), 0, 1)),
                  pl.BlockSpec((1, d), lambda i: (0, 0))],
        out_specs=pl.BlockSpec((tm, d), lambda i: (i, 0)),
        out_shape=jax.ShapeDtypeStruct((m, d), BF16),
        compiler_params=_cparams(("arbitrary",)),
        name="prenorm",
    )(x_all, mods3, mods3, ln_g.reshape(1, d))


def _in_proj_kernel(h_ref, w_ref, cos_ref, sin_ref, cs_ref, rm_ref, o_ref, acc_ref, *, n_tiles, tn):
    s = pl.program_id(0)

    @pl.when(s == 0)
    def _():
        acc_ref[1] = jnp.zeros(acc_ref.shape[1:], F32)

    cur = s % 2
    prev = acc_ref[1 - cur]
    col0 = (jnp.maximum(s - 1, 0) % n_tiles) * tn
    cos = cos_ref[...]
    sin = sin_ref[...]
    for hh in range(tn // HEAD_DIM):
        sl = slice(hh * HEAD_DIM, (hh + 1) * HEAD_DIM)
        raw = prev[:, sl]
        r = lax.rsqrt(jnp.mean(raw * raw, axis=-1, keepdims=True) + NORM_EPS)
        rm = rm_ref[:, sl]
        t = raw * (rm * r + (1.0 - rm)) * cs_ref[:, sl]
        roped = t * cos + pltpu.roll(t, HEAD_DIM // 2, 1) * sin
        is_rope = col0 + hh * HEAD_DIM < ROPE_COLS
        o_ref[:, sl] = jnp.where(is_rope, roped, raw).astype(BF16)
    acc_ref[cur] = jnp.dot(h_ref[...], w_ref[...], preferred_element_type=F32)


def _in_proj(h, w_in_p, cos_f, sin_f, colscale, rmsmask):
    m, d = h.shape
    ncol = w_in_p.shape[1]
    tn = 768 if ncol % 768 == 0 else 512
    tm = max(t for t in range(LANES, IN_PROJ_MAX_ROWS + 1, LANES) if m % t == 0)
    n_tiles = ncol // tn
    n_rows = m // tm
    n_steps = n_rows * n_tiles + 1

    def row(s):
        return jnp.minimum(s // n_tiles, n_rows - 1)

    def col(s):
        return jnp.where(s < n_steps - 1, s % n_tiles, n_tiles - 1)

    def prow(s):
        return jnp.maximum(s - 1, 0) // n_tiles

    def pcol(s):
        return jnp.maximum(s - 1, 0) % n_tiles

    kern = functools.partial(_in_proj_kernel, n_tiles=n_tiles, tn=tn)
    return pl.pallas_call(
        kern,
        grid=(n_steps,),
        in_specs=[pl.BlockSpec((tm, d), lambda s: (row(s), 0)),
                  pl.BlockSpec((d, tn), lambda s: (0, col(s))),
                  pl.BlockSpec((tm, HEAD_DIM), lambda s: (prow(s), 0)),
                  pl.BlockSpec((tm, HEAD_DIM), lambda s: (prow(s), 0)),
                  pl.BlockSpec((1, tn), lambda s: (0, pcol(s))),
                  pl.BlockSpec((1, tn), lambda s: (0, pcol(s)))],
        out_specs=pl.BlockSpec((tm, tn), lambda s: (prow(s), pcol(s))),
        out_shape=jax.ShapeDtypeStruct((m, ncol), BF16),
        scratch_shapes=[pltpu.VMEM((2, tm, tn), F32)],
        compiler_params=_cparams(("arbitrary",)),
        name="in_proj",
    )(h, w_in_p, cos_f, sin_f, colscale.reshape(1, ncol), rmsmask.reshape(1, ncol))


def _retention_kernel(q_ref, k_ref, v_ref, g_ref, dm_ref, qd_ref, kd_ref, cd_ref,
                      o_ref, s_ref, ob_ref, *, n_lat_chunks):
    d = pl.program_id(1)
    t = pl.program_id(2)

    @pl.when(t == 0)
    def _():
        s_ref[...] = jnp.zeros_like(s_ref)

    cid = jnp.where(d == 1, t, jnp.where(t == 0, 0, 1 + n_lat_chunks - t))
    outs = []
    for h in range(RET_HEADS):
        q = q_ref[:, h * RET_DK:(h + 1) * RET_DK]
        k = k_ref[:, h * RET_DK:(h + 1) * RET_DK]
        v = v_ref[:, h * RET_DV:(h + 1) * RET_DV]
        state = s_ref[h]
        scores = lax.dot_general(q, k, (((1,), (1,)), ((), ())), preferred_element_type=F32)
        p = (scores * dm_ref[h]).astype(BF16)
        qd = (q.astype(F32) * qd_ref[h]).astype(BF16)
        outs.append(jnp.dot(p, v, preferred_element_type=F32)
                    + jnp.dot(qd, state.astype(BF16), preferred_element_type=F32))
        kd = (k.astype(F32) * kd_ref[h]).astype(BF16)
        s_ref[h] = state * cd_ref[h] + lax.dot_general(
            kd, v, (((0,), (0,)), ((), ())), preferred_element_type=F32)

    @pl.when(d == 0)
    def _():
        for h in range(RET_HEADS):
            ob_ref[cid, :, h * RET_DV:(h + 1) * RET_DV] = outs[h]

    @pl.when(d == 1)
    def _():
        for h in range(RET_HEADS):
            sl = slice(h * RET_DV, (h + 1) * RET_DV)
            tot = outs[h] + ob_ref[cid, :, sl]
            tot = tot * lax.rsqrt(jnp.mean(tot * tot, axis=-1, keepdims=True) + NORM_EPS)
            g = g_ref[:, sl].astype(F32)
            o_ref[:, sl] = (tot * (g * jax.nn.sigmoid(g))).astype(BF16)


def _retention(u, tabs, *, batch, seq, n_ctx):
    m = u.shape[0]
    assert n_ctx == CHUNK and seq % CHUNK == 0
    nc = seq // CHUNK
    lat_blocks = batch * nc
    dmat, qdec, kdec, cdec = tabs

    def rowblk(b, d, t):
        lat = b * nc + jnp.where(d == 1, t - 1, nc - t)
        return jnp.where(t == 0, lat_blocks + b, lat)

    def in_map(col):
        return lambda b, d, t: (rowblk(b, d, t), col)

    def out_map(b, d, t):
        return (jnp.where(d == 1, rowblk(b, 1, t), lat_blocks + b), 0)

    def tab_map(b, d, t):
        return (d, 0, 0, 0)

    qk_w = RET_HEADS * RET_DK
    v_w = RET_HEADS * RET_DV
    kern = functools.partial(_retention_kernel, n_lat_chunks=nc)
    return pl.pallas_call(
        kern,
        grid=(batch, 2, nc + 1),
        in_specs=[pl.BlockSpec((CHUNK, qk_w), in_map(RQ // qk_w)),
                  pl.BlockSpec((CHUNK, qk_w), in_map(RK // qk_w)),
                  pl.BlockSpec((CHUNK, v_w), in_map(RV // v_w)),
                  pl.BlockSpec((CHUNK, v_w), in_map(RG // v_w)),
                  pl.BlockSpec((None, RET_HEADS, CHUNK, CHUNK), tab_map),
                  pl.BlockSpec((None, RET_HEADS, CHUNK, RET_DK), tab_map),
                  pl.BlockSpec((None, RET_HEADS, CHUNK, RET_DK), tab_map),
                  pl.BlockSpec((None, RET_HEADS, 1, RET_DV), tab_map)],
        out_specs=pl.BlockSpec((CHUNK, v_w), out_map),
        out_shape=jax.ShapeDtypeStruct((m, v_w), BF16),
        scratch_shapes=[pltpu.VMEM((RET_HEADS, RET_DK, RET_DV), F32),
                        pltpu.VMEM((nc + 1, CHUNK, v_w), F32)],
        compiler_params=_cparams(("arbitrary", "arbitrary", "arbitrary")),
        name="retention",
    )(u, u, u, u, dmat, qdec, kdec, cdec)


def _retention_tables(decay_logit):
    lg = jax.nn.log_sigmoid(decay_logit.astype(F32))
    lf, lb = lg[0][:, None, None], lg[1][:, None, None]
    pos = jnp.arange(CHUNK, dtype=F32)
    diff = pos[:, None] - pos[None, :]
    d_f = jnp.where(diff >= 0, jnp.exp(jnp.where(diff >= 0, diff, 0.0) * lf), 0.0)
    d_b = jnp.where(diff < 0, jnp.exp(jnp.where(diff < 0, -diff, 0.0) * lb), 0.0)
    ones_k = jnp.ones((1, 1, RET_DK), F32)
    q_f = jnp.exp((pos + 1.0)[None, :, None] * lf) * ones_k
    q_b = jnp.exp((CHUNK - pos)[None, :, None] * lb) * ones_k
    k_f = jnp.exp((CHUNK - 1.0 - pos)[None, :, None] * lf) * ones_k
    k_b = jnp.exp(pos[None, :, None] * lb) * ones_k
    ones_v = jnp.ones((1, 1, RET_DV), F32)
    c_f = jnp.exp(CHUNK * lf) * ones_v
    c_b = jnp.exp(CHUNK * lb) * ones_v
    return (jnp.stack([d_b, d_f]), jnp.stack([q_b, q_f]), jnp.stack([k_b, k_f]),
            jnp.stack([c_b, c_f]))


def _global_attn_kernel(q_ref, kx_ref, vx_ref, kc_ref, vc_ref, o_ref, vxe_ref, vce_ref, *, n_lat_tiles):
    i = pl.program_id(2)
    nt = (((1,), (1,)), ((), ()))

    @pl.when(i == 0)
    def _():
        vxe_ref[:, 0:HEAD_DIM] = vx_ref[...]
        vxe_ref[:, HEAD_DIM:] = jnp.ones_like(vx_ref)
        vce_ref[:, 0:HEAD_DIM] = vc_ref[...]
        vce_ref[:, HEAD_DIM:] = jnp.ones_like(vc_ref)

    def scores(h):
        q = q_ref[:, h * HEAD_DIM:(h + 1) * HEAD_DIM]
        return (lax.dot_general(q, kc_ref[...], nt, preferred_element_type=F32),
                lax.dot_general(q, kx_ref[...], nt, preferred_element_type=F32))

    def store(h, pv):
        o_ref[:, h * HEAD_DIM:(h + 1) * HEAD_DIM] = (pv[:, :HEAD_DIM] / pv[:, HEAD_DIM:]).astype(BF16)

    @pl.when(i < n_lat_tiles)
    def _():
        nxt = scores(0)
        for h in range(GROUP):
            s_c, s_x = nxt
            if h + 1 < GROUP:
                nxt = scores(h + 1)
            mx = jnp.maximum(jnp.max(s_c, axis=-1, keepdims=True), jnp.max(s_x, axis=-1, keepdims=True))
            p_c = jnp.exp2((s_c - mx).astype(BF16))
            p_x = jnp.exp2((s_x - mx).astype(BF16))
            store(h, jnp.dot(p_c, vce_ref[...], preferred_element_type=F32)
                  + jnp.dot(p_x, vxe_ref[...], preferred_element_type=F32))

    @pl.when(i >= n_lat_tiles)
    def _():
        for h in range(GROUP):
            s_c = lax.dot_general(q_ref[:, h * HEAD_DIM:(h + 1) * HEAD_DIM], kc_ref[...], nt,
                                  preferred_element_type=F32)
            p_c = jnp.exp2((s_c - jnp.max(s_c, axis=-1, keepdims=True)).astype(BF16))
            store(h, jnp.dot(p_c, vce_ref[...], preferred_element_type=F32))


def _global_attention(u, *, batch, seq, n_ctx, with_ctx):
    m = u.shape[0]
    tq = GA_TQ
    nq = seq // tq
    n_ctx_tiles = n_ctx // tq
    lat_blocks = batch * nq
    gw = GROUP * HEAD_DIM

    def qrow(b, i):
        return jnp.where(i < nq, b * nq + i, lat_blocks + b * n_ctx_tiles + (i - nq))

    kern = functools.partial(_global_attn_kernel, n_lat_tiles=nq)
    return pl.pallas_call(
        kern,
        grid=(batch, GA_KV_HEADS, nq + (n_ctx_tiles if with_ctx else 0)),
        in_specs=[pl.BlockSpec((tq, gw), lambda b, kh, i: (qrow(b, i), GQ // gw + kh)),
                  pl.BlockSpec((seq, HEAD_DIM), lambda b, kh, i: (b, GK // HEAD_DIM + kh)),
                  pl.BlockSpec((seq, HEAD_DIM), lambda b, kh, i: (b, GV // HEAD_DIM + kh)),
                  pl.BlockSpec((n_ctx, HEAD_DIM),
                               lambda b, kh, i: (batch * seq // n_ctx + b, GK // HEAD_DIM + kh)),
                  pl.BlockSpec((n_ctx, HEAD_DIM),
                               lambda b, kh, i: (batch * seq // n_ctx + b, GV // HEAD_DIM + kh))],
        out_specs=pl.BlockSpec((tq, gw), lambda b, kh, i: (qrow(b, i), kh)),
        out_shape=jax.ShapeDtypeStruct((m, GA_HEADS * HEAD_DIM), BF16),
        scratch_shapes=[pltpu.VMEM((seq, 2 * HEAD_DIM), BF16), pltpu.VMEM((n_ctx, 2 * HEAD_DIM), BF16)],
        compiler_params=_cparams(("arbitrary", "arbitrary", "arbitrary")),
        name="global_attention",
    )(u, u, u, u, u)


def _window_attn_kernel(sink_ref, q_ref, kx_ref, vx_ref, kc_ref, vc_ref, bias_ref, o_ref, vce_ref,
                        *, n_lat_tiles, seq):
    kh = pl.program_id(1)
    i = pl.program_id(2)
    span = CHUNK + 2 * WINDOW
    nt = (((1,), (1,)), ((), ()))

    @pl.when(i == 0)
    def _():
        vce_ref[:, 0:HEAD_DIM] = vc_ref[...]
        vce_ref[:, HEAD_DIM:] = jnp.ones_like(vc_ref)

    start = jnp.clip(i * CHUNK - WINDOW, 0, seq - span)
    start = pl.multiple_of(jnp.where(i < n_lat_tiles, start, 0), WINDOW)
    kw = kx_ref[pl.ds(start, span), :]
    vw = vx_ref[pl.ds(start, span), :]
    vwe = jnp.concatenate([vw, jnp.ones_like(vw)], axis=1)
    for h in range(GROUP):
        q = q_ref[:, h * HEAD_DIM:(h + 1) * HEAD_DIM]
        s_w = lax.dot_general(q, kw, nt, preferred_element_type=F32) + bias_ref[...]
        s_c = lax.dot_general(q, kc_ref[...], nt, preferred_element_type=F32)
        sink = sink_ref[kh * GROUP + h]
        mx = jnp.maximum(jnp.maximum(jnp.max(s_w, axis=-1, keepdims=True),
                                     jnp.max(s_c, axis=-1, keepdims=True)), sink)
        p_w = jnp.exp2((s_w - mx).astype(BF16))
        p_c = jnp.exp2((s_c - mx).astype(BF16))
        pv = (jnp.dot(p_w, vwe, preferred_element_type=F32)
              + jnp.dot(p_c, vce_ref[...], preferred_element_type=F32))
        den = pv[:, HEAD_DIM:] + jnp.exp2(sink - mx)
        o_ref[:, h * HEAD_DIM:(h + 1) * HEAD_DIM] = (pv[:, :HEAD_DIM] / den).astype(BF16)


def _window_bias():
    span = CHUNK + 2 * WINDOW
    r = jnp.arange(CHUNK)[:, None]
    col = jnp.arange(span)[None, :]
    offs = (0, WINDOW, 2 * WINDOW)
    tiles = [jnp.where(jnp.abs(col - off - r) <= WINDOW, 0.0, MASK_BIAS) for off in offs]
    tiles.append(jnp.full((CHUNK, span), MASK_BIAS))
    return jnp.stack(tiles).astype(F32)


def _window_attention(u, sink, *, batch, seq, n_ctx, with_ctx):
    m = u.shape[0]
    nq = seq // CHUNK
    lat_blocks = batch * nq
    gw = GROUP * HEAD_DIM
    span = CHUNK + 2 * WINDOW
    assert seq >= span and nq >= 2

    def qrow(b, i):
        return jnp.where(i < nq, b * nq + i, lat_blocks + b)

    def variant(i):
        return jnp.where(i == 0, 0, jnp.where(i < nq - 1, 1, jnp.where(i == nq - 1, 2, 3)))

    kern = functools.partial(_window_attn_kernel, n_lat_tiles=nq, seq=seq)
    return pl.pallas_call(
        kern,
        grid=(batch, WA_KV_HEADS, nq + (1 if with_ctx else 0)),
        in_specs=[pl.BlockSpec(memory_space=pltpu.SMEM),
                  pl.BlockSpec((CHUNK, gw), lambda b, kh, i: (qrow(b, i), WQ // gw + kh)),
                  pl.BlockSpec((seq, HEAD_DIM), lambda b, kh, i: (b, WK // HEAD_DIM + kh)),
                  pl.BlockSpec((seq, HEAD_DIM), lambda b, kh, i: (b, WV // HEAD_DIM + kh)),
                  pl.BlockSpec((n_ctx, HEAD_DIM),
                               lambda b, kh, i: (batch * seq // n_ctx + b, WK // HEAD_DIM + kh)),
                  pl.BlockSpec((n_ctx, HEAD_DIM),
                               lambda b, kh, i: (batch * seq // n_ctx + b, WV // HEAD_DIM + kh)),
                  pl.BlockSpec((None, CHUNK, span), lambda b, kh, i: (variant(i), 0, 0))],
        out_specs=pl.BlockSpec((CHUNK, gw), lambda b, kh, i: (qrow(b, i), kh)),
        out_shape=jax.ShapeDtypeStruct((m, WA_HEADS * HEAD_DIM), BF16),
        scratch_shapes=[pltpu.VMEM((n_ctx, 2 * HEAD_DIM), BF16)],
        compiler_params=_cparams(("arbitrary", "arbitrary", "arbitrary")),
        name="window_attention",
    )(sink.astype(F32) * LOG2_E, u, u, u, u, u, _window_bias())


def _merge_kernel(o0_ref, o1_ref, o2_ref, g0_ref, g1_ref, g2_ref, w0_ref, w1_ref, w2_ref, out_ref):
    tot = None
    for o_ref, g_ref, w_ref in ((o0_ref, g0_ref, w0_ref), (o1_ref, g1_ref, w1_ref),
                                (o2_ref, g2_ref, w2_ref)):
        term = jax.nn.sigmoid(g_ref[...].astype(F32)) * jnp.dot(
            o_ref[...], w_ref[...], preferred_element_type=F32)
        tot = term if tot is None else tot + term
    out_ref[...] = tot.astype(BF16)


def _merge(o_ret, o_ga, o_wa, u, w_branch, *, rows, tm, d):
    tn = 512
    bw = w_branch.shape[1]
    gate_blk = GATE // tn
    nd = d // tn

    def o_spec():
        return pl.BlockSpec((tm, bw), lambda i, j: (i, 0))

    def g_spec(br):
        return pl.BlockSpec((tm, tn), lambda i, j: (i, gate_blk + br * nd + j))

    def w_spec(br):
        return pl.BlockSpec((None, bw, tn), lambda i, j: (br, 0, j))

    return pl.pallas_call(
        _merge_kernel,
        grid=(rows // tm, nd),
        in_specs=[o_spec(), o_spec(), o_spec(), g_spec(0), g_spec(1), g_spec(2),
                  w_spec(0), w_spec(1), w_spec(2)],
        out_specs=pl.BlockSpec((tm, tn), lambda i, j: (i, j)),
        out_shape=jax.ShapeDtypeStruct((rows, d), BF16),
        compiler_params=_cparams(("arbitrary", "arbitrary")),
        name="branch_merge",
    )(o_ret, o_ga, o_wa, u, u, u, w_branch, w_branch, w_branch)


def _out_proj_kernel(m_ref, w_ref, x_ref, g_ref, o_ref):
    o_ref[...] = x_ref[...] + g_ref[...] * jnp.dot(m_ref[...], w_ref[...], preferred_element_type=F32)


def _out_proj(mix, w_out, x_all, mods3, *, rows, tm, d, mod_idx):
    tn = 1024 if d % 1024 == 0 else 512
    nd = d // tn
    return pl.pallas_call(
        _out_proj_kernel,
        grid=(rows // tm, nd),
        in_specs=[pl.BlockSpec((tm, d), lambda i, j: (i, 0)),
                  pl.BlockSpec((d, tn), lambda i, j: (0, j)),
                  pl.BlockSpec((tm, tn), lambda i, j: (i, j)),
                  pl.BlockSpec((None, 1, tn), lambda i, j: (mod_idx(i), 0, 2 * nd + j))],
        out_specs=pl.BlockSpec((tm, tn), lambda i, j: (i, j)),
        out_shape=jax.ShapeDtypeStruct((rows, d), F32),
        compiler_params=_cparams(("arbitrary", "arbitrary")),
        name="out_proj",
    )(mix, w_out, x_all, mods3)


def _router_kernel(x_ref, sh_ref, sc_ref, g_ref, wr_ref, br_ref, h_ref, idx_ref, wt_ref, *, n_experts):
    x = x_ref[...]
    y = x * lax.rsqrt(jnp.mean(x * x, axis=-1, keepdims=True) + NORM_EPS) * g_ref[...]
    h = y * (1.0 + sc_ref[...]) + sh_ref[...]
    h_hi = h.astype(BF16)
    h_ref[...] = h_hi
    h_lo = (h - h_hi.astype(F32)).astype(BF16)
    w = wr_ref[...]
    w_hi = w.astype(BF16)
    w_lo = (w - w_hi.astype(F32)).astype(BF16)
    logits = (jnp.dot(h_hi, w_hi, preferred_element_type=F32)
              + jnp.dot(h_lo, w_hi, preferred_element_type=F32)
              + jnp.dot(h_hi, w_lo, preferred_element_type=F32)) + br_ref[...]
    lane = lax.broadcasted_iota(jnp.int32, logits.shape, 1).astype(F32)
    vals, ids = [], []
    cur = logits
    for _ in range(TOP_K):
        mx = jnp.max(cur, axis=-1, keepdims=True)
        sel = jnp.min(jnp.where(cur == mx, lane, float(n_experts)), axis=-1, keepdims=True)
        vals.append(mx)
        ids.append(sel)
        cur = jnp.where(lane == sel, -jnp.inf, cur)
    e = [jnp.exp(v - vals[0]) for v in vals]
    den = e[0] + e[1] + e[2] + e[3]
    for k in range(TOP_K):
        idx_ref[:, k:k + 1] = ids[k].astype(jnp.int32)
        wt_ref[:, k:k + 1] = e[k] / den


def _router(x_new, mods3, ln_g, w_router, b_router, *, rows, tm, d, mod_idx):
    n_experts = w_router.shape[1]
    kern = functools.partial(_router_kernel, n_experts=n_experts)
    return pl.pallas_call(
        kern,
        grid=(rows // tm,),
        in_specs=[pl.BlockSpec((tm, d), lambda i: (i, 0)),
                  pl.BlockSpec((None, 1, d), lambda i: (mod_idx(i), 0, 3)),
                  pl.BlockSpec((None, 1, d), lambda i: (mod_idx(i), 0, 4)),
                  pl.BlockSpec((1, d), lambda i: (0, 0)),
                  pl.BlockSpec((d, n_experts), lambda i: (0, 0)),
                  pl.BlockSpec((1, n_experts), lambda i: (0, 0))],
        out_specs=[pl.BlockSpec((tm, d), lambda i: (i, 0)),
                   pl.BlockSpec((tm, TOP_K), lambda i: (i, 0)),
                   pl.BlockSpec((tm, TOP_K), lambda i: (i, 0))],
        out_shape=[jax.ShapeDtypeStruct((rows, d), BF16),
                   jax.ShapeDtypeStruct((rows, TOP_K), jnp.int32),
                   jax.ShapeDtypeStruct((rows, TOP_K), F32)],
        compiler_params=_cparams(("arbitrary",)),
        name="norm2_router",
    )(x_new, mods3, mods3, ln_g.reshape(1, d), w_router, b_router.reshape(1, n_experts))


def _moe_up_kernel(ie_ref, ic_ref, it_ref, in_ref, x_ref, wg_ref, wl_ref, bg_ref, bl_ref, o_ref):
    i = pl.program_id(0)

    def compute(n):
        x = x_ref[0:n, :]
        glu = jnp.dot(x, wg_ref[...].astype(BF16), preferred_element_type=F32) + bg_ref[...]
        lin = jnp.dot(x, wl_ref[...].astype(BF16), preferred_element_type=F32) + bl_ref[...]
        glu = jnp.minimum(glu, SWIGLU_LIMIT)
        lin = jnp.clip(lin, -SWIGLU_LIMIT, SWIGLU_LIMIT)
        o_ref[0:n, :] = (glu * jax.nn.sigmoid(SWIGLU_ALPHA * glu) * (lin + 1.0)).astype(BF16)

    tm = x_ref.shape[0]
    for quarters in range(1, MOE_TILE_PARTS + 1):
        n = quarters * tm // MOE_TILE_PARTS
        pl.when(in_ref[i] == n)(functools.partial(compute, n))


def _moe_up(sched, xs, w_gu, b_gu, *, layer, tm, tf):
    r_pad, d = xs.shape
    n_layers, n_experts, _, f2 = w_gu.shape
    f = f2 // 2
    nc = f // tf
    n_items = sched[0].shape[0]
    grid_spec = pltpu.PrefetchScalarGridSpec(
        num_scalar_prefetch=4,
        grid=(n_items,),
        in_specs=[pl.BlockSpec((tm, d), lambda i, ie, ic, it, nr: (it[i], 0)),
                  pl.BlockSpec((None, None, d, tf), lambda i, ie, ic, it, nr: (layer, ie[i], 0, ic[i])),
                  pl.BlockSpec((None, None, d, tf),
                               lambda i, ie, ic, it, nr: (layer, ie[i], 0, nc + ic[i])),
                  pl.BlockSpec((None, None, 1, tf), lambda i, ie, ic, it, nr: (layer, ie[i], 0, ic[i])),
                  pl.BlockSpec((None, None, 1, tf),
                               lambda i, ie, ic, it, nr: (layer, ie[i], 0, nc + ic[i]))],
        out_specs=pl.BlockSpec((tm, tf), lambda i, ie, ic, it, nr: (it[i], ic[i])))
    b4 = b_gu.reshape(n_layers, n_experts, 1, f2)
    return pl.pallas_call(
        _moe_up_kernel,
        grid_spec=grid_spec,
        out_shape=jax.ShapeDtypeStruct((r_pad, f), BF16),
        compiler_params=_cparams(("arbitrary",)),
        name="moe_up",
    )(*sched, xs, w_gu, w_gu, b4, b4)


def _moe_down_kernel(ie_ref, ic_ref, it_ref, in_ref, a_ref, w_ref, b_ref, o_ref):
    i = pl.program_id(0)

    def compute(n):
        y = jnp.dot(a_ref[0:n, :], w_ref[...].astype(BF16), preferred_element_type=F32) + b_ref[...]
        o_ref[0:n, :] = y.astype(BF16)

    tm = a_ref.shape[0]
    for quarters in range(1, MOE_TILE_PARTS + 1):
        n = quarters * tm // MOE_TILE_PARTS
        pl.when(in_ref[i] == n)(functools.partial(compute, n))


def _moe_down(sched, act, w_down, b_down, *, layer, tm, tn):
    r_pad, f = act.shape
    n_layers, n_experts, _, d = w_down.shape
    n_items = sched[0].shape[0]
    grid_spec = pltpu.PrefetchScalarGridSpec(
        num_scalar_prefetch=4,
        grid=(n_items,),
        in_specs=[pl.BlockSpec((tm, f), lambda i, ie, ic, it, nr: (it[i], 0)),
                  pl.BlockSpec((None, None, f, tn), lambda i, ie, ic, it, nr: (layer, ie[i], 0, ic[i])),
                  pl.BlockSpec((None, None, 1, tn), lambda i, ie, ic, it, nr: (layer, ie[i], 0, ic[i]))],
        out_specs=pl.BlockSpec((tm, tn), lambda i, ie, ic, it, nr: (it[i], ic[i])))
    return pl.pallas_call(
        _moe_down_kernel,
        grid_spec=grid_spec,
        out_shape=jax.ShapeDtypeStruct((r_pad, d), BF16),
        compiler_params=_cparams(("arbitrary",)),
        name="moe_down",
    )(*sched, act, w_down, b_down.reshape(n_layers, n_experts, 1, d))


def _moe_schedule(idx, *, n_experts, tm, n_chunks):
    n_tok = idx.shape[0]
    n_pairs = n_tok * TOP_K
    r_pad = n_pairs + n_experts * tm
    n_tiles = r_pad // tm
    part = tm // MOE_TILE_PARTS
    flat_e = idx.reshape(-1)
    experts = jnp.arange(n_experts, dtype=jnp.int32)
    onehot = (flat_e[:, None] == experts[None, :]).astype(jnp.int32)
    csum = jnp.cumsum(onehot, axis=0)
    rank = jnp.sum(csum * onehot, axis=1) - 1
    counts = csum[-1]
    ntiles = (counts + tm - 1) // tm
    tile_end = jnp.cumsum(ntiles)
    tile_start = tile_end - ntiles
    dest = jnp.sum(onehot * tile_start[None, :], axis=1) * tm + rank
    src_tok = (jnp.arange(r_pad, dtype=jnp.int32) % n_tok).at[dest].set(
        jnp.arange(n_pairs, dtype=jnp.int32) // TOP_K, unique_indices=True, mode="promise_in_bounds")
    n_used = tile_end[-1]

    def schedule(n_chunks):
        n_items = n_chunks * n_tiles
        item = jnp.arange(n_items, dtype=jnp.int32)
        valid = item < n_chunks * n_used
        ic_ = jnp.minimum(item, jnp.maximum(n_chunks * n_used - 1, 0))
        e = jnp.minimum(
            jnp.sum((ic_[:, None] >= n_chunks * tile_end[None, :]).astype(jnp.int32), axis=1),
            n_experts - 1)
        sel = (e[:, None] == experts[None, :]).astype(jnp.int32)
        nt_e = jnp.maximum(jnp.sum(sel * ntiles[None, :], axis=1), 1)
        ts_e = jnp.sum(sel * tile_start[None, :], axis=1)
        cnt_e = jnp.sum(sel * counts[None, :], axis=1)
        r = ic_ - n_chunks * ts_e
        c = r // nt_e
        lt = nt_e - 1 - r % nt_e
        t = ts_e + lt
        left = cnt_e - lt * tm
        nrows = jnp.where(valid, jnp.minimum((left + part - 1) // part * part, tm), 0)
        return (e.astype(jnp.int32), c.astype(jnp.int32), t.astype(jnp.int32),
                nrows.astype(jnp.int32))

    pos = dest.reshape(n_tok, TOP_K).T.reshape(-1)
    return [schedule(n) for n in n_chunks], src_tok, pos


def _combine_kernel(x_ref, y0_ref, y1_ref, y2_ref, y3_ref, w_ref, g_ref, ng_ref, sh_ref, sc_ref,
                    *o_refs, final):
    w = w_ref[...]
    y = None
    for k, y_ref in enumerate((y0_ref, y1_ref, y2_ref, y3_ref)):
        term = w[:, k:k + 1] * y_ref[...].astype(F32)
        y = term if y is None else y + term
    x = x_ref[...] + g_ref[...] * y
    normed = x * lax.rsqrt(jnp.mean(x * x, axis=-1, keepdims=True) + NORM_EPS) * ng_ref[...]
    if final:
        o_refs[0][...] = normed
    else:
        o_refs[0][...] = x
        o_refs[1][...] = (normed * (1.0 + sc_ref[...]) + sh_ref[...]).astype(BF16)


def _combine(x_new, ysg, wts, mods3, norm_g, mods3_next, *, rows, tm, d, mod_idx, final):
    assert TOP_K == 4
    kern = functools.partial(_combine_kernel, final=final)
    nblk = rows // tm

    def y_spec(k):
        return pl.BlockSpec((tm, d), lambda i: (k * nblk + i, 0))

    row_spec = pl.BlockSpec((tm, d), lambda i: (i, 0))
    out_specs = [row_spec] if final else [row_spec, row_spec]
    out_shape = [jax.ShapeDtypeStruct((rows, d), F32)]
    if not final:
        out_shape.append(jax.ShapeDtypeStruct((rows, d), BF16))
    return pl.pallas_call(
        kern,
        grid=(nblk,),
        in_specs=[row_spec,
                  y_spec(0), y_spec(1), y_spec(2), y_spec(3),
                  pl.BlockSpec((tm, TOP_K), lambda i: (i, 0)),
                  pl.BlockSpec((None, 1, d), lambda i: (mod_idx(i), 0, 5)),
                  pl.BlockSpec((1, d), lambda i: (0, 0)),
                  pl.BlockSpec((None, 1, d), lambda i: (mod_idx(i), 0, 0)),
                  pl.BlockSpec((None, 1, d), lambda i: (mod_idx(i), 0, 1))],
        out_specs=out_specs,
        out_shape=out_shape,
        compiler_params=_cparams(("arbitrary",)),
        name="moe_combine",
    )(x_new, ysg, ysg, ysg, ysg, wts, mods3, norm_g.reshape(1, d), mods3_next, mods3_next)


def _rope_tables(batch, seq, n_ctx):
    rows = seq // GRID_W
    row = jnp.repeat(jnp.arange(rows, dtype=F32), GRID_W)
    col = jnp.tile(jnp.arange(GRID_W, dtype=F32), rows)
    quarter = HEAD_DIM // 4
    inv_freq = ROPE_THETA ** (-jnp.arange(quarter, dtype=F32) / quarter)
    ang = jnp.concatenate([row[:, None] * inv_freq, col[:, None] * inv_freq], axis=-1)
    cos, sin = jnp.cos(ang), jnp.sin(ang)
    cos_x = jnp.tile(jnp.concatenate([cos, cos], axis=-1), (batch, 1))
    sin_x = jnp.tile(jnp.concatenate([-sin, sin], axis=-1), (batch, 1))
    cos_f = jnp.concatenate([cos_x, jnp.ones((batch * n_ctx, HEAD_DIM), F32)], axis=0)
    sin_f = jnp.concatenate([sin_x, jnp.zeros((batch * n_ctx, HEAD_DIM), F32)], axis=0)
    return cos_f, sin_f


def _column_params(q_gain, k_gain, ncol):
    hp = _head_perm()
    scale = HEAD_DIM ** -0.5
    k_scale = RET_DK ** -0.5
    cs = jnp.ones((ncol,), F32)
    cs = cs.at[RK:RK + RET_HEADS * RET_DK].set(k_scale)
    cs = cs.at[GQ:GQ + GA_HEADS * HEAD_DIM].set(jnp.tile(q_gain.astype(F32)[hp] * (scale * LOG2_E), GA_HEADS))
    cs = cs.at[GK:GK + GA_KV_HEADS * HEAD_DIM].set(jnp.tile(k_gain.astype(F32)[hp], GA_KV_HEADS))
    cs = cs.at[WQ:WQ + WA_HEADS * HEAD_DIM].set(scale * LOG2_E)
    rm = jnp.zeros((ncol,), F32).at[GQ:GQ + GA_HEADS * HEAD_DIM].set(1.0)
    rm = rm.at[GK:GK + GA_KV_HEADS * HEAD_DIM].set(1.0)
    return cs, rm


def kernel(x, c, ctx, c_ctx, ln1_g, ln2_g, w_ada, b_ada, w_in, ret_decay_logit, ga_q_gain,
           ga_k_gain, wa_sink, w_branch, w_out, w_router, b_router, w_gu, b_gu, w_down, b_down,
           final_g):
    batch, seq, d = x.shape
    n_ctx = ctx.shape[1]
    depth = w_ada.shape[0]
    n_experts = w_router.shape[2]
    ncol = w_in.shape[2]
    mx, mc = batch * seq, batch * n_ctx
    m = mx + mc
    tm = min(1024, mc)
    assert seq % tm == 0 and mc % tm == 0 and batch + 1 <= 8
    tm_stream = min(512, tm)
    assert w_gu.shape[3] // 2 == w_down.shape[3] == d

    def mod_idx_for(tile):
        def mod_idx(i):
            return jnp.where(i < mx // tile, i // (seq // tile), batch)
        return mod_idx

    mod_idx = mod_idx_for(tm)
    mod_idx_s = mod_idx_for(tm_stream)

    cc8 = jnp.zeros((8, d), F32).at[:batch].set(c).at[batch].set(c_ctx)
    mods = _modulation(cc8, w_ada, b_ada)
    cos_f, sin_f = _rope_tables(batch, seq, n_ctx)

    x_all = jnp.concatenate([x.reshape(mx, d), ctx.reshape(mc, d)], axis=0)
    e_tm = 512
    tf = min(1024, d)

    for l in range(depth):
        need_ctx = l < depth - 1
        rows = m if need_ctx else mx
        mods3 = mods[l].reshape(8, 1, 6 * d)
        w_in_p = _regroup_columns(w_in[l])
        colscale, rmsmask = _column_params(ga_q_gain[l], ga_k_gain[l], ncol)
        if l == 0:
            h1 = _prenorm(x_all, mods3, ln1_g[l], tm=tm_stream, mod_idx=mod_idx_s)
        u = _in_proj(h1, w_in_p, cos_f, sin_f, colscale, rmsmask)
        o_ret = _retention(u, _retention_tables(ret_decay_logit[l]), batch=batch, seq=seq, n_ctx=n_ctx)
        o_ga = _global_attention(u, batch=batch, seq=seq, n_ctx=n_ctx, with_ctx=need_ctx)
        o_wa = _window_attention(u, wa_sink[l], batch=batch, seq=seq, n_ctx=n_ctx, with_ctx=need_ctx)
        mix = _merge(o_ret, o_ga, o_wa, u, w_branch[l].astype(BF16), rows=rows, tm=tm, d=d)
        x_new = _out_proj(mix, w_out[l].astype(BF16), x_all, mods3, rows=rows, tm=tm, d=d, mod_idx=mod_idx)
        h2, idx, wts = _router(x_new, mods3, ln2_g[l], w_router[l], b_router[l],
                               rows=rows, tm=tm_stream, d=d, mod_idx=mod_idx_s)
        (sched_up, sched_down), src_tok, pos = _moe_schedule(
            idx, n_experts=n_experts, tm=e_tm, n_chunks=(d // tf, 1))
        xs = h2.at[src_tok].get(mode="promise_in_bounds")
        act = _moe_up(sched_up, xs, w_gu, b_gu, layer=l, tm=e_tm, tf=tf)
        ys = _moe_down(sched_down, act, w_down, b_down, layer=l, tm=e_tm, tn=d)
        ysg = ys.at[pos].get(mode="promise_in_bounds")
        if need_ctx:
            x_all, h1 = _combine(x_new, ysg, wts, mods3, ln1_g[l + 1],
                                 mods[l + 1].reshape(8, 1, 6 * d), rows=rows, tm=tm_stream, d=d,
                                 mod_idx=mod_idx_s, final=False)
        else:
            (x_all,) = _combine(x_new, ysg, wts, mods3, final_g, mods3, rows=rows, tm=tm_stream,
                                d=d, mod_idx=mod_idx_s, final=True)
    return x_all[:mx].reshape(batch, seq, d)
```

```python
import functools

import jax
import jax.numpy as jnp
from jax import lax
from jax.experimental import pallas as pl
from jax.experimental.pallas import tpu as pltpu

GRID_W = 64
HEAD_DIM = 128
RET_HEADS = 4
RET_DK = 128
RET_DV = 256
GA_HEADS = 8
GA_KV_HEADS = 2
WA_HEADS = 8
WA_KV_HEADS = 2
WINDOW = 128
N_BRANCHES = 3
TOP_K = 4
SWIGLU_LIMIT = 7.0
SWIGLU_ALPHA = 1.702
ROPE_THETA = 10000.0
NORM_EPS = 1e-6
LOG2_E = 1.4426950408889634
MASK_BIAS = -1e30

LANES = 128
CHUNK = 256
GROUP = GA_HEADS // GA_KV_HEADS
IN_PROJ_MAX_ROWS = 1152
MOE_TILE_PARTS = 4
GA_TQ = 256
VMEM_LIMIT = 56 * 1024 * 1024

RQ, RK, GQ, WQ, GK, WK = 0, 512, 1024, 2048, 3072, 3328
ROPE_COLS = 3584
GV, WV, RV, RG, GATE = 3584, 3840, 4096, 5120, 6144
_ORQ, _ORK, _ORV, _ORG, _OGQ, _OGK, _OGV, _OWQ, _OWK, _OWV, _OGATE = (
    0, 512, 1024, 2048, 3072, 4096, 4352, 4608, 5632, 5888, 6144)

F32 = jnp.float32
BF16 = jnp.bfloat16


def _cparams(sem, vmem=VMEM_LIMIT):
    return pltpu.CompilerParams(dimension_semantics=sem, vmem_limit_bytes=vmem)


def _head_perm():
    j = jnp.arange(HEAD_DIM)
    return jnp.where(j < HEAD_DIM // 2, 2 * j, 2 * (j - HEAD_DIM // 2) + 1)


def _regroup_columns(w):
    d_model = w.shape[0]
    half = HEAD_DIM // 2

    def rope_heads(start, n_heads):
        blk = w[:, start:start + n_heads * HEAD_DIM].reshape(d_model, n_heads, half, 2)
        return blk.transpose(0, 1, 3, 2).reshape(d_model, n_heads * HEAD_DIM)

    def plain(start, size):
        return w[:, start:start + size]

    return jnp.concatenate([
        rope_heads(_ORQ, RET_HEADS), rope_heads(_ORK, RET_HEADS),
        rope_heads(_OGQ, GA_HEADS), rope_heads(_OWQ, WA_HEADS),
        rope_heads(_OGK, GA_KV_HEADS), rope_heads(_OWK, WA_KV_HEADS),
        plain(_OGV, GA_KV_HEADS * HEAD_DIM), plain(_OWV, WA_KV_HEADS * HEAD_DIM),
        plain(_ORV, RET_HEADS * RET_DV), plain(_ORG, RET_HEADS * RET_DV),
        plain(_OGATE, w.shape[1] - _OGATE)], axis=1).astype(BF16)


def _mod_kernel(c_ref, w_ref, b_ref, o_ref):
    c = c_ref[...]
    s = (c * jax.nn.sigmoid(c)).astype(BF16)
    o_ref[...] = jnp.dot(s, w_ref[...].astype(BF16), preferred_element_type=F32) + b_ref[...]


def _modulation(cc8, w_ada, b_ada):
    n_layers, d, n = w_ada.shape
    tn = min(1024, n)
    return pl.pallas_call(
        _mod_kernel,
        grid=(n_layers, n // tn),
        in_specs=[pl.BlockSpec((8, d), lambda l, j: (0, 0)),
                  pl.BlockSpec((None, d, tn), lambda l, j: (l, 0, j)),
                  pl.BlockSpec((None, 1, tn), lambda l, j: (l, 0, j))],
        out_specs=pl.BlockSpec((None, 8, tn), lambda l, j: (l, 0, j)),
        out_shape=jax.ShapeDtypeStruct((n_layers, 8, n), F32),
        compiler_params=_cparams(("arbitrary", "arbitrary")),
        name="modulation",
    )(cc8, w_ada, b_ada.reshape(n_layers, 1, n))


def _prenorm_kernel(x_ref, sh_ref, sc_ref, g_ref, h_ref):
    x = x_ref[...]
    y = x * lax.rsqrt(jnp.mean(x * x, axis=-1, keepdims=True) + NORM_EPS) * g_ref[...]
    h_ref[...] = (y * (1.0 + sc_ref[...]) + sh_ref[...]).astype(BF16)


def _prenorm(x_all, mods3, ln_g, *, tm, mod_idx):
    m, d = x_all.shape
    return pl.pallas_call(
        _prenorm_kernel,
        grid=(m // tm,),
        in_specs=[pl.BlockSpec((tm, d), lambda i: (i, 0)),
                  pl.BlockSpec((None, 1, d), lambda i: (mod_idx(i), 0, 0)),
                  pl.BlockSpec((None, 1, d), lambda i: (mod_idx(i), 0, 1)),
                  pl.BlockSpec((1, d), lambda i: (0, 0))],
        out_specs=pl.BlockSpec((tm, d), lambda i: (i, 0)),
        out_shape=jax.ShapeDtypeStruct((m, d), BF16),
        compiler_params=_cparams(("arbitrary",)),
        name="prenorm",
    )(x_all, mods3, mods3, ln_g.reshape(1, d))


def _in_proj_kernel(h_ref, w_ref, cos_ref, sin_ref, cs_ref, rm_ref, o_ref, acc_ref, *, n_tiles, tn):
    s = pl.program_id(0)

    @pl.when(s == 0)
    def _():
        acc_ref[1] = jnp.zeros(acc_ref.shape[1:], F32)

    cur = s % 2
    prev = acc_ref[1 - cur]
    col0 = (jnp.maximum(s - 1, 0) % n_tiles) * tn
    cos = cos_ref[...]
    sin = sin_ref[...]
    for hh in range(tn // HEAD_DIM):
        sl = slice(hh * HEAD_DIM, (hh + 1) * HEAD_DIM)
        raw = prev[:, sl]
        r = lax.rsqrt(jnp.mean(raw * raw, axis=-1, keepdims=True) + NORM_EPS)
        rm = rm_ref[:, sl]
        t = raw * (rm * r + (1.0 - rm)) * cs_ref[:, sl]
        roped = t * cos + pltpu.roll(t, HEAD_DIM // 2, 1) * sin
        is_rope = col0 + hh * HEAD_DIM < ROPE_COLS
        o_ref[:, sl] = jnp.where(is_rope, roped, raw).astype(BF16)
    acc_ref[cur] = jnp.dot(h_ref[...], w_ref[...], preferred_element_type=F32)


def _in_proj(h, w_in_p, cos_f, sin_f, colscale, rmsmask):
    m, d = h.shape
    ncol = w_in_p.shape[1]
    tn = 768 if ncol % 768 == 0 else 512
    tm = max(t for t in range(LANES, IN_PROJ_MAX_ROWS + 1, LANES) if m % t == 0)
    n_tiles = ncol // tn
    n_rows = m // tm
    n_steps = n_rows * n_tiles + 1

    def row(s):
        return jnp.minimum(s // n_tiles, n_rows - 1)

    def col(s):
        return jnp.where(s < n_steps - 1, s % n_tiles, n_tiles - 1)

    def prow(s):
        return jnp.maximum(s - 1, 0) // n_tiles

    def pcol(s):
        return jnp.maximum(s - 1, 0) % n_tiles

    kern = functools.partial(_in_proj_kernel, n_tiles=n_tiles, tn=tn)
    return pl.pallas_call(
        kern,
        grid=(n_steps,),
        in_specs=[pl.BlockSpec((tm, d), lambda s: (row(s), 0)),
                  pl.BlockSpec((d, tn), lambda s: (0, col(s))),
                  pl.BlockSpec((tm, HEAD_DIM), lambda s: (prow(s), 0)),
                  pl.BlockSpec((tm, HEAD_DIM), lambda s: (prow(s), 0)),
                  pl.BlockSpec((1, tn), lambda s: (0, pcol(s))),
                  pl.BlockSpec((1, tn), lambda s: (0, pcol(s)))],
        out_specs=pl.BlockSpec((tm, tn), lambda s: (prow(s), pcol(s))),
        out_shape=jax.ShapeDtypeStruct((m, ncol), BF16),
        scratch_shapes=[pltpu.VMEM((2, tm, tn), F32)],
        compiler_params=_cparams(("arbitrary",)),
        name="in_proj",
    )(h, w_in_p, cos_f, sin_f, colscale.reshape(1, ncol), rmsmask.reshape(1, ncol))


def _retention_kernel(q_ref, k_ref, v_ref, g_ref, dm_ref, qd_ref, kd_ref, cd_ref,
                      o_ref, s_ref, ob_ref, *, n_lat_chunks):
    d = pl.program_id(1)
    t = pl.program_id(2)

    @pl.when(t == 0)
    def _():
        s_ref[...] = jnp.zeros_like(s_ref)

    cid = jnp.where(d == 1, t, jnp.where(t == 0, 0, 1 + n_lat_chunks - t))
    outs = []
    for h in range(RET_HEADS):
        q = q_ref[:, h * RET_DK:(h + 1) * RET_DK]
        k = k_ref[:, h * RET_DK:(h + 1) * RET_DK]
        v = v_ref[:, h * RET_DV:(h + 1) * RET_DV]
        state = s_ref[h]
        scores = lax.dot_general(q, k, (((1,), (1,)), ((), ())), preferred_element_type=F32)
        p = (scores * dm_ref[h]).astype(BF16)
        qd = (q.astype(F32) * qd_ref[h]).astype(BF16)
        outs.append(jnp.dot(p, v, preferred_element_type=F32)
                    + jnp.dot(qd, state.astype(BF16), preferred_element_type=F32))
        kd = (k.astype(F32) * kd_ref[h]).astype(BF16)
        s_ref[h] = state * cd_ref[h] + lax.dot_general(
            kd, v, (((0,), (0,)), ((), ())), preferred_element_type=F32)

    @pl.when(d == 0)
    def _():
        for h in range(RET_HEADS):
            ob_ref[cid, :, h * RET_DV:(h + 1) * RET_DV] = outs[h]

    @pl.when(d == 1)
    def _():
        for h in range(RET_HEADS):
            sl = slice(h * RET_DV, (h + 1) * RET_DV)
            tot = outs[h] + ob_ref[cid, :, sl]
            tot = tot * lax.rsqrt(jnp.mean(tot * tot, axis=-1, keepdims=True) + NORM_EPS)
            g = g_ref[:, sl].astype(F32)
            o_ref[:, sl] = (tot * (g * jax.nn.sigmoid(g))).astype(BF16)


def _retention(u, tabs, *, batch, seq, n_ctx):
    m = u.shape[0]
    assert n_ctx == CHUNK and seq % CHUNK == 0
    nc = seq // CHUNK
    lat_blocks = batch * nc
    dmat, qdec, kdec, cdec = tabs

    def rowblk(b, d, t):
        lat = b * nc + jnp.where(d == 1, t - 1, nc - t)
        return jnp.where(t == 0, lat_blocks + b, lat)

    def in_map(col):
        return lambda b, d, t: (rowblk(b, d, t), col)

    def out_map(b, d, t):
        return (jnp.where(d == 1, rowblk(b, 1, t), lat_blocks + b), 0)

    def tab_map(b, d, t):
        return (d, 0, 0, 0)

    qk_w = RET_HEADS * RET_DK
    v_w = RET_HEADS * RET_DV
    kern = functools.partial(_retention_kernel, n_lat_chunks=nc)
    return pl.pallas_call(
        kern,
        grid=(batch, 2, nc + 1),
        in_specs=[pl.BlockSpec((CHUNK, qk_w), in_map(RQ // qk_w)),
                  pl.BlockSpec((CHUNK, qk_w), in_map(RK // qk_w)),
                  pl.BlockSpec((CHUNK, v_w), in_map(RV // v_w)),
                  pl.BlockSpec((CHUNK, v_w), in_map(RG // v_w)),
                  pl.BlockSpec((None, RET_HEADS, CHUNK, CHUNK), tab_map),
                  pl.BlockSpec((None, RET_HEADS, CHUNK, RET_DK), tab_map),
                  pl.BlockSpec((None, RET_HEADS, CHUNK, RET_DK), tab_map),
                  pl.BlockSpec((None, RET_HEADS, 1, RET_DV), tab_map)],
        out_specs=pl.BlockSpec((CHUNK, v_w), out_map),
        out_shape=jax.ShapeDtypeStruct((m, v_w), BF16),
        scratch_shapes=[pltpu.VMEM((RET_HEADS, RET_DK, RET_DV), F32),
                        pltpu.VMEM((nc + 1, CHUNK, v_w), F32)],
        compiler_params=_cparams(("arbitrary", "arbitrary", "arbitrary")),
        name="retention",
    )(u, u, u, u, dmat, qdec, kdec, cdec)


def _retention_tables(decay_logit):
    lg = jax.nn.log_sigmoid(decay_logit.astype(F32))
    lf, lb = lg[0][:, None, None], lg[1][:, None, None]
    pos = jnp.arange(CHUNK, dtype=F32)
    diff = pos[:, None] - pos[None, :]
    d_f = jnp.where(diff >= 0, jnp.exp(jnp.where(diff >= 0, diff, 0.0) * lf), 0.0)
    d_b = jnp.where(diff < 0, jnp.exp(jnp.where(diff < 0, -diff, 0.0) * lb), 0.0)
    ones_k = jnp.ones((1, 1, RET_DK), F32)
    q_f = jnp.exp((pos + 1.0)[None, :, None] * lf) * ones_k
    q_b = jnp.exp((CHUNK - pos)[None, :, None] * lb) * ones_k
    k_f = jnp.exp((CHUNK - 1.0 - pos)[None, :, None] * lf) * ones_k
    k_b = jnp.exp(pos[None, :, None] * lb) * ones_k
    ones_v = jnp.ones((1, 1, RET_DV), F32)
    c_f = jnp.exp(CHUNK * lf) * ones_v
    c_b = jnp.exp(CHUNK * lb) * ones_v
    return (jnp.stack([d_b, d_f]), jnp.stack([q_b, q_f]), jnp.stack([k_b, k_f]),
            jnp.stack([c_b, c_f]))


def _global_attn_kernel(q_ref, kx_ref, vx_ref, kc_ref, vc_ref, o_ref, vxe_ref, vce_ref, *, n_lat_tiles):
    i = pl.program_id(2)
    nt = (((1,), (1,)), ((), ()))

    @pl.when(i == 0)
    def _():
        vxe_ref[:, 0:HEAD_DIM] = vx_ref[...]
        vxe_ref[:, HEAD_DIM:] = jnp.ones_like(vx_ref)
        vce_ref[:, 0:HEAD_DIM] = vc_ref[...]
        vce_ref[:, HEAD_DIM:] = jnp.ones_like(vc_ref)

    def scores(h):
        q = q_ref[:, h * HEAD_DIM:(h + 1) * HEAD_DIM]
        return (lax.dot_general(q, kc_ref[...], nt, preferred_element_type=F32),
                lax.dot_general(q, kx_ref[...], nt, preferred_element_type=F32))

    def store(h, pv):
        o_ref[:, h * HEAD_DIM:(h + 1) * HEAD_DIM] = (pv[:, :HEAD_DIM] / pv[:, HEAD_DIM:]).astype(BF16)

    @pl.when(i < n_lat_tiles)
    def _():
        nxt = scores(0)
        for h in range(GROUP):
            s_c, s_x = nxt
            if h + 1 < GROUP:
                nxt = scores(h + 1)
            mx = jnp.maximum(jnp.max(s_c, axis=-1, keepdims=True), jnp.max(s_x, axis=-1, keepdims=True))
            p_c = jnp.exp2((s_c - mx).astype(BF16))
            p_x = jnp.exp2((s_x - mx).astype(BF16))
            store(h, jnp.dot(p_c, vce_ref[...], preferred_element_type=F32)
                  + jnp.dot(p_x, vxe_ref[...], preferred_element_type=F32))

    @pl.when(i >= n_lat_tiles)
    def _():
        for h in range(GROUP):
            s_c = lax.dot_general(q_ref[:, h * HEAD_DIM:(h + 1) * HEAD_DIM], kc_ref[...], nt,
                                  preferred_element_type=F32)
            p_c = jnp.exp2((s_c - jnp.max(s_c, axis=-1, keepdims=True)).astype(BF16))
            store(h, jnp.dot(p_c, vce_ref[...], preferred_element_type=F32))


def _global_attention(u, *, batch, seq, n_ctx, with_ctx):
    m = u.shape[0]
    tq = GA_TQ
    nq = seq // tq
    n_ctx_tiles = n_ctx // tq
    lat_blocks = batch * nq
    gw = GROUP * HEAD_DIM

    def qrow(b, i):
        return jnp.where(i < nq, b * nq + i, lat_blocks + b * n_ctx_tiles + (i - nq))

    kern = functools.partial(_global_attn_kernel, n_lat_tiles=nq)
    return pl.pallas_call(
        kern,
        grid=(batch, GA_KV_HEADS, nq + (n_ctx_tiles if with_ctx else 0)),
        in_specs=[pl.BlockSpec((tq, gw), lambda b, kh, i: (qrow(b, i), GQ // gw + kh)),
                  pl.BlockSpec((seq, HEAD_DIM), lambda b, kh, i: (b, GK // HEAD_DIM + kh)),
                  pl.BlockSpec((seq, HEAD_DIM), lambda b, kh, i: (b, GV // HEAD_DIM + kh)),
                  pl.BlockSpec((n_ctx, HEAD_DIM),
                               lambda b, kh, i: (batch * seq // n_ctx + b, GK // HEAD_DIM + kh)),
                  pl.BlockSpec((n_ctx, HEAD_DIM),
                               lambda b, kh, i: (batch * seq // n_ctx + b, GV // HEAD_DIM + kh))],
        out_specs=pl.BlockSpec((tq, gw), lambda b, kh, i: (qrow(b, i), kh)),
        out_shape=jax.ShapeDtypeStruct((m, GA_HEADS * HEAD_DIM), BF16),
        scratch_shapes=[pltpu.VMEM((seq, 2 * HEAD_DIM), BF16), pltpu.VMEM((n_ctx, 2 * HEAD_DIM), BF16)],
        compiler_params=_cparams(("arbitrary", "arbitrary", "arbitrary")),
        name="global_attention",
    )(u, u, u, u, u)


def _window_attn_kernel(sink_ref, q_ref, kx_ref, vx_ref, kc_ref, vc_ref, bias_ref, o_ref, vce_ref,
                        *, n_lat_tiles, seq):
    i = pl.program_id(1)
    span = CHUNK + 2 * WINDOW
    nt = (((1,), (1,)), ((), ()))

    @pl.when(i == 0)
    def _():
        for kh in range(WA_KV_HEADS):
            vce_ref[kh, :, 0:HEAD_DIM] = vc_ref[:, kh * HEAD_DIM:(kh + 1) * HEAD_DIM]
            vce_ref[kh, :, HEAD_DIM:] = jnp.ones((vc_ref.shape[0], HEAD_DIM), BF16)

    start = jnp.clip(i * CHUNK - WINDOW, 0, seq - span)
    start = pl.multiple_of(jnp.where(i < n_lat_tiles, start, 0), WINDOW)
    for kh in range(WA_KV_HEADS):
        kv = slice(kh * HEAD_DIM, (kh + 1) * HEAD_DIM)
        kw = kx_ref[pl.ds(start, span), kv]
        vw = vx_ref[pl.ds(start, span), kv]
        vwe = jnp.concatenate([vw, jnp.ones_like(vw)], axis=1)
        kc = kc_ref[:, kv]
        for h in range(kh * GROUP, (kh + 1) * GROUP):
            q = q_ref[:, h * HEAD_DIM:(h + 1) * HEAD_DIM]
            s_w = lax.dot_general(q, kw, nt, preferred_element_type=F32) + bias_ref[...]
            s_c = lax.dot_general(q, kc, nt, preferred_element_type=F32)
            sink = sink_ref[h]
            mx = jnp.maximum(jnp.maximum(jnp.max(s_w, axis=-1, keepdims=True),
                                         jnp.max(s_c, axis=-1, keepdims=True)), sink)
            p_w = jnp.exp2((s_w - mx).astype(BF16))
            p_c = jnp.exp2((s_c - mx).astype(BF16))
            pv = (jnp.dot(p_w, vwe, preferred_element_type=F32)
                  + jnp.dot(p_c, vce_ref[kh], preferred_element_type=F32))
            den = pv[:, HEAD_DIM:] + jnp.exp2(sink - mx)
            o_ref[:, h * HEAD_DIM:(h + 1) * HEAD_DIM] = (pv[:, :HEAD_DIM] / den).astype(BF16)


def _window_bias():
    span = CHUNK + 2 * WINDOW
    r = jnp.arange(CHUNK)[:, None]
    col = jnp.arange(span)[None, :]
    offs = (0, WINDOW, 2 * WINDOW)
    tiles = [jnp.where(jnp.abs(col - off - r) <= WINDOW, 0.0, MASK_BIAS) for off in offs]
    tiles.append(jnp.full((CHUNK, span), MASK_BIAS))
    return jnp.stack(tiles).astype(F32)


def _window_attention(u, sink, *, batch, seq, n_ctx, with_ctx):
    m = u.shape[0]
    nq = seq // CHUNK
    lat_blocks = batch * nq
    span = CHUNK + 2 * WINDOW
    assert seq >= span and nq >= 2

    def qrow(b, i):
        return jnp.where(i < nq, b * nq + i, lat_blocks + b)

    def variant(i):
        return jnp.where(i == 0, 0, jnp.where(i < nq - 1, 1, jnp.where(i == nq - 1, 2, 3)))

    qw = WA_HEADS * HEAD_DIM
    kvw = WA_KV_HEADS * HEAD_DIM
    kern = functools.partial(_window_attn_kernel, n_lat_tiles=nq, seq=seq)
    return pl.pallas_call(
        kern,
        grid=(batch, nq + (1 if with_ctx else 0)),
        in_specs=[pl.BlockSpec(memory_space=pltpu.SMEM),
                  pl.BlockSpec((CHUNK, qw), lambda b, i: (qrow(b, i), WQ // qw)),
                  pl.BlockSpec((seq, kvw), lambda b, i: (b, WK // kvw)),
                  pl.BlockSpec((seq, kvw), lambda b, i: (b, WV // kvw)),
                  pl.BlockSpec((n_ctx, kvw), lambda b, i: (batch * seq // n_ctx + b, WK // kvw)),
                  pl.BlockSpec((n_ctx, kvw), lambda b, i: (batch * seq // n_ctx + b, WV // kvw)),
                  pl.BlockSpec((None, CHUNK, span), lambda b, i: (variant(i), 0, 0))],
        out_specs=pl.BlockSpec((CHUNK, qw), lambda b, i: (qrow(b, i), 0)),
        out_shape=jax.ShapeDtypeStruct((m, qw), BF16),
        scratch_shapes=[pltpu.VMEM((WA_KV_HEADS, n_ctx, 2 * HEAD_DIM), BF16)],
        compiler_params=_cparams(("arbitrary", "arbitrary")),
        name="window_attention",
    )(sink.astype(F32) * LOG2_E, u, u, u, u, u, _window_bias())


def _merge_kernel(o0_ref, o1_ref, o2_ref, g0_ref, g1_ref, g2_ref, w0_ref, w1_ref, w2_ref, out_ref):
    tot = None
    for o_ref, g_ref, w_ref in ((o0_ref, g0_ref, w0_ref), (o1_ref, g1_ref, w1_ref),
                                (o2_ref, g2_ref, w2_ref)):
        term = jax.nn.sigmoid(g_ref[...].astype(F32)) * jnp.dot(
            o_ref[...], w_ref[...], preferred_element_type=F32)
        tot = term if tot is None else tot + term
    out_ref[...] = tot.astype(BF16)


def _merge(o_ret, o_ga, o_wa, u, w_branch, *, rows, tm, d):
    tn = 512
    bw = w_branch.shape[1]
    gate_blk = GATE // tn
    nd = d // tn

    def o_spec():
        return pl.BlockSpec((tm, bw), lambda i, j: (i, 0))

    def g_spec(br):
        return pl.BlockSpec((tm, tn), lambda i, j: (i, gate_blk + br * nd + j))

    def w_spec(br):
        return pl.BlockSpec((None, bw, tn), lambda i, j: (br, 0, j))

    return pl.pallas_call(
        _merge_kernel,
        grid=(rows // tm, nd),
        in_specs=[o_spec(), o_spec(), o_spec(), g_spec(0), g_spec(1), g_spec(2),
                  w_spec(0), w_spec(1), w_spec(2)],
        out_specs=pl.BlockSpec((tm, tn), lambda i, j: (i, j)),
        out_shape=jax.ShapeDtypeStruct((rows, d), BF16),
        compiler_params=_cparams(("arbitrary", "arbitrary")),
        name="branch_merge",
    )(o_ret, o_ga, o_wa, u, u, u, w_branch, w_branch, w_branch)


def _out_proj_kernel(m_ref, w_ref, x_ref, g_ref, o_ref):
    o_ref[...] = x_ref[...] + g_ref[...] * jnp.dot(m_ref[...], w_ref[...], preferred_element_type=F32)


def _out_proj(mix, w_out, x_all, mods3, *, rows, tm, d, mod_idx):
    tn = 1024 if d % 1024 == 0 else 512
    nd = d // tn
    return pl.pallas_call(
        _out_proj_kernel,
        grid=(rows // tm, nd),
        in_specs=[pl.BlockSpec((tm, d), lambda i, j: (i, 0)),
                  pl.BlockSpec((d, tn), lambda i, j: (0, j)),
                  pl.BlockSpec((tm, tn), lambda i, j: (i, j)),
                  pl.BlockSpec((None, 1, tn), lambda i, j: (mod_idx(i), 0, 2 * nd + j))],
        out_specs=pl.BlockSpec((tm, tn), lambda i, j: (i, j)),
        out_shape=jax.ShapeDtypeStruct((rows, d), F32),
        compiler_params=_cparams(("arbitrary", "arbitrary")),
        name="out_proj",
    )(mix, w_out, x_all, mods3)


def _router_kernel(x_ref, sh_ref, sc_ref, g_ref, wr_ref, br_ref, h_ref, idx_ref, wt_ref, *, n_experts):
    x = x_ref[...]
    y = x * lax.rsqrt(jnp.mean(x * x, axis=-1, keepdims=True) + NORM_EPS) * g_ref[...]
    h = y * (1.0 + sc_ref[...]) + sh_ref[...]
    h_hi = h.astype(BF16)
    h_ref[...] = h_hi
    h_lo = (h - h_hi.astype(F32)).astype(BF16)
    w = wr_ref[...]
    w_hi = w.astype(BF16)
    w_lo = (w - w_hi.astype(F32)).astype(BF16)
    logits = (jnp.dot(h_hi, w_hi, preferred_element_type=F32)
              + jnp.dot(h_lo, w_hi, preferred_element_type=F32)
              + jnp.dot(h_hi, w_lo, preferred_element_type=F32)) + br_ref[...]
    lane = lax.broadcasted_iota(jnp.int32, logits.shape, 1).astype(F32)
    vals, ids = [], []
    cur = logits
    for _ in range(TOP_K):
        mx = jnp.max(cur, axis=-1, keepdims=True)
        sel = jnp.min(jnp.where(cur == mx, lane, float(n_experts)), axis=-1, keepdims=True)
        vals.append(mx)
        ids.append(sel)
        cur = jnp.where(lane == sel, -jnp.inf, cur)
    e = [jnp.exp(v - vals[0]) for v in vals]
    den = e[0] + e[1] + e[2] + e[3]
    for k in range(TOP_K):
        idx_ref[:, k:k + 1] = ids[k].astype(jnp.int32)
        wt_ref[:, k:k + 1] = e[k] / den


def _router(x_new, mods3, ln_g, w_router, b_router, *, rows, tm, d, mod_idx):
    n_experts = w_router.shape[1]
    kern = functools.partial(_router_kernel, n_experts=n_experts)
    return pl.pallas_call(
        kern,
        grid=(rows // tm,),
        in_specs=[pl.BlockSpec((tm, d), lambda i: (i, 0)),
                  pl.BlockSpec((None, 1, d), lambda i: (mod_idx(i), 0, 3)),
                  pl.BlockSpec((None, 1, d), lambda i: (mod_idx(i), 0, 4)),
                  pl.BlockSpec((1, d), lambda i: (0, 0)),
                  pl.BlockSpec((d, n_experts), lambda i: (0, 0)),
                  pl.BlockSpec((1, n_experts), lambda i: (0, 0))],
        out_specs=[pl.BlockSpec((tm, d), lambda i: (i, 0)),
                   pl.BlockSpec((tm, TOP_K), lambda i: (i, 0)),
                   pl.BlockSpec((tm, TOP_K), lambda i: (i, 0))],
        out_shape=[jax.ShapeDtypeStruct((rows, d), BF16),
                   jax.ShapeDtypeStruct((rows, TOP_K), jnp.int32),
                   jax.ShapeDtypeStruct((rows, TOP_K), F32)],
        compiler_params=_cparams(("arbitrary",)),
        name="norm2_router",
    )(x_new, mods3, mods3, ln_g.reshape(1, d), w_router, b_router.reshape(1, n_experts))


def _moe_up_kernel(ie_ref, ic_ref, it_ref, in_ref, x_ref, wg_ref, wl_ref, bg_ref, bl_ref, o_ref):
    i = pl.program_id(0)

    def compute(n):
        x = x_ref[0:n, :]
        glu = jnp.dot(x, wg_ref[...].astype(BF16), preferred_element_type=F32) + bg_ref[...]
        lin = jnp.dot(x, wl_ref[...].astype(BF16), preferred_element_type=F32) + bl_ref[...]
        glu = jnp.minimum(glu, SWIGLU_LIMIT)
        lin = jnp.clip(lin, -SWIGLU_LIMIT, SWIGLU_LIMIT)
        o_ref[0:n, :] = (glu * jax.nn.sigmoid(SWIGLU_ALPHA * glu) * (lin + 1.0)).astype(BF16)

    tm = x_ref.shape[0]
    for quarters in range(1, MOE_TILE_PARTS + 1):
        n = quarters * tm // MOE_TILE_PARTS
        pl.when(in_ref[i] == n)(functools.partial(compute, n))


def _moe_up(sched, xs, w_gu, b_gu, *, layer, tm, tf):
    r_pad, d = xs.shape
    n_layers, n_experts, _, f2 = w_gu.shape
    f = f2 // 2
    nc = f // tf
    n_items = sched[0].shape[0]
    grid_spec = pltpu.PrefetchScalarGridSpec(
        num_scalar_prefetch=4,
        grid=(n_items,),
        in_specs=[pl.BlockSpec((tm, d), lambda i, ie, ic, it, nr: (it[i], 0)),
                  pl.BlockSpec((None, None, d, tf), lambda i, ie, ic, it, nr: (layer, ie[i], 0, ic[i])),
                  pl.BlockSpec((None, None, d, tf),
                               lambda i, ie, ic, it, nr: (layer, ie[i], 0, nc + ic[i])),
                  pl.BlockSpec((None, None, 1, tf), lambda i, ie, ic, it, nr: (layer, ie[i], 0, ic[i])),
                  pl.BlockSpec((None, None, 1, tf),
                               lambda i, ie, ic, it, nr: (layer, ie[i], 0, nc + ic[i]))],
        out_specs=pl.BlockSpec((tm, tf), lambda i, ie, ic, it, nr: (it[i], ic[i])))
    b4 = b_gu.reshape(n_layers, n_experts, 1, f2)
    return pl.pallas_call(
        _moe_up_kernel,
        grid_spec=grid_spec,
        out_shape=jax.ShapeDtypeStruct((r_pad, f), BF16),
        compiler_params=_cparams(("arbitrary",)),
        name="moe_up",
    )(*sched, xs, w_gu, w_gu, b4, b4)


def _moe_down_kernel(ie_ref, ic_ref, it_ref, in_ref, a_ref, w_ref, b_ref, o_ref):
    i = pl.program_id(0)

    def compute(n):
        y = jnp.dot(a_ref[0:n, :], w_ref[...].astype(BF16), preferred_element_type=F32) + b_ref[...]
        o_ref[0:n, :] = y.astype(BF16)

    tm = a_ref.shape[0]
    for quarters in range(1, MOE_TILE_PARTS + 1):
        n = quarters * tm // MOE_TILE_PARTS
        pl.when(in_ref[i] == n)(functools.partial(compute, n))


def _moe_down(sched, act, w_down, b_down, *, layer, tm, tn):
    r_pad, f = act.shape
    n_layers, n_experts, _, d = w_down.shape
    n_items = sched[0].shape[0]
    grid_spec = pltpu.PrefetchScalarGridSpec(
        num_scalar_prefetch=4,
        grid=(n_items,),
        in_specs=[pl.BlockSpec((tm, f), lambda i, ie, ic, it, nr: (it[i], 0)),
                  pl.BlockSpec((None, None, f, tn), lambda i, ie, ic, it, nr: (layer, ie[i], 0, ic[i])),
                  pl.BlockSpec((None, None, 1, tn), lambda i, ie, ic, it, nr: (layer, ie[i], 0, ic[i]))],
        out_specs=pl.BlockSpec((tm, tn), lambda i, ie, ic, it, nr: (it[i], ic[i])))
    return pl.pallas_call(
        _moe_down_kernel,
        grid_spec=grid_spec,
        out_shape=jax.ShapeDtypeStruct((r_pad, d), BF16),
        compiler_params=_cparams(("arbitrary",)),
        name="moe_down",
    )(*sched, act, w_down, b_down.reshape(n_layers, n_experts, 1, d))


def _moe_schedule(idx, *, n_experts, tm, n_chunks):
    n_tok = idx.shape[0]
    n_pairs = n_tok * TOP_K
    r_pad = n_pairs + n_experts * tm
    n_tiles = r_pad // tm
    part = tm // MOE_TILE_PARTS
    flat_e = idx.reshape(-1)
    experts = jnp.arange(n_experts, dtype=jnp.int32)
    onehot = (flat_e[:, None] == experts[None, :]).astype(jnp.int32)
    csum = jnp.cumsum(onehot, axis=0)
    rank = jnp.sum(csum * onehot, axis=1) - 1
    counts = csum[-1]
    ntiles = (counts + tm - 1) // tm
    tile_end = jnp.cumsum(ntiles)
    tile_start = tile_end - ntiles
    dest = jnp.sum(onehot * tile_start[None, :], axis=1) * tm + rank
    src_tok = (jnp.arange(r_pad, dtype=jnp.int32) % n_tok).at[dest].set(
        jnp.arange(n_pairs, dtype=jnp.int32) // TOP_K, unique_indices=True, mode="promise_in_bounds")
    n_used = tile_end[-1]

    def schedule(n_chunks):
        n_items = n_chunks * n_tiles
        item = jnp.arange(n_items, dtype=jnp.int32)
        valid = item < n_chunks * n_used
        ic_ = jnp.minimum(item, jnp.maximum(n_chunks * n_used - 1, 0))
        e = jnp.minimum(
            jnp.sum((ic_[:, None] >= n_chunks * tile_end[None, :]).astype(jnp.int32), axis=1),
            n_experts - 1)
        sel = (e[:, None] == experts[None, :]).astype(jnp.int32)
        nt_e = jnp.maximum(jnp.sum(sel * ntiles[None, :], axis=1), 1)
        ts_e = jnp.sum(sel * tile_start[None, :], axis=1)
        cnt_e = jnp.sum(sel * counts[None, :], axis=1)
        r = ic_ - n_chunks * ts_e
        c = r // nt_e
        lt = nt_e - 1 - r % nt_e
        t = ts_e + lt
        left = cnt_e - lt * tm
        nrows = jnp.where(valid, jnp.minimum((left + part - 1) // part * part, tm), 0)
        return (e.astype(jnp.int32), c.astype(jnp.int32), t.astype(jnp.int32),
                nrows.astype(jnp.int32))

    pos = dest.reshape(n_tok, TOP_K).T.reshape(-1)
    return [schedule(n) for n in n_chunks], src_tok, pos


def _combine_kernel(x_ref, y0_ref, y1_ref, y2_ref, y3_ref, w_ref, g_ref, ng_ref, sh_ref, sc_ref,
                    *o_refs, final):
    w = w_ref[...]
    y = None
    for k, y_ref in enumerate((y0_ref, y1_ref, y2_ref, y3_ref)):
        term = w[:, k:k + 1] * y_ref[...].astype(F32)
        y = term if y is None else y + term
    x = x_ref[...] + g_ref[...] * y
    normed = x * lax.rsqrt(jnp.mean(x * x, axis=-1, keepdims=True) + NORM_EPS) * ng_ref[...]
    if final:
        o_refs[0][...] = normed
    else:
        o_refs[0][...] = x
        o_refs[1][...] = (normed * (1.0 + sc_ref[...]) + sh_ref[...]).astype(BF16)


def _combine(x_new, ysg, wts, mods3, norm_g, mods3_next, *, rows, tm, d, mod_idx, final):
    assert TOP_K == 4
    kern = functools.partial(_combine_kernel, final=final)
    nblk = rows // tm

    def y_spec(k):
        return pl.BlockSpec((tm, d), lambda i: (k * nblk + i, 0))

    row_spec = pl.BlockSpec((tm, d), lambda i: (i, 0))
    out_specs = [row_spec] if final else [row_spec, row_spec]
    out_shape = [jax.ShapeDtypeStruct((rows, d), F32)]
    if not final:
        out_shape.append(jax.ShapeDtypeStruct((rows, d), BF16))
    return pl.pallas_call(
        kern,
        grid=(nblk,),
        in_specs=[row_spec,
                  y_spec(0), y_spec(1), y_spec(2), y_spec(3),
                  pl.BlockSpec((tm, TOP_K), lambda i: (i, 0)),
                  pl.BlockSpec((None, 1, d), lambda i: (mod_idx(i), 0, 5)),
                  pl.BlockSpec((1, d), lambda i: (0, 0)),
                  pl.BlockSpec((None, 1, d), lambda i: (mod_idx(i), 0, 0)),
                  pl.BlockSpec((None, 1, d), lambda i: (mod_idx(i), 0, 1))],
        out_specs=out_specs,
        out_shape=out_shape,
        compiler_params=_cparams(("arbitrary",)),
        name="moe_combine",
    )(x_new, ysg, ysg, ysg, ysg, wts, mods3, norm_g.reshape(1, d), mods3_next, mods3_next)


def _rope_tables(batch, seq, n_ctx):
    rows = seq // GRID_W
    row = jnp.repeat(jnp.arange(rows, dtype=F32), GRID_W)
    col = jnp.tile(jnp.arange(GRID_W, dtype=F32), rows)
    quarter = HEAD_DIM // 4
    inv_freq = ROPE_THETA ** (-jnp.arange(quarter, dtype=F32) / quarter)
    ang = jnp.concatenate([row[:, None] * inv_freq, col[:, None] * inv_freq], axis=-1)
    cos, sin = jnp.cos(ang), jnp.sin(ang)
    cos_x = jnp.tile(jnp.concatenate([cos, cos], axis=-1), (batch, 1))
    sin_x = jnp.tile(jnp.concatenate([-sin, sin], axis=-1), (batch, 1))
    cos_f = jnp.concatenate([cos_x, jnp.ones((batch * n_ctx, HEAD_DIM), F32)], axis=0)
    sin_f = jnp.concatenate([sin_x, jnp.zeros((batch * n_ctx, HEAD_DIM), F32)], axis=0)
    return cos_f, sin_f


def _column_params(q_gain, k_gain, ncol):
    hp = _head_perm()
    scale = HEAD_DIM ** -0.5
    k_scale = RET_DK ** -0.5
    cs = jnp.ones((ncol,), F32)
    cs = cs.at[RK:RK + RET_HEADS * RET_DK].set(k_scale)
    cs = cs.at[GQ:GQ + GA_HEADS * HEAD_DIM].set(jnp.tile(q_gain.astype(F32)[hp] * (scale * LOG2_E), GA_HEADS))
    cs = cs.at[GK:GK + GA_KV_HEADS * HEAD_DIM].set(jnp.tile(k_gain.astype(F32)[hp], GA_KV_HEADS))
    cs = cs.at[WQ:WQ + WA_HEADS * HEAD_DIM].set(scale * LOG2_E)
    rm = jnp.zeros((ncol,), F32).at[GQ:GQ + GA_HEADS * HEAD_DIM].set(1.0)
    rm = rm.at[GK:GK + GA_KV_HEADS * HEAD_DIM].set(1.0)
    return cs, rm


def kernel(x, c, ctx, c_ctx, ln1_g, ln2_g, w_ada, b_ada, w_in, ret_decay_logit, ga_q_gain,
           ga_k_gain, wa_sink, w_branch, w_out, w_router, b_router, w_gu, b_gu, w_down, b_down,
           final_g):
    batch, seq, d = x.shape
    n_ctx = ctx.shape[1]
    depth = w_ada.shape[0]
    n_experts = w_router.shape[2]
    ncol = w_in.shape[2]
    mx, mc = batch * seq, batch * n_ctx
    m = mx + mc
    tm = min(1024, mc)
    assert seq % tm == 0 and mc % tm == 0 and batch + 1 <= 8
    tm_stream = min(512, tm)
    assert w_gu.shape[3] // 2 == w_down.shape[3] == d

    def mod_idx_for(tile):
        def mod_idx(i):
            return jnp.where(i < mx // tile, i // (seq // tile), batch)
        return mod_idx

    mod_idx = mod_idx_for(tm)
    mod_idx_s = mod_idx_for(tm_stream)

    cc8 = jnp.zeros((8, d), F32).at[:batch].set(c).at[batch].set(c_ctx)
    mods = _modulation(cc8, w_ada, b_ada)
    cos_f, sin_f = _rope_tables(batch, seq, n_ctx)

    x_all = jnp.concatenate([x.reshape(mx, d), ctx.reshape(mc, d)], axis=0)
    e_tm = 512
    tf = min(1024, d)

    for l in range(depth):
        need_ctx = l < depth - 1
        rows = m if need_ctx else mx
        mods3 = mods[l].reshape(8, 1, 6 * d)
        w_in_p = _regroup_columns(w_in[l])
        colscale, rmsmask = _column_params(ga_q_gain[l], ga_k_gain[l], ncol)
        if l == 0:
            h1 = _prenorm(x_all, mods3, ln1_g[l], tm=tm_stream, mod_idx=mod_idx_s)
        u = _in_proj(h1, w_in_p, cos_f, sin_f, colscale, rmsmask)
        o_ret = _retention(u, _retention_tables(ret_decay_logit[l]), batch=batch, seq=seq, n_ctx=n_ctx)
        o_ga = _global_attention(u, batch=batch, seq=seq, n_ctx=n_ctx, with_ctx=need_ctx)
        o_wa = _window_attention(u, wa_sink[l], batch=batch, seq=seq, n_ctx=n_ctx, with_ctx=need_ctx)
        mix = _merge(o_ret, o_ga, o_wa, u, w_branch[l].astype(BF16), rows=rows, tm=tm, d=d)
        x_new = _out_proj(mix, w_out[l].astype(BF16), x_all, mods3, rows=rows, tm=tm, d=d, mod_idx=mod_idx)
        h2, idx, wts = _router(x_new, mods3, ln2_g[l], w_router[l], b_router[l],
                               rows=rows, tm=tm, d=d, mod_idx=mod_idx)
        (sched_up, sched_down), src_tok, pos = _moe_schedule(
            idx, n_experts=n_experts, tm=e_tm, n_chunks=(d // tf, 1))
        xs = h2.at[src_tok].get(mode="promise_in_bounds")
        act = _moe_up(sched_up, xs, w_gu, b_gu, layer=l, tm=e_tm, tf=tf)
        ys = _moe_down(sched_down, act, w_down, b_down, layer=l, tm=e_tm, tn=d)
        ysg = ys.at[pos].get(mode="promise_in_bounds")
        if need_ctx:
            x_all, h1 = _combine(x_new, ysg, wts, mods3, ln1_g[l + 1],
                                 mods[l + 1].reshape(8, 1, 6 * d), rows=rows, tm=tm_stream, d=d,
                                 mod_idx=mod_idx_s, final=False)
        else:
            (x_all,) = _combine(x_new, ysg, wts, mods3, final_g, mods3, rows=rows, tm=tm_stream,
                                d=d, mod_idx=mod_idx_s, final=True)
    return x_all[:mx].reshape(batch, seq, d)
```

```python
import functools

import jax
import jax.numpy as jnp
from jax import lax
from jax.experimental import pallas as pl
from jax.experimental.pallas import tpu as pltpu

GRID_W = 64
HEAD_DIM = 128
RET_HEADS = 4
RET_DK = 128
RET_DV = 256
GA_HEADS = 8
GA_KV_HEADS = 2
WA_HEADS = 8
WA_KV_HEADS = 2
WINDOW = 128
N_BRANCHES = 3
TOP_K = 4
SWIGLU_LIMIT = 7.0
SWIGLU_ALPHA = 1.702
ROPE_THETA = 10000.0
NORM_EPS = 1e-6
LOG2_E = 1.4426950408889634
MASK_BIAS = -1e30

LANES = 128
CHUNK = 256
GROUP = GA_HEADS // GA_KV_HEADS
IN_PROJ_MAX_ROWS = 1152
MOE_TILE_PARTS = 4
GA_TQ = 256
VMEM_LIMIT = 56 * 1024 * 1024

RQ, RK, GQ, WQ, GK, WK = 0, 512, 1024, 2048, 3072, 3328
ROPE_COLS = 3584
GV, WV, RV, RG, GATE = 3584, 3840, 4096, 5120, 6144
_ORQ, _ORK, _ORV, _ORG, _OGQ, _OGK, _OGV, _OWQ, _OWK, _OWV, _OGATE = (
    0, 512, 1024, 2048, 3072, 4096, 4352, 4608, 5632, 5888, 6144)

F32 = jnp.float32
BF16 = jnp.bfloat16


def _cparams(sem, vmem=VMEM_LIMIT):
    return pltpu.CompilerParams(dimension_semantics=sem, vmem_limit_bytes=vmem)


def _head_perm():
    j = jnp.arange(HEAD_DIM)
    return jnp.where(j < HEAD_DIM // 2, 2 * j, 2 * (j - HEAD_DIM // 2) + 1)


def _regroup_columns(w):
    d_model = w.shape[0]
    half = HEAD_DIM // 2

    def rope_heads(start, n_heads):
        blk = w[:, start:start + n_heads * HEAD_DIM].reshape(d_model, n_heads, half, 2)
        return blk.transpose(0, 1, 3, 2).reshape(d_model, n_heads * HEAD_DIM)

    def plain(start, size):
        return w[:, start:start + size]

    return jnp.concatenate([
        rope_heads(_ORQ, RET_HEADS), rope_heads(_ORK, RET_HEADS),
        rope_heads(_OGQ, GA_HEADS), rope_heads(_OWQ, WA_HEADS),
        rope_heads(_OGK, GA_KV_HEADS), rope_heads(_OWK, WA_KV_HEADS),
        plain(_OGV, GA_KV_HEADS * HEAD_DIM), plain(_OWV, WA_KV_HEADS * HEAD_DIM),
        plain(_ORV, RET_HEADS * RET_DV), plain(_ORG, RET_HEADS * RET_DV),
        plain(_OGATE, w.shape[1] - _OGATE)], axis=1).astype(BF16)


def _mod_kernel(c_ref, w_ref, b_ref, o_ref):
    c = c_ref[...]
    s = (c * jax.nn.sigmoid(c)).astype(BF16)
    o_ref[...] = jnp.dot(s, w_ref[...].astype(BF16), preferred_element_type=F32) + b_ref[...]


def _modulation(cc8, w_ada, b_ada):
    n_layers, d, n = w_ada.shape
    tn = min(1024, n)
    return pl.pallas_call(
        _mod_kernel,
        grid=(n_layers, n // tn),
        in_specs=[pl.BlockSpec((8, d), lambda l, j: (0, 0)),
                  pl.BlockSpec((None, d, tn), lambda l, j: (l, 0, j)),
                  pl.BlockSpec((None, 1, tn), lambda l, j: (l, 0, j))],
        out_specs=pl.BlockSpec((None, 8, tn), lambda l, j: (l, 0, j)),
        out_shape=jax.ShapeDtypeStruct((n_layers, 8, n), F32),
        compiler_params=_cparams(("arbitrary", "arbitrary")),
        name="modulation",
    )(cc8, w_ada, b_ada.reshape(n_layers, 1, n))


def _prenorm_kernel(x_ref, sh_ref, sc_ref, g_ref, h_ref):
    x = x_ref[...]
    y = x * lax.rsqrt(jnp.mean(x * x, axis=-1, keepdims=True) + NORM_EPS) * g_ref[...]
    h_ref[...] = (y * (1.0 + sc_ref[...]) + sh_ref[...]).astype(BF16)


def _prenorm(x_all, mods3, ln_g, *, tm, mod_idx):
    m, d = x_all.shape
    return pl.pallas_call(
        _prenorm_kernel,
        grid=(m // tm,),
        in_specs=[pl.BlockSpec((tm, d), lambda i: (i, 0)),
                  pl.BlockSpec((None, 1, d), lambda i: (mod_idx(i), 0, 0)),
                  pl.BlockSpec((None, 1, d), lambda i: (mod_idx(i), 0, 1)),
                  pl.BlockSpec((1, d), lambda i: (0, 0))],
        out_specs=pl.BlockSpec((tm, d), lambda i: (i, 0)),
        out_shape=jax.ShapeDtypeStruct((m, d), BF16),
        compiler_params=_cparams(("arbitrary",)),
        name="prenorm",
    )(x_all, mods3, mods3, ln_g.reshape(1, d))


def _in_proj_kernel(h_ref, w_ref, cos_ref, sin_ref, cs_ref, rm_ref, o_ref, acc_ref, *, n_tiles, tn):
    s = pl.program_id(0)

    @pl.when(s == 0)
    def _():
        acc_ref[1] = jnp.zeros(acc_ref.shape[1:], F32)

    cur = s % 2
    prev = acc_ref[1 - cur]
    col0 = (jnp.maximum(s - 1, 0) % n_tiles) * tn
    cos = cos_ref[...]
    sin = sin_ref[...]
    for hh in range(tn // HEAD_DIM):
        sl = slice(hh * HEAD_DIM, (hh + 1) * HEAD_DIM)
        raw = prev[:, sl]
        r = lax.rsqrt(jnp.mean(raw * raw, axis=-1, keepdims=True) + NORM_EPS)
        rm = rm_ref[:, sl]
        t = raw * (rm * r + (1.0 - rm)) * cs_ref[:, sl]
        roped = t * cos + pltpu.roll(t, HEAD_DIM // 2, 1) * sin
        is_rope = col0 + hh * HEAD_DIM < ROPE_COLS
        o_ref[:, sl] = jnp.where(is_rope, roped, raw).astype(BF16)
    acc_ref[cur] = jnp.dot(h_ref[...], w_ref[...], preferred_element_type=F32)


def _in_proj(h, w_in_p, cos_f, sin_f, colscale, rmsmask):
    m, d = h.shape
    ncol = w_in_p.shape[1]
    tn = 768 if ncol % 768 == 0 else 512
    tm = max(t for t in range(LANES, IN_PROJ_MAX_ROWS + 1, LANES) if m % t == 0)
    n_tiles = ncol // tn
    n_rows = m // tm
    n_steps = n_rows * n_tiles + 1

    def row(s):
        return jnp.minimum(s // n_tiles, n_rows - 1)

    def col(s):
        return jnp.where(s < n_steps - 1, s % n_tiles, n_tiles - 1)

    def prow(s):
        return jnp.maximum(s - 1, 0) // n_tiles

    def pcol(s):
        return jnp.maximum(s - 1, 0) % n_tiles

    kern = functools.partial(_in_proj_kernel, n_tiles=n_tiles, tn=tn)
    return pl.pallas_call(
        kern,
        grid=(n_steps,),
        in_specs=[pl.BlockSpec((tm, d), lambda s: (row(s), 0)),
                  pl.BlockSpec((d, tn), lambda s: (0, col(s))),
                  pl.BlockSpec((tm, HEAD_DIM), lambda s: (prow(s), 0)),
                  pl.BlockSpec((tm, HEAD_DIM), lambda s: (prow(s), 0)),
                  pl.BlockSpec((1, tn), lambda s: (0, pcol(s))),
                  pl.BlockSpec((1, tn), lambda s: (0, pcol(s)))],
        out_specs=pl.BlockSpec((tm, tn), lambda s: (prow(s), pcol(s))),
        out_shape=jax.ShapeDtypeStruct((m, ncol), BF16),
        scratch_shapes=[pltpu.VMEM((2, tm, tn), F32)],
        compiler_params=_cparams(("arbitrary",)),
        name="in_proj",
    )(h, w_in_p, cos_f, sin_f, colscale.reshape(1, ncol), rmsmask.reshape(1, ncol))


def _retention_kernel(q_ref, k_ref, v_ref, g_ref, dm_ref, qd_ref, kd_ref, cd_ref,
                      o_ref, s_ref, ob_ref, *, n_lat_chunks):
    d = pl.program_id(1)
    t = pl.program_id(2)

    @pl.when(t == 0)
    def _():
        s_ref[...] = jnp.zeros_like(s_ref)

    cid = jnp.where(d == 1, t, jnp.where(t == 0, 0, 1 + n_lat_chunks - t))
    outs = []
    for h in range(RET_HEADS):
        q = q_ref[:, h * RET_DK:(h + 1) * RET_DK]
        k = k_ref[:, h * RET_DK:(h + 1) * RET_DK]
        v = v_ref[:, h * RET_DV:(h + 1) * RET_DV]
        state = s_ref[h]
        scores = lax.dot_general(q, k, (((1,), (1,)), ((), ())), preferred_element_type=F32)
        p = (scores * dm_ref[h]).astype(BF16)
        qd = (q.astype(F32) * qd_ref[h]).astype(BF16)
        outs.append(jnp.dot(p, v, preferred_element_type=F32)
                    + jnp.dot(qd, state.astype(BF16), preferred_element_type=F32))
        kd = (k.astype(F32) * kd_ref[h]).astype(BF16)
        s_ref[h] = state * cd_ref[h] + lax.dot_general(
            kd, v, (((0,), (0,)), ((), ())), preferred_element_type=F32)

    @pl.when(d == 0)
    def _():
        for h in range(RET_HEADS):
            ob_ref[cid, :, h * RET_DV:(h + 1) * RET_DV] = outs[h]

    @pl.when(d == 1)
    def _():
        for h in range(RET_HEADS):
            sl = slice(h * RET_DV, (h + 1) * RET_DV)
            tot = outs[h] + ob_ref[cid, :, sl]
            tot = tot * lax.rsqrt(jnp.mean(tot * tot, axis=-1, keepdims=True) + NORM_EPS)
            g = g_ref[:, sl].astype(F32)
            o_ref[:, sl] = (tot * (g * jax.nn.sigmoid(g))).astype(BF16)


def _retention(u, tabs, *, batch, seq, n_ctx):
    m = u.shape[0]
    assert n_ctx == CHUNK and seq % CHUNK == 0
    nc = seq // CHUNK
    lat_blocks = batch * nc
    dmat, qdec, kdec, cdec = tabs

    def rowblk(b, d, t):
        lat = b * nc + jnp.where(d == 1, t - 1, nc - t)
        return jnp.where(t == 0, lat_blocks + b, lat)

    def in_map(col):
        return lambda b, d, t: (rowblk(b, d, t), col)

    def out_map(b, d, t):
        return (jnp.where(d == 1, rowblk(b, 1, t), lat_blocks + b), 0)

    def tab_map(b, d, t):
        return (d, 0, 0, 0)

    qk_w = RET_HEADS * RET_DK
    v_w = RET_HEADS * RET_DV
    kern = functools.partial(_retention_kernel, n_lat_chunks=nc)
    return pl.pallas_call(
        kern,
        grid=(batch, 2, nc + 1),
        in_specs=[pl.BlockSpec((CHUNK, qk_w), in_map(RQ // qk_w)),
                  pl.BlockSpec((CHUNK, qk_w), in_map(RK // qk_w)),
                  pl.BlockSpec((CHUNK, v_w), in_map(RV // v_w)),
                  pl.BlockSpec((CHUNK, v_w), in_map(RG // v_w)),
                  pl.BlockSpec((None, RET_HEADS, CHUNK, CHUNK), tab_map),
                  pl.BlockSpec((None, RET_HEADS, CHUNK, RET_DK), tab_map),
                  pl.BlockSpec((None, RET_HEADS, CHUNK, RET_DK), tab_map),
                  pl.BlockSpec((None, RET_HEADS, 1, RET_DV), tab_map)],
        out_specs=pl.BlockSpec((CHUNK, v_w), out_map),
        out_shape=jax.ShapeDtypeStruct((m, v_w), BF16),
        scratch_shapes=[pltpu.VMEM((RET_HEADS, RET_DK, RET_DV), F32),
                        pltpu.VMEM((nc + 1, CHUNK, v_w), F32)],
        compiler_params=_cparams(("arbitrary", "arbitrary", "arbitrary")),
        name="retention",
    )(u, u, u, u, dmat, qdec, kdec, cdec)


def _retention_tables(decay_logit):
    lg = jax.nn.log_sigmoid(decay_logit.astype(F32))
    lf, lb = lg[0][:, None, None], lg[1][:, None, None]
    pos = jnp.arange(CHUNK, dtype=F32)
    diff = pos[:, None] - pos[None, :]
    d_f = jnp.where(diff >= 0, jnp.exp(jnp.where(diff >= 0, diff, 0.0) * lf), 0.0)
    d_b = jnp.where(diff < 0, jnp.exp(jnp.where(diff < 0, -diff, 0.0) * lb), 0.0)
    ones_k = jnp.ones((1, 1, RET_DK), F32)
    q_f = jnp.exp((pos + 1.0)[None, :, None] * lf) * ones_k
    q_b = jnp.exp((CHUNK - pos)[None, :, None] * lb) * ones_k
    k_f = jnp.exp((CHUNK - 1.0 - pos)[None, :, None] * lf) * ones_k
    k_b = jnp.exp(pos[None, :, None] * lb) * ones_k
    ones_v = jnp.ones((1, 1, RET_DV), F32)
    c_f = jnp.exp(CHUNK * lf) * ones_v
    c_b = jnp.exp(CHUNK * lb) * ones_v
    return (jnp.stack([d_b, d_f]), jnp.stack([q_b, q_f]), jnp.stack([k_b, k_f]),
            jnp.stack([c_b, c_f]))


def _global_attn_kernel(q_ref, kx_ref, vx_ref, kc_ref, vc_ref, o_ref, vxe_ref, vce_ref, *, n_lat_tiles):
    i = pl.program_id(1)
    nt = (((1,), (1,)), ((), ()))

    @pl.when(i == 0)
    def _():
        for kh in range(GA_KV_HEADS):
            kv = slice(kh * HEAD_DIM, (kh + 1) * HEAD_DIM)
            vxe_ref[kh, :, 0:HEAD_DIM] = vx_ref[:, kv]
            vxe_ref[kh, :, HEAD_DIM:] = jnp.ones((vx_ref.shape[0], HEAD_DIM), BF16)
            vce_ref[kh, :, 0:HEAD_DIM] = vc_ref[:, kv]
            vce_ref[kh, :, HEAD_DIM:] = jnp.ones((vc_ref.shape[0], HEAD_DIM), BF16)

    def scores(h):
        q = q_ref[:, h * HEAD_DIM:(h + 1) * HEAD_DIM]
        kv = slice((h // GROUP) * HEAD_DIM, (h // GROUP + 1) * HEAD_DIM)
        return (lax.dot_general(q, kc_ref[:, kv], nt, preferred_element_type=F32),
                lax.dot_general(q, kx_ref[:, kv], nt, preferred_element_type=F32))

    def store(h, pv):
        o_ref[:, h * HEAD_DIM:(h + 1) * HEAD_DIM] = (pv[:, :HEAD_DIM] / pv[:, HEAD_DIM:]).astype(BF16)

    @pl.when(i < n_lat_tiles)
    def _():
        nxt = scores(0)
        for h in range(GA_HEADS):
            s_c, s_x = nxt
            if h + 1 < GA_HEADS:
                nxt = scores(h + 1)
            mx = jnp.maximum(jnp.max(s_c, axis=-1, keepdims=True), jnp.max(s_x, axis=-1, keepdims=True))
            p_c = jnp.exp2((s_c - mx).astype(BF16))
            p_x = jnp.exp2((s_x - mx).astype(BF16))
            store(h, jnp.dot(p_c, vce_ref[h // GROUP], preferred_element_type=F32)
                  + jnp.dot(p_x, vxe_ref[h // GROUP], preferred_element_type=F32))

    @pl.when(i >= n_lat_tiles)
    def _():
        for h in range(GA_HEADS):
            kv = slice((h // GROUP) * HEAD_DIM, (h // GROUP + 1) * HEAD_DIM)
            s_c = lax.dot_general(q_ref[:, h * HEAD_DIM:(h + 1) * HEAD_DIM], kc_ref[:, kv], nt,
                                  preferred_element_type=F32)
            p_c = jnp.exp2((s_c - jnp.max(s_c, axis=-1, keepdims=True)).astype(BF16))
            store(h, jnp.dot(p_c, vce_ref[h // GROUP], preferred_element_type=F32))


def _global_attention(u, *, batch, seq, n_ctx, with_ctx):
    m = u.shape[0]
    tq = GA_TQ
    nq = seq // tq
    n_ctx_tiles = n_ctx // tq
    lat_blocks = batch * nq
    qw = GA_HEADS * HEAD_DIM
    kvw = GA_KV_HEADS * HEAD_DIM

    def qrow(b, i):
        return jnp.where(i < nq, b * nq + i, lat_blocks + b * n_ctx_tiles + (i - nq))

    kern = functools.partial(_global_attn_kernel, n_lat_tiles=nq)
    return pl.pallas_call(
        kern,
        grid=(batch, nq + (n_ctx_tiles if with_ctx else 0)),
        in_specs=[pl.BlockSpec((tq, qw), lambda b, i: (qrow(b, i), GQ // qw)),
                  pl.BlockSpec((seq, kvw), lambda b, i: (b, GK // kvw)),
                  pl.BlockSpec((seq, kvw), lambda b, i: (b, GV // kvw)),
                  pl.BlockSpec((n_ctx, kvw), lambda b, i: (batch * seq // n_ctx + b, GK // kvw)),
                  pl.BlockSpec((n_ctx, kvw), lambda b, i: (batch * seq // n_ctx + b, GV // kvw))],
        out_specs=pl.BlockSpec((tq, qw), lambda b, i: (qrow(b, i), 0)),
        out_shape=jax.ShapeDtypeStruct((m, qw), BF16),
        scratch_shapes=[pltpu.VMEM((GA_KV_HEADS, seq, 2 * HEAD_DIM), BF16),
                        pltpu.VMEM((GA_KV_HEADS, n_ctx, 2 * HEAD_DIM), BF16)],
        compiler_params=_cparams(("arbitrary", "arbitrary")),
        name="global_attention",
    )(u, u, u, u, u)


def _window_attn_kernel(sink_ref, q_ref, kx_ref, vx_ref, kc_ref, vc_ref, bias_ref, o_ref, vce_ref,
                        *, n_lat_tiles, seq):
    i = pl.program_id(1)
    span = CHUNK + 2 * WINDOW
    nt = (((1,), (1,)), ((), ()))

    @pl.when(i == 0)
    def _():
        for kh in range(WA_KV_HEADS):
            vce_ref[kh, :, 0:HEAD_DIM] = vc_ref[:, kh * HEAD_DIM:(kh + 1) * HEAD_DIM]
            vce_ref[kh, :, HEAD_DIM:] = jnp.ones((vc_ref.shape[0], HEAD_DIM), BF16)

    start = jnp.clip(i * CHUNK - WINDOW, 0, seq - span)
    start = pl.multiple_of(jnp.where(i < n_lat_tiles, start, 0), WINDOW)
    for kh in range(WA_KV_HEADS):
        kv = slice(kh * HEAD_DIM, (kh + 1) * HEAD_DIM)
        kw = kx_ref[pl.ds(start, span), kv]
        vw = vx_ref[pl.ds(start, span), kv]
        vwe = jnp.concatenate([vw, jnp.ones_like(vw)], axis=1)
        kc = kc_ref[:, kv]
        for h in range(kh * GROUP, (kh + 1) * GROUP):
            q = q_ref[:, h * HEAD_DIM:(h + 1) * HEAD_DIM]
            s_w = lax.dot_general(q, kw, nt, preferred_element_type=F32) + bias_ref[...]
            s_c = lax.dot_general(q, kc, nt, preferred_element_type=F32)
            sink = sink_ref[h]
            mx = jnp.maximum(jnp.maximum(jnp.max(s_w, axis=-1, keepdims=True),
                                         jnp.max(s_c, axis=-1, keepdims=True)), sink)
            p_w = jnp.exp2((s_w - mx).astype(BF16))
            p_c = jnp.exp2((s_c - mx).astype(BF16))
            pv = (jnp.dot(p_w, vwe, preferred_element_type=F32)
                  + jnp.dot(p_c, vce_ref[kh], preferred_element_type=F32))
            den = pv[:, HEAD_DIM:] + jnp.exp2(sink - mx)
            o_ref[:, h * HEAD_DIM:(h + 1) * HEAD_DIM] = (pv[:, :HEAD_DIM] / den).astype(BF16)


def _window_bias():
    span = CHUNK + 2 * WINDOW
    r = jnp.arange(CHUNK)[:, None]
    col = jnp.arange(span)[None, :]
    offs = (0, WINDOW, 2 * WINDOW)
    tiles = [jnp.where(jnp.abs(col - off - r) <= WINDOW, 0.0, MASK_BIAS) for off in offs]
    tiles.append(jnp.full((CHUNK, span), MASK_BIAS))
    return jnp.stack(tiles).astype(F32)


def _window_attention(u, sink, *, batch, seq, n_ctx, with_ctx):
    m = u.shape[0]
    nq = seq // CHUNK
    lat_blocks = batch * nq
    span = CHUNK + 2 * WINDOW
    assert seq >= span and nq >= 2

    def qrow(b, i):
        return jnp.where(i < nq, b * nq + i, lat_blocks + b)

    def variant(i):
        return jnp.where(i == 0, 0, jnp.where(i < nq - 1, 1, jnp.where(i == nq - 1, 2, 3)))

    qw = WA_HEADS * HEAD_DIM
    kvw = WA_KV_HEADS * HEAD_DIM
    kern = functools.partial(_window_attn_kernel, n_lat_tiles=nq, seq=seq)
    return pl.pallas_call(
        kern,
        grid=(batch, nq + (1 if with_ctx else 0)),
        in_specs=[pl.BlockSpec(memory_space=pltpu.SMEM),
                  pl.BlockSpec((CHUNK, qw), lambda b, i: (qrow(b, i), WQ // qw)),
                  pl.BlockSpec((seq, kvw), lambda b, i: (b, WK // kvw)),
                  pl.BlockSpec((seq, kvw), lambda b, i: (b, WV // kvw)),
                  pl.BlockSpec((n_ctx, kvw), lambda b, i: (batch * seq // n_ctx + b, WK // kvw)),
                  pl.BlockSpec((n_ctx, kvw), lambda b, i: (batch * seq // n_ctx + b, WV // kvw)),
                  pl.BlockSpec((None, CHUNK, span), lambda b, i: (variant(i), 0, 0))],
        out_specs=pl.BlockSpec((CHUNK, qw), lambda b, i: (qrow(b, i), 0)),
        out_shape=jax.ShapeDtypeStruct((m, qw), BF16),
        scratch_shapes=[pltpu.VMEM((WA_KV_HEADS, n_ctx, 2 * HEAD_DIM), BF16)],
        compiler_params=_cparams(("arbitrary", "arbitrary")),
        name="window_attention",
    )(sink.astype(F32) * LOG2_E, u, u, u, u, u, _window_bias())


def _merge_kernel(o0_ref, o1_ref, o2_ref, g0_ref, g1_ref, g2_ref, w0_ref, w1_ref, w2_ref, out_ref):
    tot = None
    for o_ref, g_ref, w_ref in ((o0_ref, g0_ref, w0_ref), (o1_ref, g1_ref, w1_ref),
                                (o2_ref, g2_ref, w2_ref)):
        term = jax.nn.sigmoid(g_ref[...].astype(F32)) * jnp.dot(
            o_ref[...], w_ref[...], preferred_element_type=F32)
        tot = term if tot is None else tot + term
    out_ref[...] = tot.astype(BF16)


def _merge(o_ret, o_ga, o_wa, u, w_branch, *, rows, tm, d):
    tn = 512
    bw = w_branch.shape[1]
    gate_blk = GATE // tn
    nd = d // tn

    def o_spec():
        return pl.BlockSpec((tm, bw), lambda i, j: (i, 0))

    def g_spec(br):
        return pl.BlockSpec((tm, tn), lambda i, j: (i, gate_blk + br * nd + j))

    def w_spec(br):
        return pl.BlockSpec((None, bw, tn), lambda i, j: (br, 0, j))

    return pl.pallas_call(
        _merge_kernel,
        grid=(rows // tm, nd),
        in_specs=[o_spec(), o_spec(), o_spec(), g_spec(0), g_spec(1), g_spec(2),
                  w_spec(0), w_spec(1), w_spec(2)],
        out_specs=pl.BlockSpec((tm, tn), lambda i, j: (i, j)),
        out_shape=jax.ShapeDtypeStruct((rows, d), BF16),
        compiler_params=_cparams(("arbitrary", "arbitrary")),
        name="branch_merge",
    )(o_ret, o_ga, o_wa, u, u, u, w_branch, w_branch, w_branch)


def _out_proj_kernel(m_ref, w_ref, x_ref, g_ref, o_ref):
    o_ref[...] = x_ref[...] + g_ref[...] * jnp.dot(m_ref[...], w_ref[...], preferred_element_type=F32)


def _out_proj(mix, w_out, x_all, mods3, *, rows, tm, d, mod_idx):
    tn = 1024 if d % 1024 == 0 else 512
    nd = d // tn
    return pl.pallas_call(
        _out_proj_kernel,
        grid=(rows // tm, nd),
        in_specs=[pl.BlockSpec((tm, d), lambda i, j: (i, 0)),
                  pl.BlockSpec((d, tn), lambda i, j: (0, j)),
                  pl.BlockSpec((tm, tn), lambda i, j: (i, j)),
                  pl.BlockSpec((None, 1, tn), lambda i, j: (mod_idx(i), 0, 2 * nd + j))],
        out_specs=pl.BlockSpec((tm, tn), lambda i, j: (i, j)),
        out_shape=jax.ShapeDtypeStruct((rows, d), F32),
        compiler_params=_cparams(("arbitrary", "arbitrary")),
        name="out_proj",
    )(mix, w_out, x_all, mods3)


def _router_kernel(x_ref, sh_ref, sc_ref, g_ref, wr_ref, br_ref, h_ref, idx_ref, wt_ref, *, n_experts):
    x = x_ref[...]
    y = x * lax.rsqrt(jnp.mean(x * x, axis=-1, keepdims=True) + NORM_EPS) * g_ref[...]
    h = y * (1.0 + sc_ref[...]) + sh_ref[...]
    h_hi = h.astype(BF16)
    h_ref[...] = h_hi
    h_lo = (h - h_hi.astype(F32)).astype(BF16)
    w = wr_ref[...]
    w_hi = w.astype(BF16)
    w_lo = (w - w_hi.astype(F32)).astype(BF16)
    logits = (jnp.dot(h_hi, w_hi, preferred_element_type=F32)
              + jnp.dot(h_lo, w_hi, preferred_element_type=F32)
              + jnp.dot(h_hi, w_lo, preferred_element_type=F32)) + br_ref[...]
    lane = lax.broadcasted_iota(jnp.int32, logits.shape, 1).astype(F32)
    vals, ids = [], []
    cur = logits
    for _ in range(TOP_K):
        mx = jnp.max(cur, axis=-1, keepdims=True)
        sel = jnp.min(jnp.where(cur == mx, lane, float(n_experts)), axis=-1, keepdims=True)
        vals.append(mx)
        ids.append(sel)
        cur = jnp.where(lane == sel, -jnp.inf, cur)
    e = [jnp.exp(v - vals[0]) for v in vals]
    den = e[0] + e[1] + e[2] + e[3]
    for k in range(TOP_K):
        idx_ref[:, k:k + 1] = ids[k].astype(jnp.int32)
        wt_ref[:, k:k + 1] = e[k] / den


def _router(x_new, mods3, ln_g, w_router, b_router, *, rows, tm, d, mod_idx):
    n_experts = w_router.shape[1]
    kern = functools.partial(_router_kernel, n_experts=n_experts)
    return pl.pallas_call(
        kern,
        grid=(rows // tm,),
        in_specs=[pl.BlockSpec((tm, d), lambda i: (i, 0)),
                  pl.BlockSpec((None, 1, d), lambda i: (mod_idx(i), 0, 3)),
                  pl.BlockSpec((None, 1, d), lambda i: (mod_idx(i), 0, 4)),
                  pl.BlockSpec((1, d), lambda i: (0, 0)),
                  pl.BlockSpec((d, n_experts), lambda i: (0, 0)),
                  pl.BlockSpec((1, n_experts), lambda i: (0, 0))],
        out_specs=[pl.BlockSpec((tm, d), lambda i: (i, 0)),
                   pl.BlockSpec((tm, TOP_K), lambda i: (i, 0)),
                   pl.BlockSpec((tm, TOP_K), lambda i: (i, 0))],
        out_shape=[jax.ShapeDtypeStruct((rows, d), BF16),
                   jax.ShapeDtypeStruct((rows, TOP_K), jnp.int32),
                   jax.ShapeDtypeStruct((rows, TOP_K), F32)],
        compiler_params=_cparams(("arbitrary",)),
        name="norm2_router",
    )(x_new, mods3, mods3, ln_g.reshape(1, d), w_router, b_router.reshape(1, n_experts))


def _moe_up_kernel(ie_ref, ic_ref, it_ref, in_ref, x_ref, wg_ref, wl_ref, bg_ref, bl_ref, o_ref):
    i = pl.program_id(0)

    def compute(n):
        x = x_ref[0:n, :]
        glu = jnp.dot(x, wg_ref[...].astype(BF16), preferred_element_type=F32) + bg_ref[...]
        lin = jnp.dot(x, wl_ref[...].astype(BF16), preferred_element_type=F32) + bl_ref[...]
        glu = jnp.minimum(glu, SWIGLU_LIMIT)
        lin = jnp.clip(lin, -SWIGLU_LIMIT, SWIGLU_LIMIT)
        o_ref[0:n, :] = (glu * jax.nn.sigmoid(SWIGLU_ALPHA * glu) * (lin + 1.0)).astype(BF16)

    tm = x_ref.shape[0]
    for quarters in range(1, MOE_TILE_PARTS + 1):
        n = quarters * tm // MOE_TILE_PARTS
        pl.when(in_ref[i] == n)(functools.partial(compute, n))


def _moe_up(sched, xs, w_gu, b_gu, *, layer, tm, tf):
    r_pad, d = xs.shape
    n_layers, n_experts, _, f2 = w_gu.shape
    f = f2 // 2
    nc = f // tf
    n_items = sched[0].shape[0]
    grid_spec = pltpu.PrefetchScalarGridSpec(
        num_scalar_prefetch=4,
        grid=(n_items,),
        in_specs=[pl.BlockSpec((tm, d), lambda i, ie, ic, it, nr: (it[i], 0)),
                  pl.BlockSpec((None, None, d, tf), lambda i, ie, ic, it, nr: (layer, ie[i], 0, ic[i])),
                  pl.BlockSpec((None, None, d, tf),
                               lambda i, ie, ic, it, nr: (layer, ie[i], 0, nc + ic[i])),
                  pl.BlockSpec((None, None, 1, tf), lambda i, ie, ic, it, nr: (layer, ie[i], 0, ic[i])),
                  pl.BlockSpec((None, None, 1, tf),
                               lambda i, ie, ic, it, nr: (layer, ie[i], 0, nc + ic[i]))],
        out_specs=pl.BlockSpec((tm, tf), lambda i, ie, ic, it, nr: (it[i], ic[i])))
    b4 = b_gu.reshape(n_layers, n_experts, 1, f2)
    return pl.pallas_call(
        _moe_up_kernel,
        grid_spec=grid_spec,
        out_shape=jax.ShapeDtypeStruct((r_pad, f), BF16),
        compiler_params=_cparams(("arbitrary",)),
        name="moe_up",
    )(*sched, xs, w_gu, w_gu, b4, b4)


def _moe_down_kernel(ie_ref, ic_ref, it_ref, in_ref, a_ref, w_ref, b_ref, o_ref):
    i = pl.program_id(0)

    def compute(n):
        y = jnp.dot(a_ref[0:n, :], w_ref[...].astype(BF16), preferred_element_type=F32) + b_ref[...]
        o_ref[0:n, :] = y.astype(BF16)

    tm = a_ref.shape[0]
    for quarters in range(1, MOE_TILE_PARTS + 1):
        n = quarters * tm // MOE_TILE_PARTS
        pl.when(in_ref[i] == n)(functools.partial(compute, n))


def _moe_down(sched, act, w_down, b_down, *, layer, tm, tn):
    r_pad, f = act.shape
    n_layers, n_experts, _, d = w_down.shape
    n_items = sched[0].shape[0]
    grid_spec = pltpu.PrefetchScalarGridSpec(
        num_scalar_prefetch=4,
        grid=(n_items,),
        in_specs=[pl.BlockSpec((tm, f), lambda i, ie, ic, it, nr: (it[i], 0)),
                  pl.BlockSpec((None, None, f, tn), lambda i, ie, ic, it, nr: (layer, ie[i], 0, ic[i])),
                  pl.BlockSpec((None, None, 1, tn), lambda i, ie, ic, it, nr: (layer, ie[i], 0, ic[i]))],
        out_specs=pl.BlockSpec((tm, tn), lambda i, ie, ic, it, nr: (it[i], ic[i])))
    return pl.pallas_call(
        _moe_down_kernel,
        grid_spec=grid_spec,
        out_shape=jax.ShapeDtypeStruct((r_pad, d), BF16),
        compiler_params=_cparams(("arbitrary",)),
        name="moe_down",
    )(*sched, act, w_down, b_down.reshape(n_layers, n_experts, 1, d))


def _moe_schedule(idx, *, n_experts, tm, n_chunks):
    n_tok = idx.shape[0]
    n_pairs = n_tok * TOP_K
    r_pad = n_pairs + n_experts * tm
    n_tiles = r_pad // tm
    part = tm // MOE_TILE_PARTS
    flat_e = idx.reshape(-1)
    experts = jnp.arange(n_experts, dtype=jnp.int32)
    onehot = (flat_e[:, None] == experts[None, :]).astype(jnp.int32)
    csum = jnp.cumsum(onehot, axis=0)
    rank = jnp.sum(csum * onehot, axis=1) - 1
    counts = csum[-1]
    ntiles = (counts + tm - 1) // tm
    tile_end = jnp.cumsum(ntiles)
    tile_start = tile_end - ntiles
    dest = jnp.sum(onehot * tile_start[None, :], axis=1) * tm + rank
    src_tok = (jnp.arange(r_pad, dtype=jnp.int32) % n_tok).at[dest].set(
        jnp.arange(n_pairs, dtype=jnp.int32) // TOP_K, unique_indices=True, mode="promise_in_bounds")
    n_used = tile_end[-1]

    def schedule(n_chunks):
        n_items = n_chunks * n_tiles
        item = jnp.arange(n_items, dtype=jnp.int32)
        valid = item < n_chunks * n_used
        ic_ = jnp.minimum(item, jnp.maximum(n_chunks * n_used - 1, 0))
        e = jnp.minimum(
            jnp.sum((ic_[:, None] >= n_chunks * tile_end[None, :]).astype(jnp.int32), axis=1),
            n_experts - 1)
        sel = (e[:, None] == experts[None, :]).astype(jnp.int32)
        nt_e = jnp.maximum(jnp.sum(sel * ntiles[None, :], axis=1), 1)
        ts_e = jnp.sum(sel * tile_start[None, :], axis=1)
        cnt_e = jnp.sum(sel * counts[None, :], axis=1)
        r = ic_ - n_chunks * ts_e
        c = r // nt_e
        lt = nt_e - 1 - r % nt_e
        t = ts_e + lt
        left = cnt_e - lt * tm
        nrows = jnp.where(valid, jnp.minimum((left + part - 1) // part * part, tm), 0)
        return (e.astype(jnp.int32), c.astype(jnp.int32), t.astype(jnp.int32),
                nrows.astype(jnp.int32))

    pos = dest.reshape(n_tok, TOP_K).T.reshape(-1)
    return [schedule(n) for n in n_chunks], src_tok, pos


def _combine_kernel(x_ref, y0_ref, y1_ref, y2_ref, y3_ref, w_ref, g_ref, ng_ref, sh_ref, sc_ref,
                    *o_refs, final):
    w = w_ref[...]
    y = None
    for k, y_ref in enumerate((y0_ref, y1_ref, y2_ref, y3_ref)):
        term = w[:, k:k + 1] * y_ref[...].astype(F32)
        y = term if y is None else y + term
    x = x_ref[...] + g_ref[...] * y
    normed = x * lax.rsqrt(jnp.mean(x * x, axis=-1, keepdims=True) + NORM_EPS) * ng_ref[...]
    if final:
        o_refs[0][...] = normed
    else:
        o_refs[0][...] = x
        o_refs[1][...] = (normed * (1.0 + sc_ref[...]) + sh_ref[...]).astype(BF16)


def _combine(x_new, ysg, wts, mods3, norm_g, mods3_next, *, rows, tm, d, mod_idx, final):
    assert TOP_K == 4
    kern = functools.partial(_combine_kernel, final=final)
    nblk = rows // tm

    def y_spec(k):
        return pl.BlockSpec((tm, d), lambda i: (k * nblk + i, 0))

    row_spec = pl.BlockSpec((tm, d), lambda i: (i, 0))
    out_specs = [row_spec] if final else [row_spec, row_spec]
    out_shape = [jax.ShapeDtypeStruct((rows, d), F32)]
    if not final:
        out_shape.append(jax.ShapeDtypeStruct((rows, d), BF16))
    return pl.pallas_call(
        kern,
        grid=(nblk,),
        in_specs=[row_spec,
                  y_spec(0), y_spec(1), y_spec(2), y_spec(3),
                  pl.BlockSpec((tm, TOP_K), lambda i: (i, 0)),
                  pl.BlockSpec((None, 1, d), lambda i: (mod_idx(i), 0, 5)),
                  pl.BlockSpec((1, d), lambda i: (0, 0)),
                  pl.BlockSpec((None, 1, d), lambda i: (mod_idx(i), 0, 0)),
                  pl.BlockSpec((None, 1, d), lambda i: (mod_idx(i), 0, 1))],
        out_specs=out_specs,
        out_shape=out_shape,
        compiler_params=_cparams(("arbitrary",)),
        name="moe_combine",
    )(x_new, ysg, ysg, ysg, ysg, wts, mods3, norm_g.reshape(1, d), mods3_next, mods3_next)


def _rope_tables(batch, seq, n_ctx):
    rows = seq // GRID_W
    row = jnp.repeat(jnp.arange(rows, dtype=F32), GRID_W)
    col = jnp.tile(jnp.arange(GRID_W, dtype=F32), rows)
    quarter = HEAD_DIM // 4
    inv_freq = ROPE_THETA ** (-jnp.arange(quarter, dtype=F32) / quarter)
    ang = jnp.concatenate([row[:, None] * inv_freq, col[:, None] * inv_freq], axis=-1)
    cos, sin = jnp.cos(ang), jnp.sin(ang)
    cos_x = jnp.tile(jnp.concatenate([cos, cos], axis=-1), (batch, 1))
    sin_x = jnp.tile(jnp.concatenate([-sin, sin], axis=-1), (batch, 1))
    cos_f = jnp.concatenate([cos_x, jnp.ones((batch * n_ctx, HEAD_DIM), F32)], axis=0)
    sin_f = jnp.concatenate([sin_x, jnp.zeros((batch * n_ctx, HEAD_DIM), F32)], axis=0)
    return cos_f, sin_f


def _column_params(q_gain, k_gain, ncol):
    hp = _head_perm()
    scale = HEAD_DIM ** -0.5
    k_scale = RET_DK ** -0.5
    cs = jnp.ones((ncol,), F32)
    cs = cs.at[RK:RK + RET_HEADS * RET_DK].set(k_scale)
    cs = cs.at[GQ:GQ + GA_HEADS * HEAD_DIM].set(jnp.tile(q_gain.astype(F32)[hp] * (scale * LOG2_E), GA_HEADS))
    cs = cs.at[GK:GK + GA_KV_HEADS * HEAD_DIM].set(jnp.tile(k_gain.astype(F32)[hp], GA_KV_HEADS))
    cs = cs.at[WQ:WQ + WA_HEADS * HEAD_DIM].set(scale * LOG2_E)
    rm = jnp.zeros((ncol,), F32).at[GQ:GQ + GA_HEADS * HEAD_DIM].set(1.0)
    rm = rm.at[GK:GK + GA_KV_HEADS * HEAD_DIM].set(1.0)
    return cs, rm


def kernel(x, c, ctx, c_ctx, ln1_g, ln2_g, w_ada, b_ada, w_in, ret_decay_logit, ga_q_gain,
           ga_k_gain, wa_sink, w_branch, w_out, w_router, b_router, w_gu, b_gu, w_down, b_down,
           final_g):
    batch, seq, d = x.shape
    n_ctx = ctx.shape[1]
    depth = w_ada.shape[0]
    n_experts = w_router.shape[2]
    ncol = w_in.shape[2]
    mx, mc = batch * seq, batch * n_ctx
    m = mx + mc
    tm = min(1024, mc)
    assert seq % tm == 0 and mc % tm == 0 and batch + 1 <= 8
    tm_stream = min(512, tm)
    assert w_gu.shape[3] // 2 == w_down.shape[3] == d

    def mod_idx_for(tile):
        def mod_idx(i):
            return jnp.where(i < mx // tile, i // (seq // tile), batch)
        return mod_idx

    mod_idx = mod_idx_for(tm)
    mod_idx_s = mod_idx_for(tm_stream)

    cc8 = jnp.zeros((8, d), F32).at[:batch].set(c).at[batch].set(c_ctx)
    mods = _modulation(cc8, w_ada, b_ada)
    cos_f, sin_f = _rope_tables(batch, seq, n_ctx)

    x_all = jnp.concatenate([x.reshape(mx, d), ctx.reshape(mc, d)], axis=0)
    e_tm = 512
    tf = min(1024, d)

    for l in range(depth):
        need_ctx = l < depth - 1
        rows = m if need_ctx else mx
        mods3 = mods[l].reshape(8, 1, 6 * d)
        w_in_p = _regroup_columns(w_in[l])
        colscale, rmsmask = _column_params(ga_q_gain[l], ga_k_gain[l], ncol)
        if l == 0:
            h1 = _prenorm(x_all, mods3, ln1_g[l], tm=tm_stream, mod_idx=mod_idx_s)
        u = _in_proj(h1, w_in_p, cos_f, sin_f, colscale, rmsmask)
        o_ret = _retention(u, _retention_tables(ret_decay_logit[l]), batch=batch, seq=seq, n_ctx=n_ctx)
        o_ga = _global_attention(u, batch=batch, seq=seq, n_ctx=n_ctx, with_ctx=need_ctx)
        o_wa = _window_attention(u, wa_sink[l], batch=batch, seq=seq, n_ctx=n_ctx, with_ctx=need_ctx)
        mix = _merge(o_ret, o_ga, o_wa, u, w_branch[l].astype(BF16), rows=rows, tm=tm, d=d)
        x_new = _out_proj(mix, w_out[l].astype(BF16), x_all, mods3, rows=rows, tm=tm, d=d, mod_idx=mod_idx)
        h2, idx, wts = _router(x_new, mods3, ln2_g[l], w_router[l], b_router[l],
                               rows=rows, tm=tm, d=d, mod_idx=mod_idx)
        (sched_up, sched_down), src_tok, pos = _moe_schedule(
            idx, n_experts=n_experts, tm=e_tm, n_chunks=(d // tf, 1))
        xs = h2.at[src_tok].get(mode="promise_in_bounds")
        act = _moe_up(sched_up, xs, w_gu, b_gu, layer=l, tm=e_tm, tf=tf)
        ys = _moe_down(sched_down, act, w_down, b_down, layer=l, tm=e_tm, tn=d)
        ysg = ys.at[pos].get(mode="promise_in_bounds")
        if need_ctx:
            x_all, h1 = _combine(x_new, ysg, wts, mods3, ln1_g[l + 1],
                                 mods[l + 1].reshape(8, 1, 6 * d), rows=rows, tm=tm_stream, d=d,
                                 mod_idx=mod_idx_s, final=False)
        else:
            (x_all,) = _combine(x_new, ysg, wts, mods3, final_g, mods3, rows=rows, tm=tm_stream,
                                d=d, mod_idx=mod_idx_s, final=True)
    return x_all[:mx].reshape(batch, seq, d)
```
